```python
import math
import jax, jax.numpy as jnp
from jax import lax
import numpy as np

D_MODEL = 1024
BATCH = 4
SEQ = 8192
DEPTH = 1
DEC_BATCH = 32
DEC_SEQ = 16
PAST_LEN = 1024

CHUNK = 64
Q_BLOCK = 128
EPS = 1e-5

DA_HEADS = 4
DA_HEAD_DIM = 64
DA_QK = DA_HEADS * 2 * DA_HEAD_DIM
DA_WIDTH = DA_HEADS * 2 * DA_HEAD_DIM
ROPE_THETA = 10000.0

SSM_HEADS = 8
SSM_HEAD_DIM = 64
SSM_WIDTH = SSM_HEADS * SSM_HEAD_DIM
SSM_GROUPS = 2
SSM_STATE = 128
CONV_W = 4
CONV_DIM = SSM_WIDTH + 2 * SSM_GROUPS * SSM_STATE

MIX_WIDTH = DA_WIDTH + SSM_WIDTH
Q_OFF = 0
K_OFF = Q_OFF + DA_QK
V_OFF = K_OFF + DA_QK
Z_OFF = V_OFF + DA_WIDTH
XBC_OFF = Z_OFF + SSM_WIDTH
DT_OFF = XBC_OFF + CONV_DIM
IN_COLS = DT_OFF + SSM_HEADS

N_EXPERTS = 64
N_EXPERT_GROUPS = 8
EXPERTS_PER_GROUP = N_EXPERTS // N_EXPERT_GROUPS
TOPK_GROUPS = 4
TOP_K = 8
EXPERT_DIM = 256
SHARED_DIM = 256
ROUTED_SCALE = 2.5
MOE_BLOCK = 512

ALPHA = (2 * DEPTH) ** 0.25
BETA = (8 * DEPTH) ** -0.25

kernel_name = 'chunk_causal_diffattn_ssd_moe_stream_step'

F32 = jnp.float32


def _ln_f32(x):
    xf = x.astype(F32)
    mu = jnp.mean(xf, -1, keepdims=True)
    var = jnp.mean(jnp.square(xf - mu), -1, keepdims=True)
    return (xf - mu) * lax.rsqrt(var + EPS)


def ln_plain(x):
    return _ln_f32(x).astype(x.dtype)


def ln_affine(x, g, b):
    return (_ln_f32(x) * g + b).astype(x.dtype)


def rms_norm(x):
    xf = x.astype(F32)
    return xf * lax.rsqrt(jnp.mean(jnp.square(xf), -1, keepdims=True) + EPS)


def rope(x, pos):
    s, d = x.shape[1], x.shape[-1]
    half = d // 2
    inv = ROPE_THETA ** (-jnp.arange(half, dtype=F32) / half)
    ang = pos.astype(F32)[:, None] * inv[None, :]
    shp = (1, s) + (1,) * (x.ndim - 3) + (half,)
    cos = jnp.cos(ang).reshape(shp)
    sin = jnp.sin(ang).reshape(shp)
    x1 = x[..., :half].astype(F32)
    x2 = x[..., half:].astype(F32)
    return jnp.concatenate([x1 * cos - x2 * sin, x1 * sin + x2 * cos], -1).astype(x.dtype)


def diff_attention(q, k, v, lam, mask):
    sc = jnp.einsum('bqhmd,bkhmd->bhmqk', q, k, preferred_element_type=F32) * (DA_HEAD_DIM ** -0.5)
    sc = jnp.where(mask, sc, -jnp.inf)
    pr = jax.nn.softmax(sc, axis=-1)
    a = pr[:, :, 0] - lam * pr[:, :, 1]
    return jnp.einsum('bhqk,bkhe->bqhe', a, v.astype(F32))


def diff_attention_blocked(q, k, v, lam):
    b, s = q.shape[:2]
    nb = s // Q_BLOCK
    qb = jnp.moveaxis(q.reshape((b, nb, Q_BLOCK) + q.shape[2:]), 1, 0)
    k_chunk = jnp.arange(s) // CHUNK

    def one(args):
        qi, i = args
        q_chunk = (i * Q_BLOCK + jnp.arange(Q_BLOCK)) // CHUNK
        return diff_attention(qi, k, v, lam, k_chunk[None, :] <= q_chunk[:, None])

    out = lax.map(one, (qb, jnp.arange(nb)))
    return jnp.moveaxis(out, 0, 1).reshape((b, s) + out.shape[3:])


def ssd_scan(x, dt, a, bm, cm, h0, chunk):
    b, s, nh, hp = x.shape
    nc = s // chunk
    g = bm.shape[2]
    r = nh // g
    x = x.astype(F32).reshape(b, nc, chunk, g, r, hp)
    dt = dt.astype(F32).reshape(b, nc, chunk, g, r)
    bm = bm.astype(F32).reshape(b, nc, chunk, g, -1)
    cm = cm.astype(F32).reshape(b, nc, chunk, g, -1)
    acs = jnp.cumsum(dt * a.reshape(g, r), axis=2)
    seg = acs[:, :, :, None] - acs[:, :, None, :]
    causal = jnp.tril(jnp.ones((chunk, chunk), bool))[:, :, None, None]
    lmat = jnp.exp(jnp.where(causal, seg, -jnp.inf))
    xdt = x * dt[..., None]
    cb = jnp.einsum('bcign,bcjgn->bcijg', cm, bm)
    y_diag = jnp.einsum('bcijg,bcijgr,bcjgrp->bcigrp', cb, lmat, xdt)
    decay = jnp.exp(acs[:, :, -1:] - acs)
    states = jnp.einsum('bcjgn,bcjgr,bcjgrp->bcgrpn', bm, decay, xdt)
    chunk_decay = jnp.exp(acs[:, :, -1])

    def step(hc, inp):
        st, dec = inp
        return hc * dec[..., None, None] + st, hc

    h_init = h0.astype(F32).reshape(b, g, r, hp, -1)
    h_last, h_prev = lax.scan(step, h_init, (jnp.moveaxis(states, 1, 0), jnp.moveaxis(chunk_decay, 1, 0)))
    h_prev = jnp.moveaxis(h_prev, 0, 1)
    y_off = jnp.einsum('bcign,bcgrpn,bcigr->bcigrp', cm, h_prev, jnp.exp(acs))
    y = (y_diag + y_off).reshape(b, s, nh, hp)
    return y, h_last.reshape(b, nh, hp, -1)


def ssd_mixer(z, xbc, dt_raw, conv_prev, h0, conv_w, conv_b, dt_bias, a_log, d_skip, norm_w, chunk):
    b, s, _ = xbc.shape
    xp = jnp.concatenate([conv_prev.astype(xbc.dtype), xbc], axis=1)
    conv = conv_b
    for j in range(CONV_W):
        conv = conv + xp[:, j:j + s] * conv_w[j]
    new_conv = xp[:, s:]
    xa = jax.nn.silu(conv)
    gn = SSM_GROUPS * SSM_STATE
    xs = xa[..., :SSM_WIDTH].reshape(b, s, SSM_HEADS, SSM_HEAD_DIM)
    bm = xa[..., SSM_WIDTH:SSM_WIDTH + gn].reshape(b, s, SSM_GROUPS, SSM_STATE)
    cm = xa[..., SSM_WIDTH + gn:].reshape(b, s, SSM_GROUPS, SSM_STATE)
    dt = jax.nn.softplus(dt_raw.astype(F32) + dt_bias)
    a = -jnp.exp(a_log.astype(F32))
    y, h_last = ssd_scan(xs, dt, a, bm, cm, h0, chunk)
    y = y + xs.astype(F32) * d_skip[:, None]
    y = y.reshape(b, s, SSM_WIDTH) * jax.nn.silu(z.astype(F32))
    y = rms_norm(y.reshape(b, s, SSM_GROUPS, -1)).reshape(b, s, SSM_WIDTH) * norm_w
    return y, new_conv, h_last


def moe_ffn(h, router_w, router_bias, e_gate, e_up, e_down, s_gate, s_up, s_down):
    b, s, d = h.shape
    n = b * s
    x = h.reshape(n, d)
    score = jax.nn.sigmoid((x @ router_w).astype(F32))
    biased = score + router_bias
    grp_score = lax.top_k(biased.reshape(n, N_EXPERT_GROUPS, EXPERTS_PER_GROUP), 2)[0].sum(-1)
    _, gidx = lax.top_k(grp_score, TOPK_GROUPS)
    gmask = jnp.any(gidx[..., None] == jnp.arange(N_EXPERT_GROUPS), axis=-2)
    emask = jnp.repeat(gmask, EXPERTS_PER_GROUP, axis=-1)
    _, eidx = lax.top_k(jnp.where(emask, biased, -jnp.inf), TOP_K)
    w = jnp.take_along_axis(score, eidx, axis=-1)
    w = w / jnp.sum(w, -1, keepdims=True) * ROUTED_SCALE
    gate = jnp.sum((eidx[..., None] == jnp.arange(N_EXPERTS)) * w[..., None], axis=-2)
    mb = min(MOE_BLOCK, n)
    pad = (-n) % mb
    xb_all = jnp.pad(x, ((0, pad), (0, 0))).reshape(-1, mb, d)
    gb_all = jnp.pad(gate, ((0, pad), (0, 0))).reshape(-1, mb, N_EXPERTS)

    def block(args):
        xb, gb = args
        hg = jnp.einsum('td,edf->tef', xb, e_gate)
        hu = jnp.einsum('td,edf->tef', xb, e_up)
        hid = (jax.nn.silu(hg.astype(F32)) * hu * gb[..., None]).astype(xb.dtype)
        return jnp.einsum('tef,efd->td', hid, e_down)

    routed = lax.map(block, (xb_all, gb_all)).reshape(-1, d)[:n]
    shared = (jax.nn.silu(x @ s_gate) * (x @ s_up)) @ s_down
    return (routed + shared).reshape(b, s, d).astype(h.dtype)


def trunk_layer(x, c, pos, prm, layer_idx, k_past, v_past, conv_prev, ssm_prev, sample):
    b, s, _ = x.shape
    mod = (jax.nn.silu(c) @ prm['w_ada'] + prm['b_ada'])[:, None, :]
    sh1, sc1, g1, sh2, sc2, g2 = jnp.split(mod, 6, axis=-1)
    h = ln_plain(x) * (1.0 + sc1) + sh1
    proj = h @ prm['w_in']
    q = rope(proj[..., Q_OFF:K_OFF].reshape(b, s, DA_HEADS, 2, DA_HEAD_DIM), pos)
    k = rope(proj[..., K_OFF:V_OFF].reshape(b, s, DA_HEADS, 2, DA_HEAD_DIM), pos)
    v = proj[..., V_OFF:Z_OFF].reshape(b, s, DA_HEADS, 2 * DA_HEAD_DIM)
    z = proj[..., Z_OFF:XBC_OFF]
    xbc = proj[..., XBC_OFF:DT_OFF]
    dt_raw = proj[..., DT_OFF:IN_COLS]
    lam_init = 0.8 - 0.6 * math.exp(-0.3 * layer_idx)
    lam = (jnp.exp(jnp.sum(prm['lambda_q1'].astype(F32) * prm['lambda_k1']))
           - jnp.exp(jnp.sum(prm['lambda_q2'].astype(F32) * prm['lambda_k2'])) + lam_init)
    if sample:
        k_all = jnp.concatenate([k_past.reshape(b, -1, DA_HEADS, 2, DA_HEAD_DIM).astype(k.dtype), k], axis=1)
        v_all = jnp.concatenate([v_past.astype(v.dtype), v], axis=1)
        att = diff_attention(q, k_all, v_all, lam, jnp.ones((s, k_all.shape[1]), bool))
        ssd_chunk = s
    else:
        att = diff_attention_blocked(q, k, v, lam)
        conv_prev = jnp.zeros((b, CONV_W - 1, CONV_DIM), x.dtype)
        ssm_prev = jnp.zeros((b, SSM_HEADS, SSM_HEAD_DIM, SSM_STATE), F32)
        ssd_chunk = CHUNK
    att = (rms_norm(att) * prm['subln_w'] * (1.0 - lam_init)).reshape(b, s, DA_WIDTH)
    ssm, new_conv, new_ssm = ssd_mixer(z, xbc, dt_raw, conv_prev, ssm_prev, prm['conv_w'], prm['conv_b'],
                                       prm['dt_bias'], prm['a_log'], prm['d_skip'], prm['ssm_norm_w'], ssd_chunk)
    mix = jnp.concatenate([att.astype(x.dtype), ssm.astype(x.dtype)], axis=-1) @ prm['w_out']
    x = ln_affine(ALPHA * x + (1.0 + g1) * mix, prm['ln1_g'], prm['ln1_b'])
    h = ln_plain(x) * (1.0 + sc2) + sh2
    ffn = moe_ffn(h, prm['router_w'], prm['router_bias'], prm['exp_w_gate'], prm['exp_w_up'],
                  prm['exp_w_down'], prm['sh_w_gate'], prm['sh_w_up'], prm['sh_w_down'])
    x = ln_affine(ALPHA * x + (1.0 + g2) * ffn, prm['ln2_g'], prm['ln2_b'])
    return x, k.reshape(b, s, DA_HEADS, 2 * DA_HEAD_DIM), v, new_conv, new_ssm


def setup_inputs(seed: int = 0) -> dict:
    key = jax.random.key(seed)
    ks = iter(jax.random.split(key, 48))

    def nrm(shape, scale):
        return jax.random.normal(next(ks), shape, F32) * scale

    L = DEPTH
    x_prompt = nrm((BATCH, SEQ, D_MODEL), 1.0)
    x_sample = nrm((DEC_BATCH, DEC_SEQ, D_MODEL), 1.0)
    c_prompt = nrm((BATCH, D_MODEL), 1.0)
    c_sample = nrm((DEC_BATCH, D_MODEL), 1.0)
    cache_k = nrm((L, DEC_BATCH, PAST_LEN, DA_HEADS, 2 * DA_HEAD_DIM), 1.0)
    cache_v = nrm((L, DEC_BATCH, PAST_LEN, DA_HEADS, 2 * DA_HEAD_DIM), BETA)
    state_conv = nrm((L, DEC_BATCH, CONV_W - 1, CONV_DIM), 1.0)
    state_ssm = nrm((L, DEC_BATCH, SSM_HEADS, SSM_HEAD_DIM, SSM_STATE), 0.5)
    w_ada = nrm((L, D_MODEL, 6 * D_MODEL), 0.2 * D_MODEL ** -0.5)
    b_ada = nrm((L, 6 * D_MODEL), 0.02)
    w_in = nrm((L, D_MODEL, IN_COLS), D_MODEL ** -0.5)
    w_in = w_in.at[:, :, V_OFF:Z_OFF].multiply(BETA)
    lambda_q1 = nrm((L, DA_HEAD_DIM), 0.1)
    lambda_k1 = nrm((L, DA_HEAD_DIM), 0.1)
    lambda_q2 = nrm((L, DA_HEAD_DIM), 0.1)
    lambda_k2 = nrm((L, DA_HEAD_DIM), 0.1)
    subln_w = 1.0 + nrm((L, 2 * DA_HEAD_DIM), 0.02)
    conv_w = nrm((L, CONV_W, CONV_DIM), CONV_W ** -0.5)
    conv_b = nrm((L, CONV_DIM), 0.02)
    u = jax.random.uniform(next(ks), (L, SSM_HEADS), F32)
    dt0 = jnp.exp(u * (math.log(0.1) - math.log(1e-3)) + math.log(1e-3))
    dt_bias = dt0 + jnp.log(-jnp.expm1(-dt0))
    a_log = jnp.log(jax.random.uniform(next(ks), (L, SSM_HEADS), F32, minval=1.0, maxval=16.0))
    d_skip = 1.0 + nrm((L, SSM_HEADS), 0.1)
    ssm_norm_w = 1.0 + nrm((L, SSM_WIDTH), 0.02)
    w_out = nrm((L, MIX_WIDTH, D_MODEL), BETA * MIX_WIDTH ** -0.5)
    ln1_g = 1.0 + nrm((L, D_MODEL), 0.02)
    ln1_b = nrm((L, D_MODEL), 0.02)
    router_w = nrm((L, D_MODEL, N_EXPERTS), D_MODEL ** -0.5)
    router_bias = nrm((L, N_EXPERTS), 0.01)
    exp_w_gate = nrm((L, N_EXPERTS, D_MODEL, EXPERT_DIM), D_MODEL ** -0.5)
    exp_w_up = nrm((L, N_EXPERTS, D_MODEL, EXPERT_DIM), D_MODEL ** -0.5)
    exp_w_down = nrm((L, N_EXPERTS, EXPERT_DIM, D_MODEL), BETA * EXPERT_DIM ** -0.5)
    sh_w_gate = nrm((L, D_MODEL, SHARED_DIM), D_MODEL ** -0.5)
    sh_w_up = nrm((L, D_MODEL, SHARED_DIM), D_MODEL ** -0.5)
    sh_w_down = nrm((L, SHARED_DIM, D_MODEL), BETA * SHARED_DIM ** -0.5)
    ln2_g = 1.0 + nrm((L, D_MODEL), 0.02)
    ln2_b = nrm((L, D_MODEL), 0.02)
    return {'x_prompt': x_prompt, 'x_sample': x_sample, 'c_prompt': c_prompt, 'c_sample': c_sample,
            'cache_k': cache_k, 'cache_v': cache_v, 'state_conv': state_conv, 'state_ssm': state_ssm,
            'w_ada': w_ada, 'b_ada': b_ada, 'w_in': w_in,
            'lambda_q1': lambda_q1, 'lambda_k1': lambda_k1, 'lambda_q2': lambda_q2, 'lambda_k2': lambda_k2,
            'subln_w': subln_w, 'conv_w': conv_w, 'conv_b': conv_b, 'dt_bias': dt_bias, 'a_log': a_log,
            'd_skip': d_skip, 'ssm_norm_w': ssm_norm_w, 'w_out': w_out, 'ln1_g': ln1_g, 'ln1_b': ln1_b,
            'router_w': router_w, 'router_bias': router_bias, 'exp_w_gate': exp_w_gate, 'exp_w_up': exp_w_up,
            'exp_w_down': exp_w_down, 'sh_w_gate': sh_w_gate, 'sh_w_up': sh_w_up, 'sh_w_down': sh_w_down,
            'ln2_g': ln2_g, 'ln2_b': ln2_b}


def reference(x_prompt, x_sample, c_prompt, c_sample, cache_k, cache_v, state_conv, state_ssm,
              w_ada, b_ada, w_in, lambda_q1, lambda_k1, lambda_q2, lambda_k2, subln_w, conv_w, conv_b,
              dt_bias, a_log, d_skip, ssm_norm_w, w_out, ln1_g, ln1_b, router_w, router_bias,
              exp_w_gate, exp_w_up, exp_w_down, sh_w_gate, sh_w_up, sh_w_down, ln2_g, ln2_b):
    pos_prompt = jnp.arange(x_prompt.shape[1])
    pos_sample = PAST_LEN + jnp.arange(x_sample.shape[1])
    yp, ys = x_prompt, x_sample
    kp_l, vp_l, cp_l, sp_l, ks_l, vs_l, cs_l, ss_l = [], [], [], [], [], [], [], []
    for l in range(DEPTH):
        prm = {'w_ada': w_ada[l], 'b_ada': b_ada[l], 'w_in': w_in[l],
               'lambda_q1': lambda_q1[l], 'lambda_k1': lambda_k1[l],
               'lambda_q2': lambda_q2[l], 'lambda_k2': lambda_k2[l], 'subln_w': subln_w[l],
               'conv_w': conv_w[l], 'conv_b': conv_b[l], 'dt_bias': dt_bias[l], 'a_log': a_log[l],
               'd_skip': d_skip[l], 'ssm_norm_w': ssm_norm_w[l], 'w_out': w_out[l],
               'ln1_g': ln1_g[l], 'ln1_b': ln1_b[l], 'router_w': router_w[l], 'router_bias': router_bias[l],
               'exp_w_gate': exp_w_gate[l], 'exp_w_up': exp_w_up[l], 'exp_w_down': exp_w_down[l],
               'sh_w_gate': sh_w_gate[l], 'sh_w_up': sh_w_up[l], 'sh_w_down': sh_w_down[l],
               'ln2_g': ln2_g[l], 'ln2_b': ln2_b[l]}
        yp, kp, vp, cp, sp = trunk_layer(yp, c_prompt, pos_prompt, prm, l, None, None, None, None, False)
        ys, kn, vn, cn, sn = trunk_layer(ys, c_sample, pos_sample, prm, l, cache_k[l], cache_v[l],
                                         state_conv[l], state_ssm[l], True)
        kp_l.append(kp); vp_l.append(vp); cp_l.append(cp); sp_l.append(sp)
        ks_l.append(kn); vs_l.append(vn); cs_l.append(cn); ss_l.append(sn)
    k_prompt = jnp.stack(kp_l)
    v_prompt = jnp.stack(vp_l)
    conv_prompt = jnp.stack(cp_l)
    ssm_prompt = jnp.stack(sp_l)
    k_sample = jnp.stack(ks_l)
    v_sample = jnp.stack(vs_l)
    conv_sample = jnp.stack(cs_l)
    ssm_sample = jnp.stack(ss_l)
    return (yp, ys, k_prompt, v_prompt, conv_prompt, ssm_prompt, k_sample, v_sample, conv_sample, ssm_sample)
```

```python
import functools
import math

import jax
import jax.numpy as jnp
from jax import lax
from jax.experimental import pallas as pl
from jax.experimental.pallas import tpu as pltpu

F32 = jnp.float32
BF16 = jnp.bfloat16

D_MODEL = 1024
CHUNK = 64
CHUNK_SHIFT = 6
HEAD_SHIFT = 6
EPS = 1e-5
DA_HEADS = 4
DA_HEAD_DIM = 64
DA_WIDTH = DA_HEADS * 2 * DA_HEAD_DIM
ROPE_THETA = 10000.0
SSM_HEADS = 8
SSM_HEAD_DIM = 64
SSM_WIDTH = SSM_HEADS * SSM_HEAD_DIM
SSM_GROUPS = 2
SSM_STATE = 128
CONV_W = 4
CONV_DIM = SSM_WIDTH + 2 * SSM_GROUPS * SSM_STATE
Q_OFF = 0
K_OFF = 512
V_OFF = 1024
Z_OFF = 1536
XBC_OFF = 2048
DT_OFF = 3072
N_EXPERTS = 64
N_EXPERT_GROUPS = 8
EXPERTS_PER_GROUP = 8
TOPK_GROUPS = 4
TOP_K = 8
EXPERT_DIM = 256
ROUTED_SCALE = 2.5

LANES = 128
VMEM_LIMIT = 56 * 1024 * 1024


def _cparams(*sem):
    return pltpu.CompilerParams(dimension_semantics=sem, vmem_limit_bytes=VMEM_LIMIT)


def _dot(a, b):
    return jnp.dot(a, b, preferred_element_type=F32)


def _dot_nt(a, b):
    return lax.dot_general(a, b, (((1,), (1,)), ((), ())), preferred_element_type=F32)


def _split2(a):
    hi = a.astype(BF16)
    lo = (a - hi.astype(F32)).astype(BF16)
    return hi, lo


def _split3(a):
    hi = a.astype(BF16)
    r = a - hi.astype(F32)
    mid = r.astype(BF16)
    lo = (r - mid.astype(F32)).astype(BF16)
    return hi, mid, lo


def _dot3(a, b, dot=_dot):
    ah, al = _split2(a)
    bh, bl = _split2(b)
    return dot(ah, bh) + (dot(ah, bl) + dot(al, bh))


def _dot_exact_lhs(e, a, dot=_dot):
    ah, am, al = _split3(a)
    return dot(e, ah) + (dot(e, am) + dot(e, al))


def _silu(x):
    return x * jax.nn.sigmoid(x)


def _softplus(x):
    return jnp.maximum(x, 0.0) + jnp.log1p(jnp.exp(-jnp.abs(x)))


def _ln(x):
    mu = jnp.mean(x, -1, keepdims=True)
    xc = x - mu
    var = jnp.mean(xc * xc, -1, keepdims=True)
    return xc * lax.rsqrt(var + EPS)


def _ada_kernel(c_ref, w_ref, b_ref, o_ref):
    o_ref[...] = _dot3(_silu(c_ref[...]), w_ref[...]) + b_ref[...]


def _ada(c, w_ada, b_ada):
    rows = c.shape[0]
    n = w_ada.shape[1]
    tn = 1024
    return pl.pallas_call(
        _ada_kernel,
        grid=(n // tn,),
        in_specs=[pl.BlockSpec((rows, D_MODEL), lambda i: (0, 0)),
                  pl.BlockSpec((D_MODEL, tn), lambda i: (0, i)),
                  pl.BlockSpec((1, tn), lambda i: (0, i))],
        out_specs=pl.BlockSpec((rows, tn), lambda i: (0, i)),
        out_shape=jax.ShapeDtypeStruct((rows, n), F32),
        compiler_params=_cparams("arbitrary"),
        name="ada",
    )(c, w_ada, b_ada.reshape(1, n))


def _proj_kernel(x_ref, sc_ref, sh_ref, cos_ref, sin_ref, w_ref, wdt_ref,
                 q_ref, k_ref, kb_ref, v_ref, vb_ref, z_ref, xbc_ref, dt_ref, *, bb, ts):
    tm = bb * ts
    h = _ln(x_ref[...]) * (1.0 + sc_ref[...]) + sh_ref[...]
    h = h.reshape(tm, D_MODEL)
    hb = h.astype(BF16)

    cos = jnp.tile(cos_ref[...], (1, DA_WIDTH // LANES))
    sin = jnp.tile(sin_ref[...], (1, DA_WIDTH // LANES))
    lane = lax.broadcasted_iota(jnp.int32, (tm, DA_WIDTH), 1)
    first = (lane & (DA_HEAD_DIM - 1)) < (DA_HEAD_DIM // 2)

    def rope(t):
        rot = jnp.where(first, pltpu.roll(t, DA_WIDTH - DA_HEAD_DIM // 2, 1),
                        pltpu.roll(t, DA_HEAD_DIM // 2, 1))
        return t * cos + rot * sin

    q = rope(_dot(hb, w_ref[:, Q_OFF:K_OFF])) * (DA_HEAD_DIM ** -0.5)
    q_ref[...] = q.reshape(bb, ts, DA_WIDTH).astype(BF16)
    k = rope(_dot(hb, w_ref[:, K_OFF:V_OFF])).reshape(bb, ts, DA_WIDTH)
    k_ref[...] = k
    kb_ref[...] = k.astype(BF16)
    v = _dot(hb, w_ref[:, V_OFF:Z_OFF]).reshape(bb, ts, DA_WIDTH)
    v_ref[...] = v
    vb_ref[...] = v.astype(BF16)
    z_ref[...] = _dot(hb, w_ref[:, Z_OFF:XBC_OFF]).reshape(bb, ts, SSM_WIDTH)
    xbc_ref[...] = _dot(hb, w_ref[:, XBC_OFF:DT_OFF]).reshape(bb, ts, CONV_DIM)
    dt_ref[...] = _dot3(h, wdt_ref[...]).reshape(bb, ts, LANES)


def _proj(x, mod4, cos_t, sin_t, w_main, w_dt, *, bb, ts):
    B, S, _ = x.shape
    tm = bb * ts
    grid = (S // ts, B // bb)
    row3 = lambda w: pl.BlockSpec((bb, ts, w), lambda i, b: (b, i, 0))
    modspec = lambda which: pl.BlockSpec((bb, None, 1, D_MODEL), lambda i, b: (b, which, 0, 0))
    out_shapes = [jax.ShapeDtypeStruct((B, S, DA_WIDTH), BF16),
                  jax.ShapeDtypeStruct((B, S, DA_WIDTH), F32),
                  jax.ShapeDtypeStruct((B, S, DA_WIDTH), BF16),
                  jax.ShapeDtypeStruct((B, S, DA_WIDTH), F32),
                  jax.ShapeDtypeStruct((B, S, DA_WIDTH), BF16),
                  jax.ShapeDtypeStruct((B, S, SSM_WIDTH), F32),
                  jax.ShapeDtypeStruct((B, S, CONV_DIM), F32),
                  jax.ShapeDtypeStruct((B, S, LANES), F32)]
    out_specs = [row3(DA_WIDTH)] * 5 + [row3(SSM_WIDTH), row3(CONV_DIM), row3(LANES)]
    return pl.pallas_call(
        functools.partial(_proj_kernel, bb=bb, ts=ts),
        grid=grid,
        in_specs=[row3(D_MODEL), modspec(1), modspec(0),
                  pl.BlockSpec((tm, LANES), lambda i, b: (i, 0)),
                  pl.BlockSpec((tm, LANES), lambda i, b: (i, 0)),
                  pl.BlockSpec((D_MODEL, DT_OFF), lambda i, b: (0, 0)),
                  pl.BlockSpec((D_MODEL, LANES), lambda i, b: (0, 0))],
        out_specs=out_specs,
        out_shape=out_shapes,
        compiler_params=_cparams("arbitrary", "arbitrary"),
        name="proj",
    )(x, mod4, mod4, cos_t, sin_t, w_main, w_dt)


def _lambda(lq1, lk1, lq2, lk2, lam_init):
    s1 = jnp.sum(lq1[...] * lk1[...], axis=1, keepdims=True)
    s2 = jnp.sum(lq2[...] * lk2[...], axis=1, keepdims=True)
    return jnp.exp(s1) - jnp.exp(s2) + lam_init


def _subln(o, subw, lam_init):
    ms = jnp.mean(o * o, -1, keepdims=True)
    return o * lax.rsqrt(ms + EPS) * subw * (1.0 - lam_init)


def _attn_prompt_kernel(lq1, lk1, lq2, lk2, subw_ref, q_ref, k_ref, v_ref, o_ref, acc1, acc2,
                        *, tq, lam_init):
    tk = tq
    i = pl.program_id(2)
    q = q_ref[0]
    lane = lax.broadcasted_iota(jnp.int32, (tq, LANES), 1)
    zero = jnp.zeros_like(q)
    q1 = jnp.where(lane < DA_HEAD_DIM, q, zero)
    q2 = jnp.where(lane >= DA_HEAD_DIM, q, zero)
    acc1[...] = jnp.zeros_like(acc1)
    acc2[...] = jnp.zeros_like(acc2)

    def update(s, m, l, acc, v):
        mn = jnp.maximum(m, jnp.max(s, axis=-1, keepdims=True))
        a = jnp.exp(m - mn)
        p = jnp.exp(s - mn)
        l = a * l + jnp.sum(p, axis=-1, keepdims=True)
        acc[...] = a * acc[...] + _dot(p.astype(BF16), v)
        return mn, l

    def step(j, carry, diag):
        m1, l1, m2, l2 = carry
        off = pl.multiple_of(j * tk, tk)
        k = k_ref[0, pl.ds(off, tk), :]
        v = v_ref[0, pl.ds(off, tk), :]
        s1 = _dot_nt(q1, k)
        s2 = _dot_nt(q2, k)
        if diag:
            r = lax.broadcasted_iota(jnp.int32, (tq, tk), 0) >> CHUNK_SHIFT
            c = lax.broadcasted_iota(jnp.int32, (tq, tk), 1) >> CHUNK_SHIFT
            keep = c <= r
            s1 = jnp.where(keep, s1, -jnp.inf)
            s2 = jnp.where(keep, s2, -jnp.inf)
        m1, l1 = update(s1, m1, l1, acc1, v)
        m2, l2 = update(s2, m2, l2, acc2, v)
        return m1, l1, m2, l2

    ninf = jnp.full((tq, 1), -jnp.inf, F32)
    zcol = jnp.zeros((tq, 1), F32)
    carry = lax.fori_loop(0, i, lambda j, c: step(j, c, False), (ninf, zcol, ninf, zcol))
    _, l1, _, l2 = step(i, carry, True)
    lam = _lambda(lq1, lk1, lq2, lk2, lam_init)
    o = acc1[...] / l1 - lam * (acc2[...] / l2)
    o_ref[0] = _subln(o, subw_ref[...], lam_init).astype(BF16)


def _attn_prompt(qb, kb, vb, lam_vecs, subw, lam_init, *, tq):
    B, S, _ = qb.shape
    small = pl.BlockSpec((1, DA_HEAD_DIM), lambda b, h, i: (0, 0))
    return pl.pallas_call(
        functools.partial(_attn_prompt_kernel, tq=tq, lam_init=lam_init),
        grid=(B, DA_HEADS, S // tq),
        in_specs=[small, small, small, small,
                  pl.BlockSpec((1, LANES), lambda b, h, i: (0, 0)),
                  pl.BlockSpec((1, tq, LANES), lambda b, h, i: (b, i, h)),
                  pl.BlockSpec((1, S, LANES), lambda b, h, i: (b, 0, h)),
                  pl.BlockSpec((1, S, LANES), lambda b, h, i: (b, 0, h))],
        out_specs=pl.BlockSpec((1, tq, LANES), lambda b, h, i: (b, i, h)),
        out_shape=jax.ShapeDtypeStruct((B, S, DA_WIDTH), BF16),
        scratch_shapes=[pltpu.VMEM((tq, LANES), F32), pltpu.VMEM((tq, LANES), F32)],
        compiler_params=_cparams("arbitrary", "arbitrary", "arbitrary"),
        name="attn_prompt",
    )(*lam_vecs, subw, qb, kb, vb)


def _attn_sample_kernel(lq1, lk1, lq2, lk2, subw_ref, q_ref, kn_ref, vn_ref, kp_ref, vp_ref, o_ref,
                        *, lam_init):
    s = q_ref.shape[1]
    lam = _lambda(lq1, lk1, lq2, lk2, lam_init)
    lane = lax.broadcasted_iota(jnp.int32, (s, LANES), 1)
    outs = []
    for h in range(DA_HEADS):
        cols = slice(h * LANES, (h + 1) * LANES)
        q = q_ref[0, :, cols]
        zero = jnp.zeros_like(q)
        kn = kn_ref[0, :, cols]
        vn = vn_ref[0, :, cols]
        kp = kp_ref[0, :, cols].astype(BF16)
        vp = vp_ref[0, :, cols].astype(BF16)
        probs = []
        for qm in (jnp.where(lane < DA_HEAD_DIM, q, zero), jnp.where(lane >= DA_HEAD_DIM, q, zero)):
            sp = _dot_nt(qm, kp)
            sn = _dot_nt(qm, kn)
            m = jnp.maximum(jnp.max(sp, -1, keepdims=True), jnp.max(sn, -1, keepdims=True))
            pp = jnp.exp(sp - m)
            pn = jnp.exp(sn - m)
            l = jnp.sum(pp, -1, keepdims=True) + jnp.sum(pn, -1, keepdims=True)
            probs.append((pp / l, pn / l))
        ap = probs[0][0] - lam * probs[1][0]
        an = probs[0][1] - lam * probs[1][1]
        o = _dot(ap.astype(BF16), vp) + _dot(an.astype(BF16), vn)
        outs.append(_subln(o, subw_ref[...], lam_init))
    o_ref[0] = jnp.concatenate(outs, axis=-1).astype(BF16)


def _attn_sample(qb, kb, vb, k_past, v_past, lam_vecs, subw, lam_init):
    B, s, _ = qb.shape
    past = k_past.shape[1]
    small = pl.BlockSpec((1, DA_HEAD_DIM), lambda b: (0, 0))
    new = pl.BlockSpec((1, s, DA_WIDTH), lambda b: (b, 0, 0))
    old = pl.BlockSpec((1, past, DA_WIDTH), lambda b: (b, 0, 0))
    return pl.pallas_call(
        functools.partial(_attn_sample_kernel, lam_init=lam_init),
        grid=(B,),
        in_specs=[small, small, small, small, pl.BlockSpec((1, LANES), lambda b: (0, 0)),
                  new, new, new, old, old],
        out_specs=new,
        out_shape=jax.ShapeDtypeStruct((B, s, DA_WIDTH), BF16),
        compiler_params=_cparams("arbitrary"),
        name="attn_sample",
    )(*lam_vecs, subw, qb, kb, vb, k_past, v_past)


def _ssd_kernel(xbc_ref, z_ref, dt_ref, cp_ref, h0_ref, cw_ref, cb_ref, dtb_ref, alog_ref, dsk_ref, nw_ref,
                y_ref, hout_ref, tail_ref, st_ref, *, t_in, T):
    j = pl.program_id(1)
    PAIR = 2 * SSM_HEAD_DIM

    @pl.when(j == 0)
    def _():
        tail_ref[...] = cp_ref[0]
        st_ref[...] = h0_ref[0]

    def rows(ref, width):
        val = ref[0]
        if T == t_in:
            return val
        return jnp.concatenate([val, jnp.zeros((T - t_in, width), F32)], axis=0)

    xb = rows(xbc_ref, CONV_DIM)
    xp = jnp.concatenate([tail_ref[...], xb], axis=0)
    conv = cb_ref[...] + cw_ref[CONV_W - 1:CONV_W, :] * xb
    for sft in range(1, CONV_W):
        conv = conv + cw_ref[CONV_W - 1 - sft:CONV_W - sft, :] * pltpu.roll(xp, sft, 0)[8:, :]
    if T == t_in:
        tail_ref[...] = xb[T - 8:, :]
    xa = _silu(conv)
    xs = xa[:, :SSM_WIDTH]

    dt = _softplus(rows(dt_ref, LANES) + dtb_ref[...])
    if T != t_in:
        valid = lax.broadcasted_iota(jnp.int32, (T, LANES), 0) < t_in
        dt = jnp.where(valid, dt, 0.0)
    da = dt * (-jnp.exp(alog_ref[...]))

    rr = lax.broadcasted_iota(jnp.int32, (T, T), 0)
    cc = lax.broadcasted_iota(jnp.int32, (T, T), 1)
    causal = cc <= rr
    tril = jnp.where(causal, 1.0, 0.0).astype(BF16)
    acs = _dot_exact_lhs(tril, da)
    e16 = jnp.where(lax.broadcasted_iota(jnp.int32, (16, LANES), 0)
                    == lax.broadcasted_iota(jnp.int32, (16, LANES), 1), 1.0, 0.0).astype(BF16)
    acs_t = _dot_exact_lhs(e16, acs, dot=_dot_nt)

    last = acs[T - 1:T, :]
    expand = jnp.where((lax.broadcasted_iota(jnp.int32, (LANES, SSM_WIDTH), 1) >> HEAD_SHIFT)
                       == lax.broadcasted_iota(jnp.int32, (LANES, SSM_WIDTH), 0), 1.0, 0.0).astype(BF16)

    def per_head_lanes(v):
        hi, lo = _split2(v)
        return _dot(hi, expand) + _dot(lo, expand)

    xdt = xs * per_head_lanes(dt)
    xdd = xdt * per_head_lanes(jnp.exp(last - acs))
    eacs = per_head_lanes(jnp.exp(acs))

    head_of_row = lax.broadcasted_iota(jnp.int32, (SSM_WIDTH, LANES), 0) >> HEAD_SHIFT
    sel = head_of_row == lax.broadcasted_iota(jnp.int32, (SSM_WIDTH, LANES), 1)
    rowdec = jnp.sum(jnp.where(sel, jnp.exp(last), 0.0), axis=1, keepdims=True)

    lane = lax.broadcasted_iota(jnp.int32, (T, PAIR), 1)
    ys = []
    for p in range(SSM_HEADS // 2):
        g = (2 * p) // (SSM_HEADS // SSM_GROUPS)
        bm = xa[:, SSM_WIDTH + g * SSM_STATE:SSM_WIDTH + (g + 1) * SSM_STATE].astype(BF16)
        cm = xa[:, SSM_WIDTH + (SSM_GROUPS + g) * SSM_STATE:
                SSM_WIDTH + (SSM_GROUPS + g + 1) * SSM_STATE].astype(BF16)
        cb = _dot_nt(cm, bm)
        cols = slice(p * PAIR, (p + 1) * PAIR)
        xdt_p = xdt[:, cols].astype(BF16)
        yd = []
        for hh in (2 * p, 2 * p + 1):
            seg = acs[:, hh:hh + 1] - acs_t[hh:hh + 1, :]
            lmat = jnp.exp(jnp.where(causal, seg, -jnp.inf))
            yd.append(_dot((cb * lmat).astype(BF16), xdt_p))
        y_diag = jnp.where(lane < SSM_HEAD_DIM, yd[0], yd[1])
        st = st_ref[cols, :]
        y_off = _dot_nt(cm, st.astype(BF16)) * eacs[:, cols]
        ys.append(y_diag + y_off)
        st_ref[cols, :] = st * rowdec[cols, :] + _dot(xdd[:, cols].T.astype(BF16), bm)
    y = jnp.concatenate(ys, axis=-1) + xs * dsk_ref[...]
    y = y * _silu(rows(z_ref, SSM_WIDTH))
    half = SSM_WIDTH // SSM_GROUPS
    outs = []
    for g in range(SSM_GROUPS):
        yg = y[:, g * half:(g + 1) * half]
        outs.append(yg * lax.rsqrt(jnp.mean(yg * yg, -1, keepdims=True) + EPS))
    y = jnp.concatenate(outs, axis=-1) * nw_ref[...]
    y_ref[0] = y[:t_in, :].astype(BF16)

    @pl.when(j == pl.num_programs(1) - 1)
    def _():
        hout_ref[0] = st_ref[...]


def _ssd(xbc, z, dt_raw, conv_prev8, h0, conv_w, conv_b, dtb, alog, dsk, nw, *, t_in, T):
    B, S, _ = xbc.shape
    row = lambda w: pl.BlockSpec((1, t_in, w), lambda b, j: (b, j, 0))
    per_b = lambda r, w: pl.BlockSpec((1, r, w), lambda b, j: (b, 0, 0))
    const = lambda r, w: pl.BlockSpec((r, w), lambda b, j: (0, 0))
    return pl.pallas_call(
        functools.partial(_ssd_kernel, t_in=t_in, T=T),
        grid=(B, S // t_in),
        in_specs=[row(CONV_DIM), row(SSM_WIDTH), row(LANES),
                  per_b(8, CONV_DIM), per_b(SSM_WIDTH, SSM_STATE),
                  const(CONV_W, CONV_DIM), const(1, CONV_DIM), const(1, LANES), const(1, LANES),
                  const(1, SSM_WIDTH), const(1, SSM_WIDTH)],
        out_specs=[row(SSM_WIDTH), per_b(SSM_WIDTH, SSM_STATE)],
        out_shape=[jax.ShapeDtypeStruct((B, S, SSM_WIDTH), BF16),
                   jax.ShapeDtypeStruct((B, SSM_WIDTH, SSM_STATE), F32)],
        scratch_shapes=[pltpu.VMEM((8, CONV_DIM), F32), pltpu.VMEM((SSM_WIDTH, SSM_STATE), F32)],
        compiler_params=_cparams("arbitrary", "arbitrary"),
        name="ssd",
    )(xbc, z, dt_raw, conv_prev8, h0, conv_w, conv_b, dtb, alog, dsk, nw)


def _router(h, rwt_ref, rb_ref):
    tm = h.shape[0]
    G = N_EXPERT_GROUPS
    logits = _dot3(rwt_ref[...], h, dot=_dot_nt)
    score = jax.nn.sigmoid(logits[:N_EXPERTS, :])
    biased = score + rb_ref[:N_EXPERTS, :]
    sc = [score[G * j:G * (j + 1), :] for j in range(EXPERTS_PER_GROUP)]
    v = [biased[G * j:G * (j + 1), :] for j in range(EXPERTS_PER_GROUP)]

    grp = jnp.zeros((G, tm), F32)
    for j in range(EXPERTS_PER_GROUP):
        cnt = jnp.zeros((G, tm), jnp.int32)
        for j2 in range(EXPERTS_PER_GROUP):
            if j2 == j:
                continue
            beats = (v[j2] >= v[j]) if j2 < j else (v[j2] > v[j])
            cnt = cnt + beats.astype(jnp.int32)
        grp = grp + jnp.where(cnt < 2, v[j], 0.0)

    gid = lax.broadcasted_iota(jnp.int32, (G, tm), 0)
    cnt = jnp.zeros((G, tm), jnp.int32)
    for g2 in range(G):
        row = grp[g2:g2 + 1, :]
        beats = (row > grp) | ((row == grp) & (gid > g2))
        cnt = cnt + beats.astype(jnp.int32)
    gmask = cnt < TOPK_GROUPS

    mv = [jnp.where(gmask, v[j], -jnp.inf) for j in range(EXPERTS_PER_GROUP)]
    cnts = [jnp.zeros((G, tm), jnp.int32) for _ in range(EXPERTS_PER_GROUP)]
    for j2 in range(EXPERTS_PER_GROUP):
        for g2 in range(G):
            row = mv[j2][g2:g2 + 1, :]
            for j in range(EXPERTS_PER_GROUP):
                earlier = (gid > g2) | (gid == g2) if j2 < j else (gid > g2)
                beats = (row > mv[j]) | ((row == mv[j]) & earlier)
                cnts[j] = cnts[j] + beats.astype(jnp.int32)
    w = [jnp.where(cnts[j] < TOP_K, sc[j], 0.0) for j in range(EXPERTS_PER_GROUP)]
    tot = w[0]
    for j in range(1, EXPERTS_PER_GROUP):
        tot = tot + w[j]
    tot = jnp.sum(tot, axis=0, keepdims=True)
    gates = [w[j] / tot * ROUTED_SCALE for j in range(EXPERTS_PER_GROUP)]
    return jnp.concatenate(gates + [jnp.zeros((LANES - N_EXPERTS, tm), F32)], axis=0)


def _mix_kernel(x_ref, att_ref, ssm_ref, g1_ref, sc2_ref, sh2_ref, wo_ref, l1g_ref, l1b_ref, rwt_ref, rb_ref,
                x1_ref, h2_ref, gate_ref, *, bb, ts, alpha):
    tm = bb * ts
    att = att_ref[...].reshape(tm, DA_WIDTH)
    ssm = ssm_ref[...].reshape(tm, SSM_WIDTH)
    mix = _dot(att, wo_ref[:DA_WIDTH, :]) + _dot(ssm, wo_ref[DA_WIDTH:, :])
    y = alpha * x_ref[...] + (1.0 + g1_ref[...]) * mix.reshape(bb, ts, D_MODEL)
    x1 = _ln(y) * l1g_ref[...] + l1b_ref[...]
    x1_ref[...] = x1
    h2 = (_ln(x1) * (1.0 + sc2_ref[...]) + sh2_ref[...]).reshape(tm, D_MODEL)
    h2_ref[...] = h2.astype(BF16)
    gate_ref[...] = _router(h2, rwt_ref, rb_ref).T


def _mix(x, att, ssm, mod4, wo, l1g, l1b, rwt, rb, *, bb, ts, alpha):
    B, S, _ = x.shape
    tm = bb * ts
    row3 = lambda w: pl.BlockSpec((bb, ts, w), lambda i, b: (b, i, 0))
    modspec = lambda which: pl.BlockSpec((bb, None, 1, D_MODEL), lambda i, b: (b, which, 0, 0))
    const = lambda r, w: pl.BlockSpec((r, w), lambda i, b: (0, 0))
    flat = lambda w: pl.BlockSpec((tm, w), lambda i, b: (b * (S // ts) + i, 0))
    return pl.pallas_call(
        functools.partial(_mix_kernel, bb=bb, ts=ts, alpha=alpha),
        grid=(S // ts, B // bb),
        in_specs=[row3(D_MODEL), row3(DA_WIDTH), row3(SSM_WIDTH), modspec(2), modspec(4), modspec(3),
                  const(D_MODEL, D_MODEL), const(1, D_MODEL), const(1, D_MODEL),
                  const(LANES, D_MODEL), const(LANES, 1)],
        out_specs=[row3(D_MODEL), flat(D_MODEL), flat(LANES)],
        out_shape=[jax.ShapeDtypeStruct((B, S, D_MODEL), F32),
                   jax.ShapeDtypeStruct((B * S, D_MODEL), BF16),
                   jax.ShapeDtypeStruct((B * S, LANES), F32)],
        compiler_params=_cparams("arbitrary", "arbitrary"),
        name="mix",
    )(x, att, ssm, mod4, mod4, mod4, wo, l1g, l1b, rwt, rb)


EXPERTS_PER_STEP = 2


def _moe_kernel(h_ref, gate_ref, x1_ref, g2_ref, wg_ref, wu_ref, wd_ref, sg_ref, su_ref, sd_ref,
                l2g_ref, l2b_ref, o_ref, acc_ref, *, bb, ts, alpha):
    tm = bb * ts
    e = pl.program_id(2)
    h = h_ref[...]

    @pl.when(e == 0)
    def _():
        acc_ref[...] = jnp.zeros_like(acc_ref)

    gate = gate_ref[...]
    lane = lax.broadcasted_iota(jnp.int32, (tm, LANES), 1)
    hids = []
    for s in range(EXPERTS_PER_STEP):
        col = e * EXPERTS_PER_STEP + s
        gcol = jnp.sum(jnp.where(lane == col, gate, 0.0), axis=1, keepdims=True)
        hid = _silu(_dot(h, wg_ref[s])) * _dot(h, wu_ref[s]) * gcol
        hids.append(hid.astype(BF16))
    hid = jnp.concatenate(hids, axis=-1)
    acc_ref[...] += _dot(hid, wd_ref[...].reshape(EXPERTS_PER_STEP * EXPERT_DIM, D_MODEL))

    @pl.when(e == pl.num_programs(2) - 1)
    def _():
        shared = _dot((_silu(_dot(h, sg_ref[...])) * _dot(h, su_ref[...])).astype(BF16), sd_ref[...])
        ffn = (acc_ref[...] + shared).reshape(bb, ts, D_MODEL)
        y = alpha * x1_ref[...] + (1.0 + g2_ref[...]) * ffn
        o_ref[...] = _ln(y) * l2g_ref[...] + l2b_ref[...]


def _expert_of_column(c):
    return (c % N_EXPERT_GROUPS) * EXPERTS_PER_GROUP + c // N_EXPERT_GROUPS


def _moe(h2, gate, x1, mod4, wg, wu, wd, sg, su, sd, l2g, l2b, *, bb, ts, alpha):
    B, S, _ = x1.shape
    tm = bb * ts
    row3 = lambda w: pl.BlockSpec((bb, ts, w), lambda i, b, e: (b, i, 0))
    flat = lambda w: pl.BlockSpec((tm, w), lambda i, b, e: (b * (S // ts) + i, 0))
    const = lambda r, w: pl.BlockSpec((r, w), lambda i, b, e: (0, 0))
    ew_in = pl.BlockSpec((EXPERTS_PER_STEP, None, D_MODEL, EXPERT_DIM),
                         lambda i, b, e: (e % (N_EXPERT_GROUPS // EXPERTS_PER_STEP), e // (N_EXPERT_GROUPS // EXPERTS_PER_STEP), 0, 0))
    ew_out = pl.BlockSpec((EXPERTS_PER_STEP, None, EXPERT_DIM, D_MODEL),
                          lambda i, b, e: (e % (N_EXPERT_GROUPS // EXPERTS_PER_STEP), e // (N_EXPERT_GROUPS // EXPERTS_PER_STEP), 0, 0))
    return pl.pallas_call(
        functools.partial(_moe_kernel, bb=bb, ts=ts, alpha=alpha),
        grid=(S // ts, B // bb, N_EXPERTS // EXPERTS_PER_STEP),
        in_specs=[flat(D_MODEL), flat(LANES), row3(D_MODEL),
                  pl.BlockSpec((bb, None, 1, D_MODEL), lambda i, b, e: (b, 5, 0, 0)),
                  ew_in, ew_in, ew_out,
                  const(D_MODEL, EXPERT_DIM), const(D_MODEL, EXPERT_DIM), const(EXPERT_DIM, D_MODEL),
                  const(1, D_MODEL), const(1, D_MODEL)],
        out_specs=row3(D_MODEL),
        out_shape=jax.ShapeDtypeStruct((B, S, D_MODEL), F32),
        scratch_shapes=[pltpu.VMEM((tm, D_MODEL), F32)],
        compiler_params=_cparams("arbitrary", "arbitrary", "arbitrary"),
        name="moe",
    )(h2, gate, x1, mod4, wg, wu, wd, sg, su, sd, l2g, l2b)


def _rope_tables(pos):
    half = DA_HEAD_DIM // 2
    inv = ROPE_THETA ** (-jnp.arange(half, dtype=F32) / half)
    ang = pos.astype(F32)[:, None] * inv[None, :]
    cos = jnp.cos(ang)
    sin = jnp.sin(ang)
    reps = LANES // DA_HEAD_DIM
    cos_t = jnp.tile(jnp.concatenate([cos, cos], -1), (1, reps))
    sin_t = jnp.tile(jnp.concatenate([-sin, sin], -1), (1, reps))
    return cos_t, sin_t


def _layer(x, mod4, pos, prm, layer_idx, depth, k_past, v_past, conv_prev, ssm_prev, sample):
    B, S, _ = x.shape
    alpha = (2 * depth) ** 0.25
    lam_init = 0.8 - 0.6 * math.exp(-0.3 * layer_idx)
    cos_t, sin_t = _rope_tables(pos)
    if sample:
        bb, ts = B, S
        cos_t = jnp.tile(cos_t, (B, 1))
        sin_t = jnp.tile(sin_t, (B, 1))
    else:
        bb, ts = 1, min(512, S)
    qb, k, kb, v, vb, z, xbc, dt_raw = _proj(x, mod4, cos_t, sin_t, prm['w_main'], prm['w_dt'], bb=bb, ts=ts)
    if sample:
        att = _attn_sample(qb, kb, vb, k_past, v_past, prm['lam_vecs'], prm['subln_w'], lam_init)
        conv_prev8 = jnp.pad(conv_prev, ((0, 0), (8 - (CONV_W - 1), 0), (0, 0)))
        h0 = ssm_prev.reshape(B, SSM_WIDTH, SSM_STATE)
        t_in, T = S, max(S, LANES)
        new_conv = jnp.concatenate([conv_prev, xbc], axis=1)[:, -(CONV_W - 1):]
    else:
        att = _attn_prompt(qb, kb, vb, prm['lam_vecs'], prm['subln_w'], lam_init, tq=min(256, S))
        conv_prev8 = jnp.zeros((B, 8, CONV_DIM), F32)
        h0 = jnp.zeros((B, SSM_WIDTH, SSM_STATE), F32)
        t_in = T = min(256, S)
        new_conv = xbc[:, -(CONV_W - 1):]
    ssm, h_last = _ssd(xbc, z, dt_raw, conv_prev8, h0, prm['conv_w'], prm['conv_b'], prm['dt_bias'],
                       prm['a_log'], prm['d_skip'], prm['ssm_norm_w'], t_in=t_in, T=T)
    x1, h2, gate = _mix(x, att, ssm, mod4, prm['w_out'], prm['ln1_g'], prm['ln1_b'], prm['router_wt'],
                        prm['router_b'], bb=bb, ts=ts, alpha=alpha)
    if not sample:
        ts = min(1024, S)
    y = _moe(h2, gate, x1, mod4, prm['wg'], prm['wu'], prm['wd'], prm['sg'], prm['su'], prm['sd'],
             prm['ln2_g'], prm['ln2_b'], bb=bb, ts=ts, alpha=alpha)
    return (y, k.reshape(B, S, DA_HEADS, 2 * DA_HEAD_DIM), v.reshape(B, S, DA_HEADS, 2 * DA_HEAD_DIM),
            new_conv, h_last.reshape(B, SSM_HEADS, SSM_HEAD_DIM, SSM_STATE))


def _prep_params(w_in, lq1, lk1, lq2, lk2, subln_w, conv_w, conv_b, dt_bias, a_log, d_skip, ssm_norm_w, w_out,
                 ln1_g, ln1_b, router_w, router_bias, ewg, ewu, ewd, swg, swu, swd, ln2_g, ln2_b):
    pad8 = lambda a: jnp.pad(a.reshape(1, -1), ((0, 0), (0, LANES - a.shape[-1])))
    row = lambda a: a.reshape(1, -1)
    perm = (jnp.arange(N_EXPERTS) % N_EXPERT_GROUPS) * EXPERTS_PER_GROUP + jnp.arange(N_EXPERTS) // N_EXPERT_GROUPS
    rwt = jnp.pad(router_w.T[perm], ((0, LANES - N_EXPERTS), (0, 0)))
    rb = jnp.pad(router_bias[perm], (0, LANES - N_EXPERTS)).reshape(LANES, 1)
    grouped = lambda w: w.astype(BF16).reshape((N_EXPERT_GROUPS, EXPERTS_PER_GROUP) + w.shape[1:])
    return {
        'w_main': w_in[:, :DT_OFF].astype(BF16),
        'w_dt': jnp.pad(w_in[:, DT_OFF:], ((0, 0), (0, LANES - SSM_HEADS))),
        'lam_vecs': (row(lq1), row(lk1), row(lq2), row(lk2)),
        'subln_w': row(subln_w),
        'conv_w': conv_w, 'conv_b': row(conv_b),
        'dt_bias': pad8(dt_bias), 'a_log': pad8(a_log),
        'd_skip': row(jnp.repeat(d_skip, SSM_HEAD_DIM)), 'ssm_norm_w': row(ssm_norm_w),
        'w_out': w_out.astype(BF16), 'ln1_g': row(ln1_g), 'ln1_b': row(ln1_b),
        'router_wt': rwt, 'router_b': rb,
        'wg': grouped(ewg), 'wu': grouped(ewu), 'wd': grouped(ewd),
        'sg': swg.astype(BF16), 'su': swu.astype(BF16), 'sd': swd.astype(BF16),
        'ln2_g': row(ln2_g), 'ln2_b': row(ln2_b),
    }


def kernel(x_prompt, x_sample, c_prompt, c_sample, cache_k, cache_v, state_conv, state_ssm, w_ada, b_ada, w_in, lambda_q1, lambda_k1, lambda_q2, lambda_k2, subln_w, conv_w, conv_b, dt_bias, a_log, d_skip, ssm_norm_w, w_out, ln1_g, ln1_b, router_w, router_bias, exp_w_gate, exp_w_up, exp_w_down, sh_w_gate, sh_w_up, sh_w_down, ln2_g, ln2_b):
    depth = w_ada.shape[0]
    bp, sp, _ = x_prompt.shape
    bs, ss, _ = x_sample.shape
    past = cache_k.shape[2]
    pos_prompt = jnp.arange(sp)
    pos_sample = past + jnp.arange(ss)
    c_rows = bp + bs
    c_pad = (-c_rows) % 8
    c_all = jnp.pad(jnp.concatenate([c_prompt, c_sample], 0), ((0, c_pad), (0, 0)))
    yp, ys = x_prompt, x_sample
    outs = [[] for _ in range(8)]
    for l in range(depth):
        prm = _prep_params(w_in[l], lambda_q1[l], lambda_k1[l], lambda_q2[l], lambda_k2[l], subln_w[l],
                           conv_w[l], conv_b[l], dt_bias[l], a_log[l], d_skip[l], ssm_norm_w[l], w_out[l],
                           ln1_g[l], ln1_b[l], router_w[l], router_bias[l], exp_w_gate[l], exp_w_up[l],
                           exp_w_down[l], sh_w_gate[l], sh_w_up[l], sh_w_down[l], ln2_g[l], ln2_b[l])
        mod = _ada(c_all, w_ada[l], b_ada[l]).reshape(c_rows + c_pad, 6, 1, D_MODEL)
        yp, kp, vp, cp, hp = _layer(yp, mod[:bp], pos_prompt, prm, l, depth, None, None, None, None, False)
        ys, kn, vn, cn, hn = _layer(ys, mod[bp:c_rows], pos_sample, prm, l, depth,
                                    cache_k[l].reshape(bs, past, DA_WIDTH), cache_v[l].reshape(bs, past, DA_WIDTH),
                                    state_conv[l], state_ssm[l], True)
        for lst, val in zip(outs, (kp, vp, cp, hp, kn, vn, cn, hn)):
            lst.append(val)
    return (yp, ys) + tuple(jnp.stack(o) for o in outs)
```

```python
import functools
import math

import jax
import jax.numpy as jnp
from jax import lax
from jax.experimental import pallas as pl
from jax.experimental.pallas import tpu as pltpu

F32 = jnp.float32
BF16 = jnp.bfloat16

D_MODEL = 1024
CHUNK = 64
CHUNK_SHIFT = 6
HEAD_SHIFT = 6
EPS = 1e-5
LOG2E = math.log2(math.e)
DA_HEADS = 4
DA_HEAD_DIM = 64
DA_WIDTH = DA_HEADS * 2 * DA_HEAD_DIM
ROPE_THETA = 10000.0
SSM_HEADS = 8
SSM_HEAD_DIM = 64
SSM_WIDTH = SSM_HEADS * SSM_HEAD_DIM
SSM_GROUPS = 2
SSM_STATE = 128
CONV_W = 4
CONV_DIM = SSM_WIDTH + 2 * SSM_GROUPS * SSM_STATE
Q_OFF = 0
K_OFF = 512
V_OFF = 1024
Z_OFF = 1536
XBC_OFF = 2048
DT_OFF = 3072
N_EXPERTS = 64
N_EXPERT_GROUPS = 8
EXPERTS_PER_GROUP = 8
TOPK_GROUPS = 4
TOP_K = 8
EXPERT_DIM = 256
ROUTED_SCALE = 2.5

LANES = 128
VMEM_LIMIT = 56 * 1024 * 1024


def _cparams(*sem):
    return pltpu.CompilerParams(dimension_semantics=sem, vmem_limit_bytes=VMEM_LIMIT)


def _dot(a, b):
    return jnp.dot(a, b, preferred_element_type=F32)


def _dot_nt(a, b):
    return lax.dot_general(a, b, (((1,), (1,)), ((), ())), preferred_element_type=F32)


def _split2(a):
    hi = a.astype(BF16)
    lo = (a - hi.astype(F32)).astype(BF16)
    return hi, lo


def _split3(a):
    hi = a.astype(BF16)
    r = a - hi.astype(F32)
    mid = r.astype(BF16)
    lo = (r - mid.astype(F32)).astype(BF16)
    return hi, mid, lo


def _dot3(a, b, dot=_dot):
    ah, al = _split2(a)
    bh, bl = _split2(b)
    return dot(ah, bh) + (dot(ah, bl) + dot(al, bh))


def _dot_exact_lhs(e, a, dot=_dot):
    ah, am, al = _split3(a)
    return dot(e, ah) + (dot(e, am) + dot(e, al))


def _silu(x):
    return x * jax.nn.sigmoid(x)


def _softplus(x):
    return jnp.maximum(x, 0.0) + jnp.log1p(jnp.exp(-jnp.abs(x)))


def _ln(x):
    mu = jnp.mean(x, -1, keepdims=True)
    xc = x - mu
    var = jnp.mean(xc * xc, -1, keepdims=True)
    return xc * lax.rsqrt(var + EPS)


def _ada_kernel(c_ref, w_ref, b_ref, o_ref):
    o_ref[...] = _dot3(_silu(c_ref[...]), w_ref[...]) + b_ref[...]


def _ada(c, w_ada, b_ada):
    rows = c.shape[0]
    n = w_ada.shape[1]
    tn = 1024
    return pl.pallas_call(
        _ada_kernel,
        grid=(n // tn,),
        in_specs=[pl.BlockSpec((rows, D_MODEL), lambda i: (0, 0)),
                  pl.BlockSpec((D_MODEL, tn), lambda i: (0, i)),
                  pl.BlockSpec((1, tn), lambda i: (0, i))],
        out_specs=pl.BlockSpec((rows, tn), lambda i: (0, i)),
        out_shape=jax.ShapeDtypeStruct((rows, n), F32),
        compiler_params=_cparams("arbitrary"),
        name="ada",
    )(c, w_ada, b_ada.reshape(1, n))


def _proj_kernel(x_ref, sc_ref, sh_ref, cos_ref, sin_ref, w_ref, wdt_ref, wvt_ref,
                 q_ref, k_ref, kb_ref, v_ref, vb_ref, z_ref, xbc_ref, dt_ref, *, bb, ts, v_transposed):
    tm = bb * ts
    h = _ln(x_ref[...]) * (1.0 + sc_ref[...]) + sh_ref[...]
    h = h.reshape(tm, D_MODEL)
    hb = h.astype(BF16)

    cos = jnp.tile(cos_ref[...], (1, DA_WIDTH // LANES))
    sin = jnp.tile(sin_ref[...], (1, DA_WIDTH // LANES))
    lane = lax.broadcasted_iota(jnp.int32, (tm, DA_WIDTH), 1)
    first = (lane & (DA_HEAD_DIM - 1)) < (DA_HEAD_DIM // 2)

    def rope(t):
        rot = jnp.where(first, pltpu.roll(t, DA_WIDTH - DA_HEAD_DIM // 2, 1),
                        pltpu.roll(t, DA_HEAD_DIM // 2, 1))
        return t * cos + rot * sin

    q = rope(_dot(hb, w_ref[:, Q_OFF:K_OFF])) * (DA_HEAD_DIM ** -0.5 * LOG2E)
    q_ref[...] = q.reshape(bb, ts, DA_WIDTH).astype(BF16)
    k = rope(_dot(hb, w_ref[:, K_OFF:V_OFF])).reshape(bb, ts, DA_WIDTH)
    k_ref[...] = k
    kb_ref[...] = k.astype(BF16)
    v = _dot(hb, w_ref[:, V_OFF:Z_OFF]).reshape(bb, ts, DA_WIDTH)
    v_ref[...] = v
    if v_transposed:
        vb_ref[0, 0] = _dot_nt(wvt_ref[...], hb).astype(BF16)
    else:
        vb_ref[...] = v.astype(BF16)
    z_ref[...] = _dot(hb, w_ref[:, Z_OFF:XBC_OFF]).reshape(bb, ts, SSM_WIDTH)
    xbc_ref[...] = _dot(hb, w_ref[:, XBC_OFF:DT_OFF]).reshape(bb, ts, CONV_DIM)
    dt_ref[...] = _dot3(h, wdt_ref[...]).reshape(bb, ts, LANES)


def _proj(x, mod4, cos_t, sin_t, w_main, w_dt, w_vt, *, bb, ts, v_transposed):
    B, S, _ = x.shape
    tm = bb * ts
    grid = (S // ts, B // bb)
    row3 = lambda w: pl.BlockSpec((bb, ts, w), lambda i, b: (b, i, 0))
    modspec = lambda which: pl.BlockSpec((bb, None, 1, D_MODEL), lambda i, b: (b, which, 0, 0))
    if v_transposed:
        vb_shape = jax.ShapeDtypeStruct((B, S // ts, DA_WIDTH, ts), BF16)
        vb_spec = pl.BlockSpec((1, 1, DA_WIDTH, ts), lambda i, b: (b, i, 0, 0))
    else:
        vb_shape = jax.ShapeDtypeStruct((B, S, DA_WIDTH), BF16)
        vb_spec = row3(DA_WIDTH)
    out_shapes = [jax.ShapeDtypeStruct((B, S, DA_WIDTH), BF16),
                  jax.ShapeDtypeStruct((B, S, DA_WIDTH), F32),
                  jax.ShapeDtypeStruct((B, S, DA_WIDTH), BF16),
                  jax.ShapeDtypeStruct((B, S, DA_WIDTH), F32),
                  vb_shape,
                  jax.ShapeDtypeStruct((B, S, SSM_WIDTH), F32),
                  jax.ShapeDtypeStruct((B, S, CONV_DIM), F32),
                  jax.ShapeDtypeStruct((B, S, LANES), F32)]
    out_specs = [row3(DA_WIDTH)] * 4 + [vb_spec, row3(SSM_WIDTH), row3(CONV_DIM), row3(LANES)]
    return pl.pallas_call(
        functools.partial(_proj_kernel, bb=bb, ts=ts, v_transposed=v_transposed),
        grid=grid,
        in_specs=[row3(D_MODEL), modspec(1), modspec(0),
                  pl.BlockSpec((tm, LANES), lambda i, b: (i, 0)),
                  pl.BlockSpec((tm, LANES), lambda i, b: (i, 0)),
                  pl.BlockSpec((D_MODEL, DT_OFF), lambda i, b: (0, 0)),
                  pl.BlockSpec((D_MODEL, LANES), lambda i, b: (0, 0)),
                  pl.BlockSpec((DA_WIDTH, D_MODEL), lambda i, b: (0, 0))],
        out_specs=out_specs,
        out_shape=out_shapes,
        compiler_params=_cparams("arbitrary", "arbitrary"),
        name="proj",
    )(x, mod4, mod4, cos_t, sin_t, w_main, w_dt, w_vt)


def _lambda(lq1, lk1, lq2, lk2, lam_init):
    s1 = jnp.sum(lq1[...] * lk1[...], axis=1, keepdims=True)
    s2 = jnp.sum(lq2[...] * lk2[...], axis=1, keepdims=True)
    return jnp.exp(s1) - jnp.exp(s2) + lam_init


def _subln(o, subw, lam_init):
    ms = jnp.mean(o * o, -1, keepdims=True)
    return o * lax.rsqrt(ms + EPS) * subw * (1.0 - lam_init)


def _attn_prompt_kernel(lq1, lk1, lq2, lk2, subw_ref, q_ref, k_ref, vt_ref, o_ref,
                        acc1, acc2, sa1, sa2, sb1, sb2, mxa, mxb, m1, l1, m2, l2, *, tq, tk, lam_init):
    i = pl.program_id(2)
    qt = q_ref[0].astype(F32).T
    row = lax.broadcasted_iota(jnp.int32, (LANES, tq), 0)
    q1t = jnp.where(row < DA_HEAD_DIM, qt, 0.0).astype(BF16)
    q2t = jnp.where(row >= DA_HEAD_DIM, qt, 0.0).astype(BF16)
    acc1[...] = jnp.zeros_like(acc1)
    acc2[...] = jnp.zeros_like(acc2)
    m1[...] = jnp.full_like(m1, -jnp.inf)
    m2[...] = jnp.full_like(m2, -jnp.inf)
    l1[...] = jnp.zeros_like(l1)
    l2[...] = jnp.zeros_like(l2)
    buf_a = (sa1, sa2, mxa)
    buf_b = (sb1, sb2, mxb)

    def produce(j, buf):
        k = k_ref[0, pl.ds(pl.multiple_of(j * tk, tk), tk), :]
        for half, qh in enumerate((q1t, q2t)):
            s = _dot(k, qh)
            buf[half][...] = s
            buf[2][half:half + 1, :] = jnp.max(s, axis=0, keepdims=True)

    def consume(j, buf, masked):
        vt = vt_ref[0, j]
        if masked:
            key = (j * tk + lax.broadcasted_iota(jnp.int32, (tk, tq), 0)) >> CHUNK_SHIFT
            qry = (i * tq + lax.broadcasted_iota(jnp.int32, (tk, tq), 1)) >> CHUNK_SHIFT
            keep = key <= qry
        for half, (m, l, acc) in enumerate(((m1, l1, acc1), (m2, l2, acc2))):
            st = buf[half][...]
            if masked:
                st = jnp.where(keep, st, -jnp.inf)
                tile_max = jnp.max(st, axis=0, keepdims=True)
            else:
                tile_max = buf[2][half:half + 1, :]
            mo = m[...]
            mn = jnp.maximum(mo, tile_max)
            a = jnp.exp2(mo - mn)
            p = jnp.exp2(st - mn)
            l[...] = a * l[...] + jnp.sum(p, axis=0, keepdims=True)
            m[...] = mn
            acc[...] = a * acc[...] + _dot(vt, p.astype(BF16))

    nfull = (i * tq) // tk
    produce(0, buf_a)

    def pair(jj, carry):
        produce(2 * jj + 1, buf_b)
        consume(2 * jj, buf_a, False)
        produce(2 * jj + 2, buf_a)
        consume(2 * jj + 1, buf_b, False)
        return carry

    lax.fori_loop(0, nfull >> 1, pair, 0)
    t0 = (nfull >> 1) * 2

    @pl.when((nfull & 1) == 0)
    def _():
        consume(t0, buf_a, True)

    @pl.when((nfull & 1) == 1)
    def _():
        produce(t0 + 1, buf_b)
        consume(t0, buf_a, False)
        consume(t0 + 1, buf_b, True)

    lam = _lambda(lq1, lk1, lq2, lk2, lam_init)
    ot = acc1[...] / l1[...] - lam * (acc2[...] / l2[...])
    ot = ot * lax.rsqrt(jnp.mean(ot * ot, axis=0, keepdims=True) + EPS)
    o_ref[0] = (ot.T * subw_ref[...] * (1.0 - lam_init)).astype(BF16)


def _attn_prompt(qb, kb, vt, lam_vecs, subw, lam_init, *, tq):
    B, S, _ = qb.shape
    nkt, tk = vt.shape[1], vt.shape[3]
    small = pl.BlockSpec((1, DA_HEAD_DIM), lambda b, h, i: (0, 0))
    return pl.pallas_call(
        functools.partial(_attn_prompt_kernel, tq=tq, tk=tk, lam_init=lam_init),
        grid=(B, DA_HEADS, S // tq),
        in_specs=[small, small, small, small,
                  pl.BlockSpec((1, LANES), lambda b, h, i: (0, 0)),
                  pl.BlockSpec((1, tq, LANES), lambda b, h, i: (b, i, h)),
                  pl.BlockSpec((1, S, LANES), lambda b, h, i: (b, 0, h)),
                  pl.BlockSpec((1, nkt, LANES, tk), lambda b, h, i: (b, 0, h, 0))],
        out_specs=pl.BlockSpec((1, tq, LANES), lambda b, h, i: (b, i, h)),
        out_shape=jax.ShapeDtypeStruct((B, S, DA_WIDTH), BF16),
        scratch_shapes=([pltpu.VMEM((LANES, tq), F32)] * 2 + [pltpu.VMEM((tk, tq), F32)] * 4
                        + [pltpu.VMEM((2, tq), F32)] * 2 + [pltpu.VMEM((1, tq), F32)] * 4),
        compiler_params=_cparams("arbitrary", "arbitrary", "arbitrary"),
        name="attn_prompt",
    )(*lam_vecs, subw, qb, kb, vt)


def _attn_sample_kernel(lq1, lk1, lq2, lk2, subw_ref, q_ref, kn_ref, vn_ref, kp_ref, vp_ref, o_ref,
                        *, lam_init):
    s = q_ref.shape[1]
    lam = _lambda(lq1, lk1, lq2, lk2, lam_init)
    lane = lax.broadcasted_iota(jnp.int32, (s, LANES), 1)
    outs = []
    for h in range(DA_HEADS):
        cols = slice(h * LANES, (h + 1) * LANES)
        q = q_ref[0, :, cols]
        zero = jnp.zeros_like(q)
        kn = kn_ref[0, :, cols]
        vn = vn_ref[0, :, cols]
        past = kp_ref.shape[1] // DA_HEADS
        kp = kp_ref[0, pl.ds(h, past, stride=DA_HEADS), :].astype(BF16)
        vp = vp_ref[0, pl.ds(h, past, stride=DA_HEADS), :].astype(BF16)
        probs = []
        for qm in (jnp.where(lane < DA_HEAD_DIM, q, zero), jnp.where(lane >= DA_HEAD_DIM, q, zero)):
            sp = _dot_nt(qm, kp)
            sn = _dot_nt(qm, kn)
            m = jnp.maximum(jnp.max(sp, -1, keepdims=True), jnp.max(sn, -1, keepdims=True))
            pp = jnp.exp2(sp - m)
            pn = jnp.exp2(sn - m)
            l = jnp.sum(pp, -1, keepdims=True) + jnp.sum(pn, -1, keepdims=True)
            probs.append((pp / l, pn / l))
        ap = probs[0][0] - lam * probs[1][0]
        an = probs[0][1] - lam * probs[1][1]
        o = _dot(ap.astype(BF16), vp) + _dot(an.astype(BF16), vn)
        outs.append(_subln(o, subw_ref[...], lam_init))
    o_ref[0] = jnp.concatenate(outs, axis=-1).astype(BF16)


def _attn_sample(qb, kb, vb, k_past, v_past, lam_vecs, subw, lam_init):
    B, s, _ = qb.shape
    past_rows = k_past.shape[1]
    small = pl.BlockSpec((1, DA_HEAD_DIM), lambda b: (0, 0))
    new = pl.BlockSpec((1, s, DA_WIDTH), lambda b: (b, 0, 0))
    old = pl.BlockSpec((1, past_rows, LANES), lambda b: (b, 0, 0))
    return pl.pallas_call(
        functools.partial(_attn_sample_kernel, lam_init=lam_init),
        grid=(B,),
        in_specs=[small, small, small, small, pl.BlockSpec((1, LANES), lambda b: (0, 0)),
                  new, new, new, old, old],
        out_specs=new,
        out_shape=jax.ShapeDtypeStruct((B, s, DA_WIDTH), BF16),
        compiler_params=_cparams("arbitrary"),
        name="attn_sample",
    )(*lam_vecs, subw, qb, kb, vb, k_past, v_past)


def _ssd_kernel(xbc_ref, z_ref, dt_ref, cp_ref, h0_ref, cw_ref, cb_ref, dtb_ref, alog_ref, dsk_ref, nw_ref,
                y_ref, hout_ref, tail_ref, st_ref, *, t_in, T):
    j = pl.program_id(1)
    PAIR = 2 * SSM_HEAD_DIM

    @pl.when(j == 0)
    def _():
        tail_ref[...] = cp_ref[0]
        st_ref[...] = h0_ref[0]

    def rows(ref, width):
        val = ref[0]
        if T == t_in:
            return val
        return jnp.concatenate([val, jnp.zeros((T - t_in, width), F32)], axis=0)

    xb = rows(xbc_ref, CONV_DIM)
    xp = jnp.concatenate([tail_ref[...], xb], axis=0)
    conv = cb_ref[...] + cw_ref[CONV_W - 1:CONV_W, :] * xb
    for sft in range(1, CONV_W):
        conv = conv + cw_ref[CONV_W - 1 - sft:CONV_W - sft, :] * pltpu.roll(xp, sft, 0)[8:, :]
    if T == t_in:
        tail_ref[...] = xb[T - 8:, :]
    xa = _silu(conv)
    xs = xa[:, :SSM_WIDTH]

    dt = _softplus(rows(dt_ref, LANES) + dtb_ref[...])
    if T != t_in:
        valid = lax.broadcasted_iota(jnp.int32, (T, LANES), 0) < t_in
        dt = jnp.where(valid, dt, 0.0)
    da = dt * (-jnp.exp(alog_ref[...]))

    rr = lax.broadcasted_iota(jnp.int32, (T, T), 0)
    cc = lax.broadcasted_iota(jnp.int32, (T, T), 1)
    causal = cc <= rr
    tril = jnp.where(causal, 1.0, 0.0).astype(BF16)
    acs = _dot_exact_lhs(tril, da)
    e16 = jnp.where(lax.broadcasted_iota(jnp.int32, (16, LANES), 0)
                    == lax.broadcasted_iota(jnp.int32, (16, LANES), 1), 1.0, 0.0).astype(BF16)
    acs_t = _dot_exact_lhs(e16, acs, dot=_dot_nt)

    last = acs[T - 1:T, :]
    expand = jnp.where((lax.broadcasted_iota(jnp.int32, (LANES, SSM_WIDTH), 1) >> HEAD_SHIFT)
                       == lax.broadcasted_iota(jnp.int32, (LANES, SSM_WIDTH), 0), 1.0, 0.0).astype(BF16)

    def per_head_lanes(v):
        hi, lo = _split2(v)
        return _dot(hi, expand) + _dot(lo, expand)

    xdt = xs * per_head_lanes(dt)
    xdd = xdt * per_head_lanes(jnp.exp(last - acs))
    eacs = per_head_lanes(jnp.exp(acs))

    head_of_row = lax.broadcasted_iota(jnp.int32, (SSM_WIDTH, LANES), 0) >> HEAD_SHIFT
    sel = head_of_row == lax.broadcasted_iota(jnp.int32, (SSM_WIDTH, LANES), 1)
    rowdec = jnp.sum(jnp.where(sel, jnp.exp(last), 0.0), axis=1, keepdims=True)

    lane = lax.broadcasted_iota(jnp.int32, (T, PAIR), 1)
    ys = []
    for p in range(SSM_HEADS // 2):
        g = (2 * p) // (SSM_HEADS // SSM_GROUPS)
        bm = xa[:, SSM_WIDTH + g * SSM_STATE:SSM_WIDTH + (g + 1) * SSM_STATE].astype(BF16)
        cm = xa[:, SSM_WIDTH + (SSM_GROUPS + g) * SSM_STATE:
                SSM_WIDTH + (SSM_GROUPS + g + 1) * SSM_STATE].astype(BF16)
        cb = _dot_nt(cm, bm)
        cols = slice(p * PAIR, (p + 1) * PAIR)
        xdt_p = xdt[:, cols].astype(BF16)
        yd = []
        for hh in (2 * p, 2 * p + 1):
            seg = acs[:, hh:hh + 1] - acs_t[hh:hh + 1, :]
            lmat = jnp.exp(jnp.where(causal, seg, -jnp.inf))
            yd.append(_dot((cb * lmat).astype(BF16), xdt_p))
        y_diag = jnp.where(lane < SSM_HEAD_DIM, yd[0], yd[1])
        st = st_ref[cols, :]
        y_off = _dot_nt(cm, st.astype(BF16)) * eacs[:, cols]
        ys.append(y_diag + y_off)
        st_ref[cols, :] = st * rowdec[cols, :] + _dot(xdd[:, cols].T.astype(BF16), bm)
    y = jnp.concatenate(ys, axis=-1) + xs * dsk_ref[...]
    y = y * _silu(rows(z_ref, SSM_WIDTH))
    half = SSM_WIDTH // SSM_GROUPS
    outs = []
    for g in range(SSM_GROUPS):
        yg = y[:, g * half:(g + 1) * half]
        outs.append(yg * lax.rsqrt(jnp.mean(yg * yg, -1, keepdims=True) + EPS))
    y = jnp.concatenate(outs, axis=-1) * nw_ref[...]
    y_ref[0] = y[:t_in, :].astype(BF16)

    @pl.when(j == pl.num_programs(1) - 1)
    def _():
        hout_ref[0] = st_ref[...]


def _ssd(xbc, z, dt_raw, conv_prev8, h0, conv_w, conv_b, dtb, alog, dsk, nw, *, t_in, T):
    B, S, _ = xbc.shape
    row = lambda w: pl.BlockSpec((1, t_in, w), lambda b, j: (b, j, 0))
    per_b = lambda r, w: pl.BlockSpec((1, r, w), lambda b, j: (b, 0, 0))
    const = lambda r, w: pl.BlockSpec((r, w), lambda b, j: (0, 0))
    return pl.pallas_call(
        functools.partial(_ssd_kernel, t_in=t_in, T=T),
        grid=(B, S // t_in),
        in_specs=[row(CONV_DIM), row(SSM_WIDTH), row(LANES),
                  per_b(8, CONV_DIM), per_b(SSM_WIDTH, SSM_STATE),
                  const(CONV_W, CONV_DIM), const(1, CONV_DIM), const(1, LANES), const(1, LANES),
                  const(1, SSM_WIDTH), const(1, SSM_WIDTH)],
        out_specs=[row(SSM_WIDTH), per_b(SSM_WIDTH, SSM_STATE)],
        out_shape=[jax.ShapeDtypeStruct((B, S, SSM_WIDTH), BF16),
                   jax.ShapeDtypeStruct((B, SSM_WIDTH, SSM_STATE), F32)],
        scratch_shapes=[pltpu.VMEM((8, CONV_DIM), F32), pltpu.VMEM((SSM_WIDTH, SSM_STATE), F32)],
        compiler_params=_cparams("arbitrary", "arbitrary"),
        name="ssd",
    )(xbc, z, dt_raw, conv_prev8, h0, conv_w, conv_b, dtb, alog, dsk, nw)


def _router(h, rwt_ref, rb_ref):
    tm = h.shape[0]
    G = N_EXPERT_GROUPS
    logits = _dot3(rwt_ref[...], h, dot=_dot_nt)
    score = jax.nn.sigmoid(logits[:N_EXPERTS, :])
    biased = score + rb_ref[:N_EXPERTS, :]
    sc = [score[G * j:G * (j + 1), :] for j in range(EXPERTS_PER_GROUP)]
    v = [biased[G * j:G * (j + 1), :] for j in range(EXPERTS_PER_GROUP)]

    grp = jnp.zeros((G, tm), F32)
    for j in range(EXPERTS_PER_GROUP):
        cnt = jnp.zeros((G, tm), jnp.int32)
        for j2 in range(EXPERTS_PER_GROUP):
            if j2 == j:
                continue
            beats = (v[j2] >= v[j]) if j2 < j else (v[j2] > v[j])
            cnt = cnt + beats.astype(jnp.int32)
        grp = grp + jnp.where(cnt < 2, v[j], 0.0)

    gid = lax.broadcasted_iota(jnp.int32, (G, tm), 0)
    cnt = jnp.zeros((G, tm), jnp.int32)
    for g2 in range(G):
        row = grp[g2:g2 + 1, :]
        beats = (row > grp) | ((row == grp) & (gid > g2))
        cnt = cnt + beats.astype(jnp.int32)
    gmask = cnt < TOPK_GROUPS

    mv = [jnp.where(gmask, v[j], -jnp.inf) for j in range(EXPERTS_PER_GROUP)]
    cnts = [jnp.zeros((G, tm), jnp.int32) for _ in range(EXPERTS_PER_GROUP)]
    for j2 in range(EXPERTS_PER_GROUP):
        for g2 in range(G):
            row = mv[j2][g2:g2 + 1, :]
            for j in range(EXPERTS_PER_GROUP):
                earlier = (gid > g2) | (gid == g2) if j2 < j else (gid > g2)
                beats = (row > mv[j]) | ((row == mv[j]) & earlier)
                cnts[j] = cnts[j] + beats.astype(jnp.int32)
    w = [jnp.where(cnts[j] < TOP_K, sc[j], 0.0) for j in range(EXPERTS_PER_GROUP)]
    tot = w[0]
    for j in range(1, EXPERTS_PER_GROUP):
        tot = tot + w[j]
    tot = jnp.sum(tot, axis=0, keepdims=True)
    gates = [w[j] / tot * ROUTED_SCALE for j in range(EXPERTS_PER_GROUP)]
    return jnp.concatenate(gates + [jnp.zeros((LANES - N_EXPERTS, tm), F32)], axis=0)


def _mix_kernel(x_ref, att_ref, ssm_ref, g1_ref, sc2_ref, sh2_ref, wo_ref, l1g_ref, l1b_ref, rwt_ref, rb_ref,
                x1_ref, h2_ref, gate_ref, *, bb, ts, alpha):
    tm = bb * ts
    att = att_ref[...].reshape(tm, DA_WIDTH)
    ssm = ssm_ref[...].reshape(tm, SSM_WIDTH)
    mix = _dot(att, wo_ref[:DA_WIDTH, :]) + _dot(ssm, wo_ref[DA_WIDTH:, :])
    y = alpha * x_ref[...] + (1.0 + g1_ref[...]) * mix.reshape(bb, ts, D_MODEL)
    x1 = _ln(y) * l1g_ref[...] + l1b_ref[...]
    x1_ref[...] = x1
    h2 = (_ln(x1) * (1.0 + sc2_ref[...]) + sh2_ref[...]).reshape(tm, D_MODEL)
    h2_ref[...] = h2.astype(BF16)
    gate_ref[...] = _router(h2, rwt_ref, rb_ref).T


def _mix(x, att, ssm, mod4, wo, l1g, l1b, rwt, rb, *, bb, ts, alpha):
    B, S, _ = x.shape
    tm = bb * ts
    row3 = lambda w: pl.BlockSpec((bb, ts, w), lambda i, b: (b, i, 0))
    modspec = lambda which: pl.BlockSpec((bb, None, 1, D_MODEL), lambda i, b: (b, which, 0, 0))
    const = lambda r, w: pl.BlockSpec((r, w), lambda i, b: (0, 0))
    flat = lambda w: pl.BlockSpec((tm, w), lambda i, b: (b * (S // ts) + i, 0))
    return pl.pallas_call(
        functools.partial(_mix_kernel, bb=bb, ts=ts, alpha=alpha),
        grid=(S // ts, B // bb),
        in_specs=[row3(D_MODEL), row3(DA_WIDTH), row3(SSM_WIDTH), modspec(2), modspec(4), modspec(3),
                  const(D_MODEL, D_MODEL), const(1, D_MODEL), const(1, D_MODEL),
                  const(LANES, D_MODEL), const(LANES, 1)],
        out_specs=[row3(D_MODEL), flat(D_MODEL), flat(LANES)],
        out_shape=[jax.ShapeDtypeStruct((B, S, D_MODEL), F32),
                   jax.ShapeDtypeStruct((B * S, D_MODEL), BF16),
                   jax.ShapeDtypeStruct((B * S, LANES), F32)],
        compiler_params=_cparams("arbitrary", "arbitrary"),
        name="mix",
    )(x, att, ssm, mod4, mod4, mod4, wo, l1g, l1b, rwt, rb)


EXPERTS_PER_STEP = 2


def _moe_kernel(h_ref, gate_ref, x1_ref, g2_ref, wg_ref, wu_ref, wd_ref, sg_ref, su_ref, sd_ref,
                l2g_ref, l2b_ref, o_ref, acc_ref, *, bb, ts, alpha):
    tm = bb * ts
    e = pl.program_id(2)
    h = h_ref[...]

    @pl.when(e == 0)
    def _():
        acc_ref[...] = jnp.zeros_like(acc_ref)

    gate = gate_ref[...]
    lane = lax.broadcasted_iota(jnp.int32, (tm, LANES), 1)
    hids = []
    for s in range(EXPERTS_PER_STEP):
        col = e * EXPERTS_PER_STEP + s
        gcol = jnp.sum(jnp.where(lane == col, gate, 0.0), axis=1, keepdims=True)
        hid = _silu(_dot(h, wg_ref[s])) * _dot(h, wu_ref[s]) * gcol
        hids.append(hid.astype(BF16))
    hid = jnp.concatenate(hids, axis=-1)
    acc_ref[...] += _dot(hid, wd_ref[...].reshape(EXPERTS_PER_STEP * EXPERT_DIM, D_MODEL))

    @pl.when(e == pl.num_programs(2) - 1)
    def _():
        shared = _dot((_silu(_dot(h, sg_ref[...])) * _dot(h, su_ref[...])).astype(BF16), sd_ref[...])
        ffn = (acc_ref[...] + shared).reshape(bb, ts, D_MODEL)
        y = alpha * x1_ref[...] + (1.0 + g2_ref[...]) * ffn
        o_ref[...] = _ln(y) * l2g_ref[...] + l2b_ref[...]


def _expert_of_column(c):
    return (c % N_EXPERT_GROUPS) * EXPERTS_PER_GROUP + c // N_EXPERT_GROUPS


def _moe(h2, gate, x1, mod4, wg, wu, wd, sg, su, sd, l2g, l2b, *, bb, ts, alpha):
    B, S, _ = x1.shape
    tm = bb * ts
    row3 = lambda w: pl.BlockSpec((bb, ts, w), lambda i, b, e: (b, i, 0))
    flat = lambda w: pl.BlockSpec((tm, w), lambda i, b, e: (b * (S // ts) + i, 0))
    const = lambda r, w: pl.BlockSpec((r, w), lambda i, b, e: (0, 0))
    ew_in = pl.BlockSpec((EXPERTS_PER_STEP, None, D_MODEL, EXPERT_DIM),
                         lambda i, b, e: (e % (N_EXPERT_GROUPS // EXPERTS_PER_STEP), e // (N_EXPERT_GROUPS // EXPERTS_PER_STEP), 0, 0))
    ew_out = pl.BlockSpec((EXPERTS_PER_STEP, None, EXPERT_DIM, D_MODEL),
                          lambda i, b, e: (e % (N_EXPERT_GROUPS // EXPERTS_PER_STEP), e // (N_EXPERT_GROUPS // EXPERTS_PER_STEP), 0, 0))
    return pl.pallas_call(
        functools.partial(_moe_kernel, bb=bb, ts=ts, alpha=alpha),
        grid=(S // ts, B // bb, N_EXPERTS // EXPERTS_PER_STEP),
        in_specs=[flat(D_MODEL), flat(LANES), row3(D_MODEL),
                  pl.BlockSpec((bb, None, 1, D_MODEL), lambda i, b, e: (b, 5, 0, 0)),
                  ew_in, ew_in, ew_out,
                  const(D_MODEL, EXPERT_DIM), const(D_MODEL, EXPERT_DIM), const(EXPERT_DIM, D_MODEL),
                  const(1, D_MODEL), const(1, D_MODEL)],
        out_specs=row3(D_MODEL),
        out_shape=jax.ShapeDtypeStruct((B, S, D_MODEL), F32),
        scratch_shapes=[pltpu.VMEM((tm, D_MODEL), F32)],
        compiler_params=_cparams("arbitrary", "arbitrary", "arbitrary"),
        name="moe",
    )(h2, gate, x1, mod4, wg, wu, wd, sg, su, sd, l2g, l2b)


def _rope_tables(pos):
    half = DA_HEAD_DIM // 2
    inv = ROPE_THETA ** (-jnp.arange(half, dtype=F32) / half)
    ang = pos.astype(F32)[:, None] * inv[None, :]
    cos = jnp.cos(ang)
    sin = jnp.sin(ang)
    reps = LANES // DA_HEAD_DIM
    cos_t = jnp.tile(jnp.concatenate([cos, cos], -1), (1, reps))
    sin_t = jnp.tile(jnp.concatenate([-sin, sin], -1), (1, reps))
    return cos_t, sin_t


def _layer(x, mod4, pos, prm, layer_idx, depth, k_past, v_past, conv_prev, ssm_prev, sample):
    B, S, _ = x.shape
    alpha = (2 * depth) ** 0.25
    lam_init = 0.8 - 0.6 * math.exp(-0.3 * layer_idx)
    cos_t, sin_t = _rope_tables(pos)
    if sample:
        bb, ts = B, S
        cos_t = jnp.tile(cos_t, (B, 1))
        sin_t = jnp.tile(sin_t, (B, 1))
    else:
        bb, ts = 1, min(512, S)
    qb, k, kb, v, vb, z, xbc, dt_raw = _proj(x, mod4, cos_t, sin_t, prm['w_main'], prm['w_dt'], prm['w_vt'],
                                             bb=bb, ts=ts, v_transposed=not sample)
    if sample:
        att = _attn_sample(qb, kb, vb, k_past, v_past, prm['lam_vecs'], prm['subln_w'], lam_init)
        conv_prev8 = jnp.pad(conv_prev, ((0, 0), (8 - (CONV_W - 1), 0), (0, 0)))
        h0 = ssm_prev.reshape(B, SSM_WIDTH, SSM_STATE)
        t_in, T = S, max(S, LANES)
        new_conv = jnp.concatenate([conv_prev, xbc], axis=1)[:, -(CONV_W - 1):]
    else:
        att = _attn_prompt(qb, kb, vb, prm['lam_vecs'], prm['subln_w'], lam_init, tq=min(256, S))
        conv_prev8 = jnp.zeros((B, 8, CONV_DIM), F32)
        h0 = jnp.zeros((B, SSM_WIDTH, SSM_STATE), F32)
        t_in = T = min(256, S)
        new_conv = xbc[:, -(CONV_W - 1):]
    ssm, h_last = _ssd(xbc, z, dt_raw, conv_prev8, h0, prm['conv_w'], prm['conv_b'], prm['dt_bias'],
                       prm['a_log'], prm['d_skip'], prm['ssm_norm_w'], t_in=t_in, T=T)
    x1, h2, gate = _mix(x, att, ssm, mod4, prm['w_out'], prm['ln1_g'], prm['ln1_b'], prm['router_wt'],
                        prm['router_b'], bb=bb, ts=ts, alpha=alpha)
    if not sample:
        ts = min(1024, S)
    y = _moe(h2, gate, x1, mod4, prm['wg'], prm['wu'], prm['wd'], prm['sg'], prm['su'], prm['sd'],
             prm['ln2_g'], prm['ln2_b'], bb=bb, ts=ts, alpha=alpha)
    return (y, k.reshape(B, S, DA_HEADS, 2 * DA_HEAD_DIM), v.reshape(B, S, DA_HEADS, 2 * DA_HEAD_DIM),
            new_conv, h_last.reshape(B, SSM_HEADS, SSM_HEAD_DIM, SSM_STATE))


def _prep_params(w_in, lq1, lk1, lq2, lk2, subln_w, conv_w, conv_b, dt_bias, a_log, d_skip, ssm_norm_w, w_out,
                 ln1_g, ln1_b, router_w, router_bias, ewg, ewu, ewd, swg, swu, swd, ln2_g, ln2_b):
    pad8 = lambda a: jnp.pad(a.reshape(1, -1), ((0, 0), (0, LANES - a.shape[-1])))
    row = lambda a: a.reshape(1, -1)
    perm = (jnp.arange(N_EXPERTS) % N_EXPERT_GROUPS) * EXPERTS_PER_GROUP + jnp.arange(N_EXPERTS) // N_EXPERT_GROUPS
    rwt = jnp.pad(router_w.T[perm], ((0, LANES - N_EXPERTS), (0, 0)))
    rb = jnp.pad(router_bias[perm], (0, LANES - N_EXPERTS)).reshape(LANES, 1)
    grouped = lambda w: w.astype(BF16).reshape((N_EXPERT_GROUPS, EXPERTS_PER_GROUP) + w.shape[1:])
    return {
        'w_main': w_in[:, :DT_OFF].astype(BF16),
        'w_dt': jnp.pad(w_in[:, DT_OFF:], ((0, 0), (0, LANES - SSM_HEADS))),
        'w_vt': w_in[:, V_OFF:Z_OFF].T.astype(BF16),
        'lam_vecs': (row(lq1), row(lk1), row(lq2), row(lk2)),
        'subln_w': row(subln_w),
        'conv_w': conv_w, 'conv_b': row(conv_b),
        'dt_bias': pad8(dt_bias), 'a_log': pad8(a_log),
        'd_skip': row(jnp.repeat(d_skip, SSM_HEAD_DIM)), 'ssm_norm_w': row(ssm_norm_w),
        'w_out': w_out.astype(BF16), 'ln1_g': row(ln1_g), 'ln1_b': row(ln1_b),
        'router_wt': rwt, 'router_b': rb,
        'wg': grouped(ewg), 'wu': grouped(ewu), 'wd': grouped(ewd),
        'sg': swg.astype(BF16), 'su': swu.astype(BF16), 'sd': swd.astype(BF16),
        'ln2_g': row(ln2_g), 'ln2_b': row(ln2_b),
    }


def kernel(x_prompt, x_sample, c_prompt, c_sample, cache_k, cache_v, state_conv, state_ssm, w_ada, b_ada, w_in, lambda_q1, lambda_k1, lambda_q2, lambda_k2, subln_w, conv_w, conv_b, dt_bias, a_log, d_skip, ssm_norm_w, w_out, ln1_g, ln1_b, router_w, router_bias, exp_w_gate, exp_w_up, exp_w_down, sh_w_gate, sh_w_up, sh_w_down, ln2_g, ln2_b):
    depth = w_ada.shape[0]
    bp, sp, _ = x_prompt.shape
    bs, ss, _ = x_sample.shape
    past = cache_k.shape[2]
    pos_prompt = jnp.arange(sp)
    pos_sample = past + jnp.arange(ss)
    c_rows = bp + bs
    c_pad = (-c_rows) % 8
    c_all = jnp.pad(jnp.concatenate([c_prompt, c_sample], 0), ((0, c_pad), (0, 0)))
    yp, ys = x_prompt, x_sample
    outs = [[] for _ in range(8)]
    for l in range(depth):
        prm = _prep_params(w_in[l], lambda_q1[l], lambda_k1[l], lambda_q2[l], lambda_k2[l], subln_w[l],
                           conv_w[l], conv_b[l], dt_bias[l], a_log[l], d_skip[l], ssm_norm_w[l], w_out[l],
                           ln1_g[l], ln1_b[l], router_w[l], router_bias[l], exp_w_gate[l], exp_w_up[l],
                           exp_w_down[l], sh_w_gate[l], sh_w_up[l], sh_w_down[l], ln2_g[l], ln2_b[l])
        mod = _ada(c_all, w_ada[l], b_ada[l]).reshape(c_rows + c_pad, 6, 1, D_MODEL)
        yp, kp, vp, cp, hp = _layer(yp, mod[:bp], pos_prompt, prm, l, depth, None, None, None, None, False)
        ys, kn, vn, cn, hn = _layer(ys, mod[bp:c_rows], pos_sample, prm, l, depth,
                                    cache_k[l].reshape(bs, past * DA_HEADS, LANES),
                                    cache_v[l].reshape(bs, past * DA_HEADS, LANES),
                                    state_conv[l], state_ssm[l], True)
        for lst, val in zip(outs, (kp, vp, cp, hp, kn, vn, cn, hn)):
            lst.append(val)
    return (yp, ys) + tuple(jnp.stack(o) for o in outs)
```

```python
import functools
import math

import jax
import jax.numpy as jnp
from jax import lax
from jax.experimental import pallas as pl
from jax.experimental.pallas import tpu as pltpu

F32 = jnp.float32
BF16 = jnp.bfloat16

D_MODEL = 1024
CHUNK = 64
CHUNK_SHIFT = 6
HEAD_SHIFT = 6
EPS = 1e-5
LOG2E = math.log2(math.e)
DA_HEADS = 4
DA_HEAD_DIM = 64
DA_WIDTH = DA_HEADS * 2 * DA_HEAD_DIM
ROPE_THETA = 10000.0
SSM_HEADS = 8
SSM_HEAD_DIM = 64
SSM_WIDTH = SSM_HEADS * SSM_HEAD_DIM
SSM_GROUPS = 2
SSM_STATE = 128
CONV_W = 4
CONV_DIM = SSM_WIDTH + 2 * SSM_GROUPS * SSM_STATE
Q_OFF = 0
K_OFF = 512
V_OFF = 1024
Z_OFF = 1536
XBC_OFF = 2048
DT_OFF = 3072
N_EXPERTS = 64
N_EXPERT_GROUPS = 8
EXPERTS_PER_GROUP = 8
TOPK_GROUPS = 4
TOP_K = 8
EXPERT_DIM = 256
ROUTED_SCALE = 2.5

LANES = 128
VMEM_LIMIT = 56 * 1024 * 1024


def _cparams(*sem):
    return pltpu.CompilerParams(dimension_semantics=sem, vmem_limit_bytes=VMEM_LIMIT)


def _dot(a, b):
    return jnp.dot(a, b, preferred_element_type=F32)


def _dot_nt(a, b):
    return lax.dot_general(a, b, (((1,), (1,)), ((), ())), preferred_element_type=F32)


def _split2(a):
    hi = a.astype(BF16)
    lo = (a - hi.astype(F32)).astype(BF16)
    return hi, lo


def _split3(a):
    hi = a.astype(BF16)
    r = a - hi.astype(F32)
    mid = r.astype(BF16)
    lo = (r - mid.astype(F32)).astype(BF16)
    return hi, mid, lo


def _dot3(a, b, dot=_dot):
    ah, al = _split2(a)
    bh, bl = _split2(b)
    return dot(ah, bh) + (dot(ah, bl) + dot(al, bh))


def _dot_exact_lhs(e, a, dot=_dot):
    ah, am, al = _split3(a)
    return dot(e, ah) + (dot(e, am) + dot(e, al))


def _silu(x):
    return x * jax.nn.sigmoid(x)


def _softplus(x):
    return jnp.maximum(x, 0.0) + jnp.log1p(jnp.exp(-jnp.abs(x)))


def _ln(x):
    mu = jnp.mean(x, -1, keepdims=True)
    xc = x - mu
    var = jnp.mean(xc * xc, -1, keepdims=True)
    return xc * lax.rsqrt(var + EPS)


def _ada_kernel(c_ref, w_ref, b_ref, o_ref):
    o_ref[...] = _dot3(_silu(c_ref[...]), w_ref[...]) + b_ref[...]


def _ada(c, w_ada, b_ada):
    rows = c.shape[0]
    n = w_ada.shape[1]
    tn = 1024
    return pl.pallas_call(
        _ada_kernel,
        grid=(n // tn,),
        in_specs=[pl.BlockSpec((rows, D_MODEL), lambda i: (0, 0)),
                  pl.BlockSpec((D_MODEL, tn), lambda i: (0, i)),
                  pl.BlockSpec((1, tn), lambda i: (0, i))],
        out_specs=pl.BlockSpec((rows, tn), lambda i: (0, i)),
        out_shape=jax.ShapeDtypeStruct((rows, n), F32),
        compiler_params=_cparams("arbitrary"),
        name="ada",
    )(c, w_ada, b_ada.reshape(1, n))


def _proj_kernel(x_ref, sc_ref, sh_ref, cos_ref, sin_ref, w_ref, wdt_ref, wvt_ref,
                 q_ref, k_ref, kb_ref, v_ref, vb_ref, z_ref, xbc_ref, dt_ref, *, bb, ts, v_transposed):
    tm = bb * ts
    h = _ln(x_ref[...]) * (1.0 + sc_ref[...]) + sh_ref[...]
    h = h.reshape(tm, D_MODEL)
    hb = h.astype(BF16)

    cos = jnp.tile(cos_ref[...], (1, DA_WIDTH // LANES))
    sin = jnp.tile(sin_ref[...], (1, DA_WIDTH // LANES))
    lane = lax.broadcasted_iota(jnp.int32, (tm, DA_WIDTH), 1)
    first = (lane & (DA_HEAD_DIM - 1)) < (DA_HEAD_DIM // 2)

    def rope(t):
        rot = jnp.where(first, pltpu.roll(t, DA_WIDTH - DA_HEAD_DIM // 2, 1),
                        pltpu.roll(t, DA_HEAD_DIM // 2, 1))
        return t * cos + rot * sin

    q = rope(_dot(hb, w_ref[:, Q_OFF:K_OFF])) * (DA_HEAD_DIM ** -0.5 * LOG2E)
    q_ref[...] = q.reshape(bb, ts, DA_WIDTH).astype(BF16)
    k = rope(_dot(hb, w_ref[:, K_OFF:V_OFF])).reshape(bb, ts, DA_WIDTH)
    k_ref[...] = k
    kb_ref[...] = k.astype(BF16)
    v = _dot(hb, w_ref[:, V_OFF:Z_OFF]).reshape(bb, ts, DA_WIDTH)
    v_ref[...] = v
    if v_transposed:
        vb_ref[0, 0] = _dot_nt(wvt_ref[...], hb).astype(BF16)
    else:
        vb_ref[...] = v.astype(BF16)
    z_ref[...] = _dot(hb, w_ref[:, Z_OFF:XBC_OFF]).reshape(bb, ts, SSM_WIDTH)
    xbc_ref[...] = _dot(hb, w_ref[:, XBC_OFF:DT_OFF]).reshape(bb, ts, CONV_DIM)
    dt_ref[...] = _dot3(h, wdt_ref[...]).reshape(bb, ts, LANES)


def _proj(x, mod4, cos_t, sin_t, w_main, w_dt, w_vt, *, bb, ts, v_transposed):
    B, S, _ = x.shape
    tm = bb * ts
    grid = (S // ts, B // bb)
    row3 = lambda w: pl.BlockSpec((bb, ts, w), lambda i, b: (b, i, 0))
    modspec = lambda which: pl.BlockSpec((bb, None, 1, D_MODEL), lambda i, b: (b, which, 0, 0))
    if v_transposed:
        vb_shape = jax.ShapeDtypeStruct((B, S // ts, DA_WIDTH, ts), BF16)
        vb_spec = pl.BlockSpec((1, 1, DA_WIDTH, ts), lambda i, b: (b, i, 0, 0))
    else:
        vb_shape = jax.ShapeDtypeStruct((B, S, DA_WIDTH), BF16)
        vb_spec = row3(DA_WIDTH)
    out_shapes = [jax.ShapeDtypeStruct((B, S, DA_WIDTH), BF16),
                  jax.ShapeDtypeStruct((B, S, DA_WIDTH), F32),
                  jax.ShapeDtypeStruct((B, S, DA_WIDTH), BF16),
                  jax.ShapeDtypeStruct((B, S, DA_WIDTH), F32),
                  vb_shape,
                  jax.ShapeDtypeStruct((B, S, SSM_WIDTH), F32),
                  jax.ShapeDtypeStruct((B, S, CONV_DIM), F32),
                  jax.ShapeDtypeStruct((B, S, LANES), F32)]
    out_specs = [row3(DA_WIDTH)] * 4 + [vb_spec, row3(SSM_WIDTH), row3(CONV_DIM), row3(LANES)]
    return pl.pallas_call(
        functools.partial(_proj_kernel, bb=bb, ts=ts, v_transposed=v_transposed),
        grid=grid,
        in_specs=[row3(D_MODEL), modspec(1), modspec(0),
                  pl.BlockSpec((tm, LANES), lambda i, b: (i, 0)),
                  pl.BlockSpec((tm, LANES), lambda i, b: (i, 0)),
                  pl.BlockSpec((D_MODEL, DT_OFF), lambda i, b: (0, 0)),
                  pl.BlockSpec((D_MODEL, LANES), lambda i, b: (0, 0)),
                  pl.BlockSpec((DA_WIDTH, D_MODEL), lambda i, b: (0, 0))],
        out_specs=out_specs,
        out_shape=out_shapes,
        compiler_params=_cparams("arbitrary", "arbitrary"),
        name="proj",
    )(x, mod4, mod4, cos_t, sin_t, w_main, w_dt, w_vt)


def _lambda(lq1, lk1, lq2, lk2, lam_init):
    s1 = jnp.sum(lq1[...] * lk1[...], axis=1, keepdims=True)
    s2 = jnp.sum(lq2[...] * lk2[...], axis=1, keepdims=True)
    return jnp.exp(s1) - jnp.exp(s2) + lam_init


def _subln(o, subw, lam_init):
    ms = jnp.mean(o * o, -1, keepdims=True)
    return o * lax.rsqrt(ms + EPS) * subw * (1.0 - lam_init)


def _attn_prompt_kernel(lq1, lk1, lq2, lk2, subw_ref, q_ref, k_ref, vt_ref, o_ref,
                        acc1, acc2, sa1, sa2, sb1, sb2, mxa, mxb, m1, l1, m2, l2, *, tq, tk, lam_init):
    i = pl.program_id(2)
    qt = q_ref[0].astype(F32).T
    row = lax.broadcasted_iota(jnp.int32, (LANES, tq), 0)
    q1t = jnp.where(row < DA_HEAD_DIM, qt, 0.0).astype(BF16)
    q2t = jnp.where(row >= DA_HEAD_DIM, qt, 0.0).astype(BF16)
    acc1[...] = jnp.zeros_like(acc1)
    acc2[...] = jnp.zeros_like(acc2)
    m1[...] = jnp.full_like(m1, -jnp.inf)
    m2[...] = jnp.full_like(m2, -jnp.inf)
    l1[...] = jnp.zeros_like(l1)
    l2[...] = jnp.zeros_like(l2)
    buf_a = (sa1, sa2, mxa)
    buf_b = (sb1, sb2, mxb)

    def produce(j, buf):
        k = k_ref[0, pl.ds(pl.multiple_of(j * tk, tk), tk), :]
        for half, qh in enumerate((q1t, q2t)):
            s = _dot(k, qh)
            buf[half][...] = s
            buf[2][half:half + 1, :] = jnp.max(s, axis=0, keepdims=True)

    def consume(j, buf, masked):
        vt = vt_ref[0, j]
        if masked:
            key = (j * tk + lax.broadcasted_iota(jnp.int32, (tk, tq), 0)) >> CHUNK_SHIFT
            qry = (i * tq + lax.broadcasted_iota(jnp.int32, (tk, tq), 1)) >> CHUNK_SHIFT
            keep = key <= qry
        for half, (m, l, acc) in enumerate(((m1, l1, acc1), (m2, l2, acc2))):
            st = buf[half][...]
            if masked:
                st = jnp.where(keep, st, -jnp.inf)
                tile_max = jnp.max(st, axis=0, keepdims=True)
            else:
                tile_max = buf[2][half:half + 1, :]
            mo = m[...]
            mn = jnp.maximum(mo, tile_max)
            a = jnp.exp2(mo - mn)
            p = jnp.exp2(st - mn)
            l[...] = a * l[...] + jnp.sum(p, axis=0, keepdims=True)
            m[...] = mn
            acc[...] = a * acc[...] + _dot(vt, p.astype(BF16))

    nfull = (i * tq) // tk
    produce(0, buf_a)

    def pair(jj, carry):
        produce(2 * jj + 1, buf_b)
        consume(2 * jj, buf_a, False)
        produce(2 * jj + 2, buf_a)
        consume(2 * jj + 1, buf_b, False)
        return carry

    lax.fori_loop(0, nfull >> 1, pair, 0)
    t0 = (nfull >> 1) * 2

    @pl.when((nfull & 1) == 0)
    def _():
        consume(t0, buf_a, True)

    @pl.when((nfull & 1) == 1)
    def _():
        produce(t0 + 1, buf_b)
        consume(t0, buf_a, False)
        consume(t0 + 1, buf_b, True)

    lam = _lambda(lq1, lk1, lq2, lk2, lam_init)
    ot = acc1[...] / l1[...] - lam * (acc2[...] / l2[...])
    ot = ot * lax.rsqrt(jnp.mean(ot * ot, axis=0, keepdims=True) + EPS)
    o_ref[0] = (ot.T * subw_ref[...] * (1.0 - lam_init)).astype(BF16)


def _attn_prompt(qb, kb, vt, lam_vecs, subw, lam_init, *, tq):
    B, S, _ = qb.shape
    nkt, tk = vt.shape[1], vt.shape[3]
    small = pl.BlockSpec((1, DA_HEAD_DIM), lambda b, h, i: (0, 0))
    return pl.pallas_call(
        functools.partial(_attn_prompt_kernel, tq=tq, tk=tk, lam_init=lam_init),
        grid=(B, DA_HEADS, S // tq),
        in_specs=[small, small, small, small,
                  pl.BlockSpec((1, LANES), lambda b, h, i: (0, 0)),
                  pl.BlockSpec((1, tq, LANES), lambda b, h, i: (b, i, h)),
                  pl.BlockSpec((1, S, LANES), lambda b, h, i: (b, 0, h)),
                  pl.BlockSpec((1, nkt, LANES, tk), lambda b, h, i: (b, 0, h, 0))],
        out_specs=pl.BlockSpec((1, tq, LANES), lambda b, h, i: (b, i, h)),
        out_shape=jax.ShapeDtypeStruct((B, S, DA_WIDTH), BF16),
        scratch_shapes=([pltpu.VMEM((LANES, tq), F32)] * 2 + [pltpu.VMEM((tk, tq), F32)] * 4
                        + [pltpu.VMEM((2, tq), F32)] * 2 + [pltpu.VMEM((1, tq), F32)] * 4),
        compiler_params=_cparams("arbitrary", "arbitrary", "arbitrary"),
        name="attn_prompt",
    )(*lam_vecs, subw, qb, kb, vt)


def _attn_sample_kernel(lq1, lk1, lq2, lk2, subw_ref, q_ref, kn_ref, vn_ref, kp_ref, vp_ref, o_ref,
                        *, lam_init):
    s = q_ref.shape[1]
    lam = _lambda(lq1, lk1, lq2, lk2, lam_init)
    lane = lax.broadcasted_iota(jnp.int32, (s, LANES), 1)
    outs = []
    for h in range(DA_HEADS):
        cols = slice(h * LANES, (h + 1) * LANES)
        q = q_ref[0, :, cols]
        zero = jnp.zeros_like(q)
        kn = kn_ref[0, :, cols]
        vn = vn_ref[0, :, cols]
        kp = kp_ref[:, h, :].astype(BF16)
        vp = vp_ref[:, h, :].astype(BF16)
        probs = []
        for qm in (jnp.where(lane < DA_HEAD_DIM, q, zero), jnp.where(lane >= DA_HEAD_DIM, q, zero)):
            sp = _dot_nt(qm, kp)
            sn = _dot_nt(qm, kn)
            m = jnp.maximum(jnp.max(sp, -1, keepdims=True), jnp.max(sn, -1, keepdims=True))
            pp = jnp.exp2(sp - m)
            pn = jnp.exp2(sn - m)
            l = jnp.sum(pp, -1, keepdims=True) + jnp.sum(pn, -1, keepdims=True)
            probs.append((pp / l, pn / l))
        ap = probs[0][0] - lam * probs[1][0]
        an = probs[0][1] - lam * probs[1][1]
        o = _dot(ap.astype(BF16), vp) + _dot(an.astype(BF16), vn)
        outs.append(_subln(o, subw_ref[...], lam_init))
    o_ref[0] = jnp.concatenate(outs, axis=-1).astype(BF16)


def _attn_sample(qb, kb, vb, cache_k, cache_v, layer_idx, lam_vecs, subw, lam_init):
    B, s, _ = qb.shape
    past = cache_k.shape[2]
    small = pl.BlockSpec((1, DA_HEAD_DIM), lambda b: (0, 0))
    new = pl.BlockSpec((1, s, DA_WIDTH), lambda b: (b, 0, 0))
    old = pl.BlockSpec((None, None, past, DA_HEADS, LANES), lambda b: (layer_idx, b, 0, 0, 0))
    return pl.pallas_call(
        functools.partial(_attn_sample_kernel, lam_init=lam_init),
        grid=(B,),
        in_specs=[small, small, small, small, pl.BlockSpec((1, LANES), lambda b: (0, 0)),
                  new, new, new, old, old],
        out_specs=new,
        out_shape=jax.ShapeDtypeStruct((B, s, DA_WIDTH), BF16),
        compiler_params=_cparams("arbitrary"),
        name="attn_sample",
    )(*lam_vecs, subw, qb, kb, vb, cache_k, cache_v)


def _ssd_kernel(xbc_ref, z_ref, dt_ref, cp_ref, h0_ref, cw_ref, cb_ref, dtb_ref, alog_ref, dsk_ref, nw_ref,
                y_ref, hout_ref, tail_ref, st_ref, *, t_in, T):
    j = pl.program_id(1)
    PAIR = 2 * SSM_HEAD_DIM

    @pl.when(j == 0)
    def _():
        tail_ref[...] = cp_ref[0]
        st_ref[...] = h0_ref[0]

    def rows(ref, width):
        val = ref[0]
        if T == t_in:
            return val
        return jnp.concatenate([val, jnp.zeros((T - t_in, width), F32)], axis=0)

    xb = rows(xbc_ref, CONV_DIM)
    xp = jnp.concatenate([tail_ref[...], xb], axis=0)
    conv = cb_ref[...] + cw_ref[CONV_W - 1:CONV_W, :] * xb
    for sft in range(1, CONV_W):
        conv = conv + cw_ref[CONV_W - 1 - sft:CONV_W - sft, :] * pltpu.roll(xp, sft, 0)[8:, :]
    if T == t_in:
        tail_ref[...] = xb[T - 8:, :]
    xa = _silu(conv)
    xs = xa[:, :SSM_WIDTH]

    dt = _softplus(rows(dt_ref, LANES) + dtb_ref[...])
    if T != t_in:
        valid = lax.broadcasted_iota(jnp.int32, (T, LANES), 0) < t_in
        dt = jnp.where(valid, dt, 0.0)
    da = dt * (-jnp.exp(alog_ref[...]))

    rr = lax.broadcasted_iota(jnp.int32, (T, T), 0)
    cc = lax.broadcasted_iota(jnp.int32, (T, T), 1)
    causal = cc <= rr
    tril = jnp.where(causal, 1.0, 0.0).astype(BF16)
    acs = _dot_exact_lhs(tril, da)
    e16 = jnp.where(lax.broadcasted_iota(jnp.int32, (16, LANES), 0)
                    == lax.broadcasted_iota(jnp.int32, (16, LANES), 1), 1.0, 0.0).astype(BF16)
    acs_t = _dot_exact_lhs(e16, acs, dot=_dot_nt)

    last = acs[T - 1:T, :]
    expand = jnp.where((lax.broadcasted_iota(jnp.int32, (LANES, SSM_WIDTH), 1) >> HEAD_SHIFT)
                       == lax.broadcasted_iota(jnp.int32, (LANES, SSM_WIDTH), 0), 1.0, 0.0).astype(BF16)

    def per_head_lanes(v):
        hi, lo = _split2(v)
        return _dot(hi, expand) + _dot(lo, expand)

    xdt = xs * per_head_lanes(dt)
    xdd = xdt * per_head_lanes(jnp.exp(last - acs))
    eacs = per_head_lanes(jnp.exp(acs))

    head_of_row = lax.broadcasted_iota(jnp.int32, (SSM_WIDTH, LANES), 0) >> HEAD_SHIFT
    sel = head_of_row == lax.broadcasted_iota(jnp.int32, (SSM_WIDTH, LANES), 1)
    rowdec = jnp.sum(jnp.where(sel, jnp.exp(last), 0.0), axis=1, keepdims=True)

    lane = lax.broadcasted_iota(jnp.int32, (T, PAIR), 1)
    ys = []
    for p in range(SSM_HEADS // 2):
        g = (2 * p) // (SSM_HEADS // SSM_GROUPS)
        bm = xa[:, SSM_WIDTH + g * SSM_STATE:SSM_WIDTH + (g + 1) * SSM_STATE].astype(BF16)
        cm = xa[:, SSM_WIDTH + (SSM_GROUPS + g) * SSM_STATE:
                SSM_WIDTH + (SSM_GROUPS + g + 1) * SSM_STATE].astype(BF16)
        cb = _dot_nt(cm, bm)
        cols = slice(p * PAIR, (p + 1) * PAIR)
        xdt_p = xdt[:, cols].astype(BF16)
        yd = []
        for hh in (2 * p, 2 * p + 1):
            seg = acs[:, hh:hh + 1] - acs_t[hh:hh + 1, :]
            lmat = jnp.exp(jnp.where(causal, seg, -jnp.inf))
            yd.append(_dot((cb * lmat).astype(BF16), xdt_p))
        y_diag = jnp.where(lane < SSM_HEAD_DIM, yd[0], yd[1])
        st = st_ref[cols, :]
        y_off = _dot_nt(cm, st.astype(BF16)) * eacs[:, cols]
        ys.append(y_diag + y_off)
        st_ref[cols, :] = st * rowdec[cols, :] + _dot(xdd[:, cols].T.astype(BF16), bm)
    y = jnp.concatenate(ys, axis=-1) + xs * dsk_ref[...]
    y = y * _silu(rows(z_ref, SSM_WIDTH))
    half = SSM_WIDTH // SSM_GROUPS
    outs = []
    for g in range(SSM_GROUPS):
        yg = y[:, g * half:(g + 1) * half]
        outs.append(yg * lax.rsqrt(jnp.mean(yg * yg, -1, keepdims=True) + EPS))
    y = jnp.concatenate(outs, axis=-1) * nw_ref[...]
    y_ref[0] = y[:t_in, :].astype(BF16)

    @pl.when(j == pl.num_programs(1) - 1)
    def _():
        hout_ref[0] = st_ref[...]


def _ssd(xbc, z, dt_raw, conv_prev8, h0, conv_w, conv_b, dtb, alog, dsk, nw, *, t_in, T):
    B, S, _ = xbc.shape
    row = lambda w: pl.BlockSpec((1, t_in, w), lambda b, j: (b, j, 0))
    per_b = lambda r, w: pl.BlockSpec((1, r, w), lambda b, j: (b, 0, 0))
    const = lambda r, w: pl.BlockSpec((r, w), lambda b, j: (0, 0))
    return pl.pallas_call(
        functools.partial(_ssd_kernel, t_in=t_in, T=T),
        grid=(B, S // t_in),
        in_specs=[row(CONV_DIM), row(SSM_WIDTH), row(LANES),
                  per_b(8, CONV_DIM), per_b(SSM_WIDTH, SSM_STATE),
                  const(CONV_W, CONV_DIM), const(1, CONV_DIM), const(1, LANES), const(1, LANES),
                  const(1, SSM_WIDTH), const(1, SSM_WIDTH)],
        out_specs=[row(SSM_WIDTH), per_b(SSM_WIDTH, SSM_STATE)],
        out_shape=[jax.ShapeDtypeStruct((B, S, SSM_WIDTH), BF16),
                   jax.ShapeDtypeStruct((B, SSM_WIDTH, SSM_STATE), F32)],
        scratch_shapes=[pltpu.VMEM((8, CONV_DIM), F32), pltpu.VMEM((SSM_WIDTH, SSM_STATE), F32)],
        compiler_params=_cparams("arbitrary", "arbitrary"),
        name="ssd",
    )(xbc, z, dt_raw, conv_prev8, h0, conv_w, conv_b, dtb, alog, dsk, nw)


def _router(h, rwt_ref, rb_ref):
    tm = h.shape[0]
    G = N_EXPERT_GROUPS
    logits = _dot3(rwt_ref[...], h, dot=_dot_nt)
    score = jax.nn.sigmoid(logits[:N_EXPERTS, :])
    biased = score + rb_ref[:N_EXPERTS, :]
    sc = [score[G * j:G * (j + 1), :] for j in range(EXPERTS_PER_GROUP)]
    v = [biased[G * j:G * (j + 1), :] for j in range(EXPERTS_PER_GROUP)]

    grp = jnp.zeros((G, tm), F32)
    for j in range(EXPERTS_PER_GROUP):
        cnt = jnp.zeros((G, tm), jnp.int32)
        for j2 in range(EXPERTS_PER_GROUP):
            if j2 == j:
                continue
            beats = (v[j2] >= v[j]) if j2 < j else (v[j2] > v[j])
            cnt = cnt + beats.astype(jnp.int32)
        grp = grp + jnp.where(cnt < 2, v[j], 0.0)

    gid = lax.broadcasted_iota(jnp.int32, (G, tm), 0)
    cnt = jnp.zeros((G, tm), jnp.int32)
    for g2 in range(G):
        row = grp[g2:g2 + 1, :]
        beats = (row > grp) | ((row == grp) & (gid > g2))
        cnt = cnt + beats.astype(jnp.int32)
    gmask = cnt < TOPK_GROUPS

    mv = [jnp.where(gmask, v[j], -jnp.inf) for j in range(EXPERTS_PER_GROUP)]
    cnts = [jnp.zeros((G, tm), jnp.int32) for _ in range(EXPERTS_PER_GROUP)]
    for j2 in range(EXPERTS_PER_GROUP):
        for g2 in range(G):
            row = mv[j2][g2:g2 + 1, :]
            for j in range(EXPERTS_PER_GROUP):
                earlier = (gid > g2) | (gid == g2) if j2 < j else (gid > g2)
                beats = (row > mv[j]) | ((row == mv[j]) & earlier)
                cnts[j] = cnts[j] + beats.astype(jnp.int32)
    w = [jnp.where(cnts[j] < TOP_K, sc[j], 0.0) for j in range(EXPERTS_PER_GROUP)]
    tot = w[0]
    for j in range(1, EXPERTS_PER_GROUP):
        tot = tot + w[j]
    tot = jnp.sum(tot, axis=0, keepdims=True)
    gates = [w[j] / tot * ROUTED_SCALE for j in range(EXPERTS_PER_GROUP)]
    return jnp.concatenate(gates + [jnp.zeros((LANES - N_EXPERTS, tm), F32)], axis=0)


def _mix_kernel(x_ref, att_ref, ssm_ref, g1_ref, sc2_ref, sh2_ref, wo_ref, l1g_ref, l1b_ref, rwt_ref, rb_ref,
                x1_ref, h2_ref, gate_ref, *, bb, ts, alpha):
    tm = bb * ts
    att = att_ref[...].reshape(tm, DA_WIDTH)
    ssm = ssm_ref[...].reshape(tm, SSM_WIDTH)
    mix = _dot(att, wo_ref[:DA_WIDTH, :]) + _dot(ssm, wo_ref[DA_WIDTH:, :])
    y = alpha * x_ref[...] + (1.0 + g1_ref[...]) * mix.reshape(bb, ts, D_MODEL)
    x1 = _ln(y) * l1g_ref[...] + l1b_ref[...]
    x1_ref[...] = x1
    h2 = (_ln(x1) * (1.0 + sc2_ref[...]) + sh2_ref[...]).reshape(tm, D_MODEL)
    h2_ref[...] = h2.astype(BF16)
    gate_ref[...] = _router(h2, rwt_ref, rb_ref).T


def _mix(x, att, ssm, mod4, wo, l1g, l1b, rwt, rb, *, bb, ts, alpha):
    B, S, _ = x.shape
    tm = bb * ts
    row3 = lambda w: pl.BlockSpec((bb, ts, w), lambda i, b: (b, i, 0))
    modspec = lambda which: pl.BlockSpec((bb, None, 1, D_MODEL), lambda i, b: (b, which, 0, 0))
    const = lambda r, w: pl.BlockSpec((r, w), lambda i, b: (0, 0))
    flat = lambda w: pl.BlockSpec((tm, w), lambda i, b: (b * (S // ts) + i, 0))
    return pl.pallas_call(
        functools.partial(_mix_kernel, bb=bb, ts=ts, alpha=alpha),
        grid=(S // ts, B // bb),
        in_specs=[row3(D_MODEL), row3(DA_WIDTH), row3(SSM_WIDTH), modspec(2), modspec(4), modspec(3),
                  const(D_MODEL, D_MODEL), const(1, D_MODEL), const(1, D_MODEL),
                  const(LANES, D_MODEL), const(LANES, 1)],
        out_specs=[row3(D_MODEL), flat(D_MODEL), flat(LANES)],
        out_shape=[jax.ShapeDtypeStruct((B, S, D_MODEL), F32),
                   jax.ShapeDtypeStruct((B * S, D_MODEL), BF16),
                   jax.ShapeDtypeStruct((B * S, LANES), F32)],
        compiler_params=_cparams("arbitrary", "arbitrary"),
        name="mix",
    )(x, att, ssm, mod4, mod4, mod4, wo, l1g, l1b, rwt, rb)


EXPERTS_PER_STEP = 4


def _moe_kernel(h_ref, gate_ref, x1_ref, g2_ref, wg_ref, wu_ref, wd_ref, sg_ref, su_ref, sd_ref,
                l2g_ref, l2b_ref, o_ref, acc_ref, *, bb, ts, alpha):
    tm = bb * ts
    e = pl.program_id(2)
    h = h_ref[...]

    @pl.when(e == 0)
    def _():
        acc_ref[...] = jnp.zeros_like(acc_ref)

    gate = gate_ref[...]
    lane = lax.broadcasted_iota(jnp.int32, (tm, LANES), 1)
    hids = []
    for s in range(EXPERTS_PER_STEP):
        col = e * EXPERTS_PER_STEP + s
        gcol = jnp.sum(jnp.where(lane == col, gate, 0.0), axis=1, keepdims=True)
        hid = _silu(_dot(h, wg_ref[s])) * _dot(h, wu_ref[s]) * gcol
        hids.append(hid.astype(BF16))
    hid = jnp.concatenate(hids, axis=-1)
    acc_ref[...] += _dot(hid, wd_ref[...].reshape(EXPERTS_PER_STEP * EXPERT_DIM, D_MODEL))

    @pl.when(e == pl.num_programs(2) - 1)
    def _():
        shared = _dot((_silu(_dot(h, sg_ref[...])) * _dot(h, su_ref[...])).astype(BF16), sd_ref[...])
        ffn = (acc_ref[...] + shared).reshape(bb, ts, D_MODEL)
        y = alpha * x1_ref[...] + (1.0 + g2_ref[...]) * ffn
        o_ref[...] = _ln(y) * l2g_ref[...] + l2b_ref[...]


def _expert_of_column(c):
    return (c % N_EXPERT_GROUPS) * EXPERTS_PER_GROUP + c // N_EXPERT_GROUPS


def _moe(h2, gate, x1, mod4, wg, wu, wd, sg, su, sd, l2g, l2b, *, bb, ts, alpha):
    B, S, _ = x1.shape
    tm = bb * ts
    row3 = lambda w: pl.BlockSpec((bb, ts, w), lambda i, b, e: (b, i, 0))
    flat = lambda w: pl.BlockSpec((tm, w), lambda i, b, e: (b * (S // ts) + i, 0))
    const = lambda r, w: pl.BlockSpec((r, w), lambda i, b, e: (0, 0))
    ew_in = pl.BlockSpec((EXPERTS_PER_STEP, None, D_MODEL, EXPERT_DIM),
                         lambda i, b, e: (e % (N_EXPERT_GROUPS // EXPERTS_PER_STEP), e // (N_EXPERT_GROUPS // EXPERTS_PER_STEP), 0, 0))
    ew_out = pl.BlockSpec((EXPERTS_PER_STEP, None, EXPERT_DIM, D_MODEL),
                          lambda i, b, e: (e % (N_EXPERT_GROUPS // EXPERTS_PER_STEP), e // (N_EXPERT_GROUPS // EXPERTS_PER_STEP), 0, 0))
    return pl.pallas_call(
        functools.partial(_moe_kernel, bb=bb, ts=ts, alpha=alpha),
        grid=(S // ts, B // bb, N_EXPERTS // EXPERTS_PER_STEP),
        in_specs=[flat(D_MODEL), flat(LANES), row3(D_MODEL),
                  pl.BlockSpec((bb, None, 1, D_MODEL), lambda i, b, e: (b, 5, 0, 0)),
                  ew_in, ew_in, ew_out,
                  const(D_MODEL, EXPERT_DIM), const(D_MODEL, EXPERT_DIM), const(EXPERT_DIM, D_MODEL),
                  const(1, D_MODEL), const(1, D_MODEL)],
        out_specs=row3(D_MODEL),
        out_shape=jax.ShapeDtypeStruct((B, S, D_MODEL), F32),
        scratch_shapes=[pltpu.VMEM((tm, D_MODEL), F32)],
        compiler_params=_cparams("arbitrary", "arbitrary", "arbitrary"),
        name="moe",
    )(h2, gate, x1, mod4, wg, wu, wd, sg, su, sd, l2g, l2b)


def _rope_tables(pos):
    half = DA_HEAD_DIM // 2
    inv = ROPE_THETA ** (-jnp.arange(half, dtype=F32) / half)
    ang = pos.astype(F32)[:, None] * inv[None, :]
    cos = jnp.cos(ang)
    sin = jnp.sin(ang)
    reps = LANES // DA_HEAD_DIM
    cos_t = jnp.tile(jnp.concatenate([cos, cos], -1), (1, reps))
    sin_t = jnp.tile(jnp.concatenate([-sin, sin], -1), (1, reps))
    return cos_t, sin_t


def _layer(x, mod4, pos, prm, layer_idx, depth, k_past, v_past, conv_prev, ssm_prev, sample):
    B, S, _ = x.shape
    alpha = (2 * depth) ** 0.25
    lam_init = 0.8 - 0.6 * math.exp(-0.3 * layer_idx)
    cos_t, sin_t = _rope_tables(pos)
    if sample:
        bb, ts = B, S
        cos_t = jnp.tile(cos_t, (B, 1))
        sin_t = jnp.tile(sin_t, (B, 1))
    else:
        bb, ts = 1, min(512, S)
    qb, k, kb, v, vb, z, xbc, dt_raw = _proj(x, mod4, cos_t, sin_t, prm['w_main'], prm['w_dt'], prm['w_vt'],
                                             bb=bb, ts=ts, v_transposed=not sample)
    if sample:
        att = _attn_sample(qb, kb, vb, k_past, v_past, layer_idx, prm['lam_vecs'], prm['subln_w'], lam_init)
        conv_prev8 = jnp.pad(conv_prev, ((0, 0), (8 - (CONV_W - 1), 0), (0, 0)))
        h0 = ssm_prev.reshape(B, SSM_WIDTH, SSM_STATE)
        t_in, T = S, max(S, LANES)
        new_conv = jnp.concatenate([conv_prev, xbc], axis=1)[:, -(CONV_W - 1):]
    else:
        att = _attn_prompt(qb, kb, vb, prm['lam_vecs'], prm['subln_w'], lam_init, tq=min(512, S))
        conv_prev8 = jnp.zeros((B, 8, CONV_DIM), F32)
        h0 = jnp.zeros((B, SSM_WIDTH, SSM_STATE), F32)
        t_in = T = min(256, S)
        new_conv = xbc[:, -(CONV_W - 1):]
    ssm, h_last = _ssd(xbc, z, dt_raw, conv_prev8, h0, prm['conv_w'], prm['conv_b'], prm['dt_bias'],
                       prm['a_log'], prm['d_skip'], prm['ssm_norm_w'], t_in=t_in, T=T)
    x1, h2, gate = _mix(x, att, ssm, mod4, prm['w_out'], prm['ln1_g'], prm['ln1_b'], prm['router_wt'],
                        prm['router_b'], bb=bb, ts=ts, alpha=alpha)
    if not sample:
        ts = min(1024, S)
    y = _moe(h2, gate, x1, mod4, prm['wg'], prm['wu'], prm['wd'], prm['sg'], prm['su'], prm['sd'],
             prm['ln2_g'], prm['ln2_b'], bb=bb, ts=ts, alpha=alpha)
    return (y, k.reshape(B, S, DA_HEADS, 2 * DA_HEAD_DIM), v.reshape(B, S, DA_HEADS, 2 * DA_HEAD_DIM),
            new_conv, h_last.reshape(B, SSM_HEADS, SSM_HEAD_DIM, SSM_STATE))


def _prep_params(w_in, lq1, lk1, lq2, lk2, subln_w, conv_w, conv_b, dt_bias, a_log, d_skip, ssm_norm_w, w_out,
                 ln1_g, ln1_b, router_w, router_bias, ewg, ewu, ewd, swg, swu, swd, ln2_g, ln2_b):
    pad8 = lambda a: jnp.pad(a.reshape(1, -1), ((0, 0), (0, LANES - a.shape[-1])))
    row = lambda a: a.reshape(1, -1)
    perm = (jnp.arange(N_EXPERTS) % N_EXPERT_GROUPS) * EXPERTS_PER_GROUP + jnp.arange(N_EXPERTS) // N_EXPERT_GROUPS
    rwt = jnp.pad(router_w.T[perm], ((0, LANES - N_EXPERTS), (0, 0)))
    rb = jnp.pad(router_bias[perm], (0, LANES - N_EXPERTS)).reshape(LANES, 1)
    grouped = lambda w: w.astype(BF16).reshape((N_EXPERT_GROUPS, EXPERTS_PER_GROUP) + w.shape[1:])
    return {
        'w_main': w_in[:, :DT_OFF].astype(BF16),
        'w_dt': jnp.pad(w_in[:, DT_OFF:], ((0, 0), (0, LANES - SSM_HEADS))),
        'w_vt': w_in[:, V_OFF:Z_OFF].T.astype(BF16),
        'lam_vecs': (row(lq1), row(lk1), row(lq2), row(lk2)),
        'subln_w': row(subln_w),
        'conv_w': conv_w, 'conv_b': row(conv_b),
        'dt_bias': pad8(dt_bias), 'a_log': pad8(a_log),
        'd_skip': row(jnp.repeat(d_skip, SSM_HEAD_DIM)), 'ssm_norm_w': row(ssm_norm_w),
        'w_out': w_out.astype(BF16), 'ln1_g': row(ln1_g), 'ln1_b': row(ln1_b),
        'router_wt': rwt, 'router_b': rb,
        'wg': grouped(ewg), 'wu': grouped(ewu), 'wd': grouped(ewd),
        'sg': swg.astype(BF16), 'su': swu.astype(BF16), 'sd': swd.astype(BF16),
        'ln2_g': row(ln2_g), 'ln2_b': row(ln2_b),
    }


def kernel(x_prompt, x_sample, c_prompt, c_sample, cache_k, cache_v, state_conv, state_ssm, w_ada, b_ada, w_in, lambda_q1, lambda_k1, lambda_q2, lambda_k2, subln_w, conv_w, conv_b, dt_bias, a_log, d_skip, ssm_norm_w, w_out, ln1_g, ln1_b, router_w, router_bias, exp_w_gate, exp_w_up, exp_w_down, sh_w_gate, sh_w_up, sh_w_down, ln2_g, ln2_b):
    depth = w_ada.shape[0]
    bp, sp, _ = x_prompt.shape
    bs, ss, _ = x_sample.shape
    past = cache_k.shape[2]
    pos_prompt = jnp.arange(sp)
    pos_sample = past + jnp.arange(ss)
    c_rows = bp + bs
    c_pad = (-c_rows) % 8
    c_all = jnp.pad(jnp.concatenate([c_prompt, c_sample], 0), ((0, c_pad), (0, 0)))
    yp, ys = x_prompt, x_sample
    outs = [[] for _ in range(8)]
    for l in range(depth):
        prm = _prep_params(w_in[l], lambda_q1[l], lambda_k1[l], lambda_q2[l], lambda_k2[l], subln_w[l],
                           conv_w[l], conv_b[l], dt_bias[l], a_log[l], d_skip[l], ssm_norm_w[l], w_out[l],
                           ln1_g[l], ln1_b[l], router_w[l], router_bias[l], exp_w_gate[l], exp_w_up[l],
                           exp_w_down[l], sh_w_gate[l], sh_w_up[l], sh_w_down[l], ln2_g[l], ln2_b[l])
        mod = _ada(c_all, w_ada[l], b_ada[l]).reshape(c_rows + c_pad, 6, 1, D_MODEL)
        yp, kp, vp, cp, hp = _layer(yp, mod[:bp], pos_prompt, prm, l, depth, None, None, None, None, False)
        ys, kn, vn, cn, hn = _layer(ys, mod[bp:c_rows], pos_sample, prm, l, depth,
                                    cache_k, cache_v,
                                    state_conv[l], state_ssm[l], True)
        for lst, val in zip(outs, (kp, vp, cp, hp, kn, vn, cn, hn)):
            lst.append(val)
    return (yp, ys) + tuple(jnp.stack(o) for o in outs)
```

```python
import functools
import math

import jax
import jax.numpy as jnp
from jax import lax
from jax.experimental import pallas as pl
from jax.experimental.pallas import tpu as pltpu

F32 = jnp.float32
BF16 = jnp.bfloat16

D_MODEL = 1024
CHUNK = 64
CHUNK_SHIFT = 6
HEAD_SHIFT = 6
EPS = 1e-5
LOG2E = math.log2(math.e)
DA_HEADS = 4
DA_HEAD_DIM = 64
DA_WIDTH = DA_HEADS * 2 * DA_HEAD_DIM
ROPE_THETA = 10000.0
SSM_HEADS = 8
SSM_HEAD_DIM = 64
SSM_WIDTH = SSM_HEADS * SSM_HEAD_DIM
SSM_GROUPS = 2
SSM_STATE = 128
CONV_W = 4
CONV_DIM = SSM_WIDTH + 2 * SSM_GROUPS * SSM_STATE
Q_OFF = 0
K_OFF = 512
V_OFF = 1024
Z_OFF = 1536
XBC_OFF = 2048
DT_OFF = 3072
N_EXPERTS = 64
N_EXPERT_GROUPS = 8
EXPERTS_PER_GROUP = 8
TOPK_GROUPS = 4
TOP_K = 8
EXPERT_DIM = 256
ROUTED_SCALE = 2.5

LANES = 128
VMEM_LIMIT = 56 * 1024 * 1024


def _cparams(*sem):
    return pltpu.CompilerParams(dimension_semantics=sem, vmem_limit_bytes=VMEM_LIMIT)


def _dot(a, b):
    return jnp.dot(a, b, preferred_element_type=F32)


def _dot_nt(a, b):
    return lax.dot_general(a, b, (((1,), (1,)), ((), ())), preferred_element_type=F32)


def _split2(a):
    hi = a.astype(BF16)
    lo = (a - hi.astype(F32)).astype(BF16)
    return hi, lo


def _split3(a):
    hi = a.astype(BF16)
    r = a - hi.astype(F32)
    mid = r.astype(BF16)
    lo = (r - mid.astype(F32)).astype(BF16)
    return hi, mid, lo


def _dot3(a, b, dot=_dot):
    ah, al = _split2(a)
    bh, bl = _split2(b)
    return dot(ah, bh) + (dot(ah, bl) + dot(al, bh))


def _dot_exact_lhs(e, a, dot=_dot):
    ah, am, al = _split3(a)
    return dot(e, ah) + (dot(e, am) + dot(e, al))


def _silu(x):
    return x * jax.nn.sigmoid(x)


def _softplus(x):
    return jnp.maximum(x, 0.0) + jnp.log1p(jnp.exp(-jnp.abs(x)))


def _ln(x):
    mu = jnp.mean(x, -1, keepdims=True)
    xc = x - mu
    var = jnp.mean(xc * xc, -1, keepdims=True)
    return xc * lax.rsqrt(var + EPS)


def _ada_kernel(c_ref, w_ref, b_ref, o_ref):
    o_ref[...] = _dot3(_silu(c_ref[...]), w_ref[...]) + b_ref[...]


def _ada(c, w_ada, b_ada):
    rows = c.shape[0]
    n = w_ada.shape[1]
    tn = 1024
    return pl.pallas_call(
        _ada_kernel,
        grid=(n // tn,),
        in_specs=[pl.BlockSpec((rows, D_MODEL), lambda i: (0, 0)),
                  pl.BlockSpec((D_MODEL, tn), lambda i: (0, i)),
                  pl.BlockSpec((1, tn), lambda i: (0, i))],
        out_specs=pl.BlockSpec((rows, tn), lambda i: (0, i)),
        out_shape=jax.ShapeDtypeStruct((rows, n), F32),
        compiler_params=_cparams("arbitrary"),
        name="ada",
    )(c, w_ada, b_ada.reshape(1, n))


def _proj_kernel(x_ref, sc_ref, sh_ref, cos_ref, sin_ref, w_ref, wdt_ref, wvt_ref,
                 q_ref, k_ref, kb_ref, v_ref, vb_ref, z_ref, xbc_ref, dt_ref, *, bb, ts, v_transposed):
    tm = bb * ts
    h = _ln(x_ref[...]) * (1.0 + sc_ref[...]) + sh_ref[...]
    h = h.reshape(tm, D_MODEL)
    hb = h.astype(BF16)

    cos = jnp.tile(cos_ref[...], (1, DA_WIDTH // LANES))
    sin = jnp.tile(sin_ref[...], (1, DA_WIDTH // LANES))
    lane = lax.broadcasted_iota(jnp.int32, (tm, DA_WIDTH), 1)
    first = (lane & (DA_HEAD_DIM - 1)) < (DA_HEAD_DIM // 2)

    def rope(t):
        rot = jnp.where(first, pltpu.roll(t, DA_WIDTH - DA_HEAD_DIM // 2, 1),
                        pltpu.roll(t, DA_HEAD_DIM // 2, 1))
        return t * cos + rot * sin

    q = rope(_dot(hb, w_ref[:, Q_OFF:K_OFF])) * (DA_HEAD_DIM ** -0.5 * LOG2E)
    q_ref[...] = q.reshape(bb, ts, DA_WIDTH).astype(BF16)
    def store_heads(ref, t):
        for hd in range(DA_HEADS):
            ref[:, pl.ds(hd, ts, stride=DA_HEADS), :] = t[:, :, hd * LANES:(hd + 1) * LANES]

    k = rope(_dot(hb, w_ref[:, K_OFF:V_OFF])).reshape(bb, ts, DA_WIDTH)
    store_heads(k_ref, k)
    kb_ref[...] = k.astype(BF16)
    v = _dot(hb, w_ref[:, V_OFF:Z_OFF]).reshape(bb, ts, DA_WIDTH)
    store_heads(v_ref, v)
    if v_transposed:
        vb_ref[0, 0] = _dot_nt(wvt_ref[...], hb).astype(BF16)
    else:
        vb_ref[...] = v.astype(BF16)
    z_ref[...] = _dot(hb, w_ref[:, Z_OFF:XBC_OFF]).reshape(bb, ts, SSM_WIDTH)
    xbc_ref[...] = _dot(hb, w_ref[:, XBC_OFF:DT_OFF]).reshape(bb, ts, CONV_DIM)
    dt_ref[...] = _dot3(h, wdt_ref[...]).reshape(bb, ts, LANES)


def _proj(x, mod4, cos_t, sin_t, w_main, w_dt, w_vt, *, bb, ts, v_transposed):
    B, S, _ = x.shape
    tm = bb * ts
    grid = (S // ts, B // bb)
    row3 = lambda w: pl.BlockSpec((bb, ts, w), lambda i, b: (b, i, 0))
    modspec = lambda which: pl.BlockSpec((bb, None, 1, D_MODEL), lambda i, b: (b, which, 0, 0))
    if v_transposed:
        vb_shape = jax.ShapeDtypeStruct((B, S // ts, DA_WIDTH, ts), BF16)
        vb_spec = pl.BlockSpec((1, 1, DA_WIDTH, ts), lambda i, b: (b, i, 0, 0))
    else:
        vb_shape = jax.ShapeDtypeStruct((B, S, DA_WIDTH), BF16)
        vb_spec = row3(DA_WIDTH)
    heads_shape = jax.ShapeDtypeStruct((B, S * DA_HEADS, LANES), F32)
    heads_spec = pl.BlockSpec((bb, ts * DA_HEADS, LANES), lambda i, b: (b, i, 0))
    out_shapes = [jax.ShapeDtypeStruct((B, S, DA_WIDTH), BF16),
                  heads_shape,
                  jax.ShapeDtypeStruct((B, S, DA_WIDTH), BF16),
                  heads_shape,
                  vb_shape,
                  jax.ShapeDtypeStruct((B, S, SSM_WIDTH), F32),
                  jax.ShapeDtypeStruct((B, S, CONV_DIM), F32),
                  jax.ShapeDtypeStruct((B, S, LANES), F32)]
    out_specs = [row3(DA_WIDTH), heads_spec, row3(DA_WIDTH), heads_spec,
                 vb_spec, row3(SSM_WIDTH), row3(CONV_DIM), row3(LANES)]
    return pl.pallas_call(
        functools.partial(_proj_kernel, bb=bb, ts=ts, v_transposed=v_transposed),
        grid=grid,
        in_specs=[row3(D_MODEL), modspec(1), modspec(0),
                  pl.BlockSpec((tm, LANES), lambda i, b: (i, 0)),
                  pl.BlockSpec((tm, LANES), lambda i, b: (i, 0)),
                  pl.BlockSpec((D_MODEL, DT_OFF), lambda i, b: (0, 0)),
                  pl.BlockSpec((D_MODEL, LANES), lambda i, b: (0, 0)),
                  pl.BlockSpec((DA_WIDTH, D_MODEL), lambda i, b: (0, 0))],
        out_specs=out_specs,
        out_shape=out_shapes,
        compiler_params=_cparams("arbitrary", "arbitrary"),
        name="proj",
    )(x, mod4, mod4, cos_t, sin_t, w_main, w_dt, w_vt)


def _lambda(lq1, lk1, lq2, lk2, lam_init):
    s1 = jnp.sum(lq1[...] * lk1[...], axis=1, keepdims=True)
    s2 = jnp.sum(lq2[...] * lk2[...], axis=1, keepdims=True)
    return jnp.exp(s1) - jnp.exp(s2) + lam_init


def _subln(o, subw, lam_init):
    ms = jnp.mean(o * o, -1, keepdims=True)
    return o * lax.rsqrt(ms + EPS) * subw * (1.0 - lam_init)


def _attn_prompt_kernel(lq1, lk1, lq2, lk2, subw_ref, q_ref, k_ref, vt_ref, o_ref,
                        acc1, acc2, sa1, sa2, sb1, sb2, mxa, mxb, m1, l1, m2, l2, *, tq, tk, lam_init):
    i = pl.program_id(2)
    qt = q_ref[0].astype(F32).T
    row = lax.broadcasted_iota(jnp.int32, (LANES, tq), 0)
    q1t = jnp.where(row < DA_HEAD_DIM, qt, 0.0).astype(BF16)
    q2t = jnp.where(row >= DA_HEAD_DIM, qt, 0.0).astype(BF16)
    acc1[...] = jnp.zeros_like(acc1)
    acc2[...] = jnp.zeros_like(acc2)
    m1[...] = jnp.full_like(m1, -jnp.inf)
    m2[...] = jnp.full_like(m2, -jnp.inf)
    l1[...] = jnp.zeros_like(l1)
    l2[...] = jnp.zeros_like(l2)
    buf_a = (sa1, sa2, mxa)
    buf_b = (sb1, sb2, mxb)

    def produce(j, buf):
        k = k_ref[0, pl.ds(pl.multiple_of(j * tk, tk), tk), :]
        for half, qh in enumerate((q1t, q2t)):
            s = _dot(k, qh)
            buf[half][...] = s
            buf[2][half:half + 1, :] = jnp.max(s, axis=0, keepdims=True)

    def consume(j, buf, masked):
        vt = vt_ref[0, j]
        if masked:
            key = (j * tk + lax.broadcasted_iota(jnp.int32, (tk, tq), 0)) >> CHUNK_SHIFT
            qry = (i * tq + lax.broadcasted_iota(jnp.int32, (tk, tq), 1)) >> CHUNK_SHIFT
            keep = key <= qry
        for half, (m, l, acc) in enumerate(((m1, l1, acc1), (m2, l2, acc2))):
            st = buf[half][...]
            if masked:
                st = jnp.where(keep, st, -jnp.inf)
                tile_max = jnp.max(st, axis=0, keepdims=True)
            else:
                tile_max = buf[2][half:half + 1, :]
            mo = m[...]
            mn = jnp.maximum(mo, tile_max)
            a = jnp.exp2(mo - mn)
            p = jnp.exp2(st - mn)
            l[...] = a * l[...] + jnp.sum(p, axis=0, keepdims=True)
            m[...] = mn
            acc[...] = a * acc[...] + _dot(vt, p.astype(BF16))

    nfull = (i * tq) // tk
    produce(0, buf_a)

    def pair(jj, carry):
        produce(2 * jj + 1, buf_b)
        consume(2 * jj, buf_a, False)
        produce(2 * jj + 2, buf_a)
        consume(2 * jj + 1, buf_b, False)
        return carry

    lax.fori_loop(0, nfull >> 1, pair, 0)
    t0 = (nfull >> 1) * 2

    @pl.when((nfull & 1) == 0)
    def _():
        consume(t0, buf_a, True)

    @pl.when((nfull & 1) == 1)
    def _():
        produce(t0 + 1, buf_b)
        consume(t0, buf_a, False)
        consume(t0 + 1, buf_b, True)

    lam = _lambda(lq1, lk1, lq2, lk2, lam_init)
    ot = acc1[...] / l1[...] - lam * (acc2[...] / l2[...])
    ot = ot * lax.rsqrt(jnp.mean(ot * ot, axis=0, keepdims=True) + EPS)
    o_ref[0] = (ot.T * subw_ref[...] * (1.0 - lam_init)).astype(BF16)


def _attn_prompt(qb, kb, vt, lam_vecs, subw, lam_init, *, tq):
    B, S, _ = qb.shape
    nkt, tk = vt.shape[1], vt.shape[3]
    small = pl.BlockSpec((1, DA_HEAD_DIM), lambda b, h, i: (0, 0))
    return pl.pallas_call(
        functools.partial(_attn_prompt_kernel, tq=tq, tk=tk, lam_init=lam_init),
        grid=(B, DA_HEADS, S // tq),
        in_specs=[small, small, small, small,
                  pl.BlockSpec((1, LANES), lambda b, h, i: (0, 0)),
                  pl.BlockSpec((1, tq, LANES), lambda b, h, i: (b, i, h)),
                  pl.BlockSpec((1, S, LANES), lambda b, h, i: (b, 0, h)),
                  pl.BlockSpec((1, nkt, LANES, tk), lambda b, h, i: (b, 0, h, 0))],
        out_specs=pl.BlockSpec((1, tq, LANES), lambda b, h, i: (b, i, h)),
        out_shape=jax.ShapeDtypeStruct((B, S, DA_WIDTH), BF16),
        scratch_shapes=([pltpu.VMEM((LANES, tq), F32)] * 2 + [pltpu.VMEM((tk, tq), F32)] * 4
                        + [pltpu.VMEM((2, tq), F32)] * 2 + [pltpu.VMEM((1, tq), F32)] * 4),
        compiler_params=_cparams("arbitrary", "arbitrary", "arbitrary"),
        name="attn_prompt",
    )(*lam_vecs, subw, qb, kb, vt)


def _attn_sample_kernel(lq1, lk1, lq2, lk2, subw_ref, q_ref, kn_ref, vn_ref, kp_ref, vp_ref, o_ref,
                        *, lam_init):
    s = q_ref.shape[1]
    lam = _lambda(lq1, lk1, lq2, lk2, lam_init)
    lane = lax.broadcasted_iota(jnp.int32, (s, LANES), 1)
    outs = []
    for h in range(DA_HEADS):
        cols = slice(h * LANES, (h + 1) * LANES)
        q = q_ref[0, :, cols]
        zero = jnp.zeros_like(q)
        kn = kn_ref[0, :, cols]
        vn = vn_ref[0, :, cols]
        past = kp_ref.shape[1] // DA_HEADS
        kp = kp_ref[0, pl.ds(h, past, stride=DA_HEADS), :].astype(BF16)
        vp = vp_ref[0, pl.ds(h, past, stride=DA_HEADS), :].astype(BF16)
        probs = []
        for qm in (jnp.where(lane < DA_HEAD_DIM, q, zero), jnp.where(lane >= DA_HEAD_DIM, q, zero)):
            sp = _dot_nt(qm, kp)
            sn = _dot_nt(qm, kn)
            m = jnp.maximum(jnp.max(sp, -1, keepdims=True), jnp.max(sn, -1, keepdims=True))
            pp = jnp.exp2(sp - m)
            pn = jnp.exp2(sn - m)
            l = jnp.sum(pp, -1, keepdims=True) + jnp.sum(pn, -1, keepdims=True)
            probs.append((pp / l, pn / l))
        ap = probs[0][0] - lam * probs[1][0]
        an = probs[0][1] - lam * probs[1][1]
        o = _dot(ap.astype(BF16), vp) + _dot(an.astype(BF16), vn)
        outs.append(_subln(o, subw_ref[...], lam_init))
    o_ref[0] = jnp.concatenate(outs, axis=-1).astype(BF16)


def _attn_sample(qb, kb, vb, k_past, v_past, lam_vecs, subw, lam_init):
    B, s, _ = qb.shape
    past_rows = k_past.shape[1]
    small = pl.BlockSpec((1, DA_HEAD_DIM), lambda b: (0, 0))
    new = pl.BlockSpec((1, s, DA_WIDTH), lambda b: (b, 0, 0))
    old = pl.BlockSpec((1, past_rows, LANES), lambda b: (b, 0, 0))
    return pl.pallas_call(
        functools.partial(_attn_sample_kernel, lam_init=lam_init),
        grid=(B,),
        in_specs=[small, small, small, small, pl.BlockSpec((1, LANES), lambda b: (0, 0)),
                  new, new, new, old, old],
        out_specs=new,
        out_shape=jax.ShapeDtypeStruct((B, s, DA_WIDTH), BF16),
        compiler_params=_cparams("arbitrary"),
        name="attn_sample",
    )(*lam_vecs, subw, qb, kb, vb, k_past, v_past)


def _ssd_kernel(xbc_ref, z_ref, dt_ref, cp_ref, h0_ref, cw_ref, cb_ref, dtb_ref, alog_ref, dsk_ref, nw_ref,
                y_ref, hout_ref, tail_ref, st_ref, *, t_in, T):
    j = pl.program_id(1)
    PAIR = 2 * SSM_HEAD_DIM

    @pl.when(j == 0)
    def _():
        tail_ref[...] = cp_ref[0]
        st_ref[...] = h0_ref[0]

    def rows(ref, width):
        val = ref[0]
        if T == t_in:
            return val
        return jnp.concatenate([val, jnp.zeros((T - t_in, width), F32)], axis=0)

    xb = rows(xbc_ref, CONV_DIM)
    xp = jnp.concatenate([tail_ref[...], xb], axis=0)
    conv = cb_ref[...] + cw_ref[CONV_W - 1:CONV_W, :] * xb
    for sft in range(1, CONV_W):
        conv = conv + cw_ref[CONV_W - 1 - sft:CONV_W - sft, :] * pltpu.roll(xp, sft, 0)[8:, :]
    if T == t_in:
        tail_ref[...] = xb[T - 8:, :]
    xa = _silu(conv)
    xs = xa[:, :SSM_WIDTH]

    dt = _softplus(rows(dt_ref, LANES) + dtb_ref[...])
    if T != t_in:
        valid = lax.broadcasted_iota(jnp.int32, (T, LANES), 0) < t_in
        dt = jnp.where(valid, dt, 0.0)
    da = dt * (-jnp.exp(alog_ref[...]))

    rr = lax.broadcasted_iota(jnp.int32, (T, T), 0)
    cc = lax.broadcasted_iota(jnp.int32, (T, T), 1)
    causal = cc <= rr
    tril = jnp.where(causal, 1.0, 0.0).astype(BF16)
    acs = _dot_exact_lhs(tril, da)
    e16 = jnp.where(lax.broadcasted_iota(jnp.int32, (16, LANES), 0)
                    == lax.broadcasted_iota(jnp.int32, (16, LANES), 1), 1.0, 0.0).astype(BF16)
    acs_t = _dot_exact_lhs(e16, acs, dot=_dot_nt)

    last = acs[T - 1:T, :]
    expand = jnp.where((lax.broadcasted_iota(jnp.int32, (LANES, SSM_WIDTH), 1) >> HEAD_SHIFT)
                       == lax.broadcasted_iota(jnp.int32, (LANES, SSM_WIDTH), 0), 1.0, 0.0).astype(BF16)

    def per_head_lanes(v):
        hi, lo = _split2(v)
        return _dot(hi, expand) + _dot(lo, expand)

    xdt = xs * per_head_lanes(dt)
    xdd = xdt * per_head_lanes(jnp.exp(last - acs))
    eacs = per_head_lanes(jnp.exp(acs))

    head_of_row = lax.broadcasted_iota(jnp.int32, (SSM_WIDTH, LANES), 0) >> HEAD_SHIFT
    sel = head_of_row == lax.broadcasted_iota(jnp.int32, (SSM_WIDTH, LANES), 1)
    rowdec = jnp.sum(jnp.where(sel, jnp.exp(last), 0.0), axis=1, keepdims=True)

    lane = lax.broadcasted_iota(jnp.int32, (T, PAIR), 1)
    ys = []
    for p in range(SSM_HEADS // 2):
        g = (2 * p) // (SSM_HEADS // SSM_GROUPS)
        bm = xa[:, SSM_WIDTH + g * SSM_STATE:SSM_WIDTH + (g + 1) * SSM_STATE].astype(BF16)
        cm = xa[:, SSM_WIDTH + (SSM_GROUPS + g) * SSM_STATE:
                SSM_WIDTH + (SSM_GROUPS + g + 1) * SSM_STATE].astype(BF16)
        cb = _dot_nt(cm, bm)
        cols = slice(p * PAIR, (p + 1) * PAIR)
        xdt_p = xdt[:, cols].astype(BF16)
        yd = []
        for hh in (2 * p, 2 * p + 1):
            seg = acs[:, hh:hh + 1] - acs_t[hh:hh + 1, :]
            lmat = jnp.exp(jnp.where(causal, seg, -jnp.inf))
            yd.append(_dot((cb * lmat).astype(BF16), xdt_p))
        y_diag = jnp.where(lane < SSM_HEAD_DIM, yd[0], yd[1])
        st = st_ref[cols, :]
        y_off = _dot_nt(cm, st.astype(BF16)) * eacs[:, cols]
        ys.append(y_diag + y_off)
        st_ref[cols, :] = st * rowdec[cols, :] + _dot(xdd[:, cols].T.astype(BF16), bm)
    y = jnp.concatenate(ys, axis=-1) + xs * dsk_ref[...]
    y = y * _silu(rows(z_ref, SSM_WIDTH))
    half = SSM_WIDTH // SSM_GROUPS
    outs = []
    for g in range(SSM_GROUPS):
        yg = y[:, g * half:(g + 1) * half]
        outs.append(yg * lax.rsqrt(jnp.mean(yg * yg, -1, keepdims=True) + EPS))
    y = jnp.concatenate(outs, axis=-1) * nw_ref[...]
    y_ref[0] = y[:t_in, :].astype(BF16)

    @pl.when(j == pl.num_programs(1) - 1)
    def _():
        hout_ref[0] = st_ref[...]


def _ssd(xbc, z, dt_raw, conv_prev8, h0, conv_w, conv_b, dtb, alog, dsk, nw, *, t_in, T):
    B, S, _ = xbc.shape
    row = lambda w: pl.BlockSpec((1, t_in, w), lambda b, j: (b, j, 0))
    per_b = lambda r, w: pl.BlockSpec((1, r, w), lambda b, j: (b, 0, 0))
    const = lambda r, w: pl.BlockSpec((r, w), lambda b, j: (0, 0))
    return pl.pallas_call(
        functools.partial(_ssd_kernel, t_in=t_in, T=T),
        grid=(B, S // t_in),
        in_specs=[row(CONV_DIM), row(SSM_WIDTH), row(LANES),
                  per_b(8, CONV_DIM), per_b(SSM_WIDTH, SSM_STATE),
                  const(CONV_W, CONV_DIM), const(1, CONV_DIM), const(1, LANES), const(1, LANES),
                  const(1, SSM_WIDTH), const(1, SSM_WIDTH)],
        out_specs=[row(SSM_WIDTH), per_b(SSM_WIDTH, SSM_STATE)],
        out_shape=[jax.ShapeDtypeStruct((B, S, SSM_WIDTH), BF16),
                   jax.ShapeDtypeStruct((B, SSM_WIDTH, SSM_STATE), F32)],
        scratch_shapes=[pltpu.VMEM((8, CONV_DIM), F32), pltpu.VMEM((SSM_WIDTH, SSM_STATE), F32)],
        compiler_params=_cparams("arbitrary", "arbitrary"),
        name="ssd",
    )(xbc, z, dt_raw, conv_prev8, h0, conv_w, conv_b, dtb, alog, dsk, nw)


def _router(h, rwt_ref, rb_ref):
    tm = h.shape[0]
    G = N_EXPERT_GROUPS
    logits = _dot3(rwt_ref[...], h, dot=_dot_nt)
    score = jax.nn.sigmoid(logits[:N_EXPERTS, :])
    biased = score + rb_ref[:N_EXPERTS, :]
    sc = [score[G * j:G * (j + 1), :] for j in range(EXPERTS_PER_GROUP)]
    v = [biased[G * j:G * (j + 1), :] for j in range(EXPERTS_PER_GROUP)]

    def tree(op, xs):
        while len(xs) > 1:
            xs = [op(xs[a], xs[a + 1]) for a in range(0, len(xs) - 1, 2)] + ([xs[-1]] if len(xs) % 2 else [])
        return xs[0]

    m1 = tree(jnp.maximum, v)
    first = tree(jnp.minimum, [jnp.where(v[j] == m1, float(j), float(EXPERTS_PER_GROUP))
                               for j in range(EXPERTS_PER_GROUP)])
    m2 = tree(jnp.maximum, [jnp.where(first == float(j), -jnp.inf, v[j]) for j in range(EXPERTS_PER_GROUP)])
    grp = m1 + m2

    gid = lax.broadcasted_iota(jnp.int32, (G, tm), 0)
    cnt = jnp.zeros((G, tm), jnp.int32)
    for g2 in range(G):
        row = grp[g2:g2 + 1, :]
        beats = (row > grp) | ((row == grp) & (gid > g2))
        cnt = cnt + beats.astype(jnp.int32)
    gmask = cnt < TOPK_GROUPS

    idx = [(gid * EXPERTS_PER_GROUP + j).astype(F32) for j in range(EXPERTS_PER_GROUP)]
    mv = [jnp.where(gmask, v[j], -jnp.inf) for j in range(EXPERTS_PER_GROUP)]
    w = [jnp.zeros((G, tm), F32) for _ in range(EXPERTS_PER_GROUP)]
    for _ in range(TOP_K):
        top = jnp.max(tree(jnp.maximum, mv), axis=0, keepdims=True)
        who = jnp.min(tree(jnp.minimum, [jnp.where(mv[j] == top, idx[j], float(N_EXPERTS))
                                         for j in range(EXPERTS_PER_GROUP)]), axis=0, keepdims=True)
        for j in range(EXPERTS_PER_GROUP):
            hit = idx[j] == who
            w[j] = jnp.where(hit, sc[j], w[j])
            mv[j] = jnp.where(hit, -jnp.inf, mv[j])
    tot = w[0]
    for j in range(1, EXPERTS_PER_GROUP):
        tot = tot + w[j]
    tot = jnp.sum(tot, axis=0, keepdims=True)
    gates = [w[j] / tot * ROUTED_SCALE for j in range(EXPERTS_PER_GROUP)]
    return jnp.concatenate(gates + [jnp.zeros((LANES - N_EXPERTS, tm), F32)], axis=0)


def _mix_kernel(x_ref, att_ref, ssm_ref, g1_ref, sc2_ref, sh2_ref, wo_ref, l1g_ref, l1b_ref, rwt_ref, rb_ref,
                x1_ref, h2_ref, gate_ref, *, bb, ts, alpha):
    tm = bb * ts
    att = att_ref[...].reshape(tm, DA_WIDTH)
    ssm = ssm_ref[...].reshape(tm, SSM_WIDTH)
    mix = _dot(att, wo_ref[:DA_WIDTH, :]) + _dot(ssm, wo_ref[DA_WIDTH:, :])
    y = alpha * x_ref[...] + (1.0 + g1_ref[...]) * mix.reshape(bb, ts, D_MODEL)
    x1 = _ln(y) * l1g_ref[...] + l1b_ref[...]
    x1_ref[...] = x1
    h2 = (_ln(x1) * (1.0 + sc2_ref[...]) + sh2_ref[...]).reshape(tm, D_MODEL)
    h2_ref[...] = h2.astype(BF16)
    gate_ref[...] = _router(h2, rwt_ref, rb_ref).T


def _mix(x, att, ssm, mod4, wo, l1g, l1b, rwt, rb, *, bb, ts, alpha):
    B, S, _ = x.shape
    tm = bb * ts
    row3 = lambda w: pl.BlockSpec((bb, ts, w), lambda i, b: (b, i, 0))
    modspec = lambda which: pl.BlockSpec((bb, None, 1, D_MODEL), lambda i, b: (b, which, 0, 0))
    const = lambda r, w: pl.BlockSpec((r, w), lambda i, b: (0, 0))
    flat = lambda w: pl.BlockSpec((tm, w), lambda i, b: (b * (S // ts) + i, 0))
    return pl.pallas_call(
        functools.partial(_mix_kernel, bb=bb, ts=ts, alpha=alpha),
        grid=(S // ts, B // bb),
        in_specs=[row3(D_MODEL), row3(DA_WIDTH), row3(SSM_WIDTH), modspec(2), modspec(4), modspec(3),
                  const(D_MODEL, D_MODEL), const(1, D_MODEL), const(1, D_MODEL),
                  const(LANES, D_MODEL), const(LANES, 1)],
        out_specs=[row3(D_MODEL), flat(D_MODEL), flat(LANES)],
        out_shape=[jax.ShapeDtypeStruct((B, S, D_MODEL), F32),
                   jax.ShapeDtypeStruct((B * S, D_MODEL), BF16),
                   jax.ShapeDtypeStruct((B * S, LANES), F32)],
        compiler_params=_cparams("arbitrary", "arbitrary"),
        name="mix",
    )(x, att, ssm, mod4, mod4, mod4, wo, l1g, l1b, rwt, rb)


EXPERTS_PER_STEP = 4


def _moe_kernel(h_ref, gate_ref, x1_ref, g2_ref, wg_ref, wu_ref, wd_ref, sg_ref, su_ref, sd_ref,
                l2g_ref, l2b_ref, o_ref, acc_ref, *, bb, ts, alpha):
    tm = bb * ts
    e = pl.program_id(2)
    h = h_ref[...]

    @pl.when(e == 0)
    def _():
        acc_ref[...] = jnp.zeros_like(acc_ref)

    gate = gate_ref[...]
    lane = lax.broadcasted_iota(jnp.int32, (tm, LANES), 1)
    hids = []
    for s in range(EXPERTS_PER_STEP):
        col = e * EXPERTS_PER_STEP + s
        gcol = jnp.sum(jnp.where(lane == col, gate, 0.0), axis=1, keepdims=True)
        hid = _silu(_dot(h, wg_ref[s])) * _dot(h, wu_ref[s]) * gcol
        hids.append(hid.astype(BF16))
    hid = jnp.concatenate(hids, axis=-1)
    acc_ref[...] += _dot(hid, wd_ref[...].reshape(EXPERTS_PER_STEP * EXPERT_DIM, D_MODEL))

    @pl.when(e == pl.num_programs(2) - 1)
    def _():
        shared = _dot((_silu(_dot(h, sg_ref[...])) * _dot(h, su_ref[...])).astype(BF16), sd_ref[...])
        ffn = (acc_ref[...] + shared).reshape(bb, ts, D_MODEL)
        y = alpha * x1_ref[...] + (1.0 + g2_ref[...]) * ffn
        o_ref[...] = _ln(y) * l2g_ref[...] + l2b_ref[...]


def _expert_of_column(c):
    return (c % N_EXPERT_GROUPS) * EXPERTS_PER_GROUP + c // N_EXPERT_GROUPS


def _moe(h2, gate, x1, mod4, wg, wu, wd, sg, su, sd, l2g, l2b, *, bb, ts, alpha):
    B, S, _ = x1.shape
    tm = bb * ts
    row3 = lambda w: pl.BlockSpec((bb, ts, w), lambda i, b, e: (b, i, 0))
    flat = lambda w: pl.BlockSpec((tm, w), lambda i, b, e: (b * (S // ts) + i, 0))
    const = lambda r, w: pl.BlockSpec((r, w), lambda i, b, e: (0, 0))
    ew_in = pl.BlockSpec((EXPERTS_PER_STEP, None, D_MODEL, EXPERT_DIM),
                         lambda i, b, e: (e % (N_EXPERT_GROUPS // EXPERTS_PER_STEP), e // (N_EXPERT_GROUPS // EXPERTS_PER_STEP), 0, 0))
    ew_out = pl.BlockSpec((EXPERTS_PER_STEP, None, EXPERT_DIM, D_MODEL),
                          lambda i, b, e: (e % (N_EXPERT_GROUPS // EXPERTS_PER_STEP), e // (N_EXPERT_GROUPS // EXPERTS_PER_STEP), 0, 0))
    return pl.pallas_call(
        functools.partial(_moe_kernel, bb=bb, ts=ts, alpha=alpha),
        grid=(S // ts, B // bb, N_EXPERTS // EXPERTS_PER_STEP),
        in_specs=[flat(D_MODEL), flat(LANES), row3(D_MODEL),
                  pl.BlockSpec((bb, None, 1, D_MODEL), lambda i, b, e: (b, 5, 0, 0)),
                  ew_in, ew_in, ew_out,
                  const(D_MODEL, EXPERT_DIM), const(D_MODEL, EXPERT_DIM), const(EXPERT_DIM, D_MODEL),
                  const(1, D_MODEL), const(1, D_MODEL)],
        out_specs=row3(D_MODEL),
        out_shape=jax.ShapeDtypeStruct((B, S, D_MODEL), F32),
        scratch_shapes=[pltpu.VMEM((tm, D_MODEL), F32)],
        compiler_params=_cparams("arbitrary", "arbitrary", "arbitrary"),
        name="moe",
    )(h2, gate, x1, mod4, wg, wu, wd, sg, su, sd, l2g, l2b)


def _rope_tables(pos):
    half = DA_HEAD_DIM // 2
    inv = ROPE_THETA ** (-jnp.arange(half, dtype=F32) / half)
    ang = pos.astype(F32)[:, None] * inv[None, :]
    cos = jnp.cos(ang)
    sin = jnp.sin(ang)
    reps = LANES // DA_HEAD_DIM
    cos_t = jnp.tile(jnp.concatenate([cos, cos], -1), (1, reps))
    sin_t = jnp.tile(jnp.concatenate([-sin, sin], -1), (1, reps))
    return cos_t, sin_t


def _layer(x, mod4, pos, prm, layer_idx, depth, k_past, v_past, conv_prev, ssm_prev, sample):
    B, S, _ = x.shape
    alpha = (2 * depth) ** 0.25
    lam_init = 0.8 - 0.6 * math.exp(-0.3 * layer_idx)
    cos_t, sin_t = _rope_tables(pos)
    if sample:
        bb, ts = B, S
        cos_t = jnp.tile(cos_t, (B, 1))
        sin_t = jnp.tile(sin_t, (B, 1))
    else:
        bb, ts = 1, min(512, S)
    qb, k, kb, v, vb, z, xbc, dt_raw = _proj(x, mod4, cos_t, sin_t, prm['w_main'], prm['w_dt'], prm['w_vt'],
                                             bb=bb, ts=ts, v_transposed=not sample)
    if sample:
        att = _attn_sample(qb, kb, vb, k_past, v_past, prm['lam_vecs'], prm['subln_w'], lam_init)
        conv_prev8 = jnp.pad(conv_prev, ((0, 0), (8 - (CONV_W - 1), 0), (0, 0)))
        h0 = ssm_prev.reshape(B, SSM_WIDTH, SSM_STATE)
        t_in, T = S, max(S, LANES)
        new_conv = jnp.concatenate([conv_prev, xbc], axis=1)[:, -(CONV_W - 1):]
    else:
        att = _attn_prompt(qb, kb, vb, prm['lam_vecs'], prm['subln_w'], lam_init, tq=min(512, S))
        conv_prev8 = jnp.zeros((B, 8, CONV_DIM), F32)
        h0 = jnp.zeros((B, SSM_WIDTH, SSM_STATE), F32)
        t_in = T = min(256, S)
        new_conv = xbc[:, -(CONV_W - 1):]
    ssm, h_last = _ssd(xbc, z, dt_raw, conv_prev8, h0, prm['conv_w'], prm['conv_b'], prm['dt_bias'],
                       prm['a_log'], prm['d_skip'], prm['ssm_norm_w'], t_in=t_in, T=T)
    x1, h2, gate = _mix(x, att, ssm, mod4, prm['w_out'], prm['ln1_g'], prm['ln1_b'], prm['router_wt'],
                        prm['router_b'], bb=bb, ts=ts, alpha=alpha)
    if not sample:
        ts = min(1024, S)
    y = _moe(h2, gate, x1, mod4, prm['wg'], prm['wu'], prm['wd'], prm['sg'], prm['su'], prm['sd'],
             prm['ln2_g'], prm['ln2_b'], bb=bb, ts=ts, alpha=alpha)
    return (y, k.reshape(B, S, DA_HEADS, 2 * DA_HEAD_DIM), v.reshape(B, S, DA_HEADS, 2 * DA_HEAD_DIM),
            new_conv, h_last.reshape(B, SSM_HEADS, SSM_HEAD_DIM, SSM_STATE))


def _prep_params(w_in, lq1, lk1, lq2, lk2, subln_w, conv_w, conv_b, dt_bias, a_log, d_skip, ssm_norm_w, w_out,
                 ln1_g, ln1_b, router_w, router_bias, ewg, ewu, ewd, swg, swu, swd, ln2_g, ln2_b):
    pad8 = lambda a: jnp.pad(a.reshape(1, -1), ((0, 0), (0, LANES - a.shape[-1])))
    row = lambda a: a.reshape(1, -1)
    perm = (jnp.arange(N_EXPERTS) % N_EXPERT_GROUPS) * EXPERTS_PER_GROUP + jnp.arange(N_EXPERTS) // N_EXPERT_GROUPS
    rwt = jnp.pad(router_w.T[perm], ((0, LANES - N_EXPERTS), (0, 0)))
    rb = jnp.pad(router_bias[perm], (0, LANES - N_EXPERTS)).reshape(LANES, 1)
    grouped = lambda w: w.astype(BF16).reshape((N_EXPERT_GROUPS, EXPERTS_PER_GROUP) + w.shape[1:])
    return {
        'w_main': w_in[:, :DT_OFF].astype(BF16),
        'w_dt': jnp.pad(w_in[:, DT_OFF:], ((0, 0), (0, LANES - SSM_HEADS))),
        'w_vt': w_in[:, V_OFF:Z_OFF].T.astype(BF16),
        'lam_vecs': (row(lq1), row(lk1), row(lq2), row(lk2)),
        'subln_w': row(subln_w),
        'conv_w': conv_w, 'conv_b': row(conv_b),
        'dt_bias': pad8(dt_bias), 'a_log': pad8(a_log),
        'd_skip': row(jnp.repeat(d_skip, SSM_HEAD_DIM)), 'ssm_norm_w': row(ssm_norm_w),
        'w_out': w_out.astype(BF16), 'ln1_g': row(ln1_g), 'ln1_b': row(ln1_b),
        'router_wt': rwt, 'router_b': rb,
        'wg': grouped(ewg), 'wu': grouped(ewu), 'wd': grouped(ewd),
        'sg': swg.astype(BF16), 'su': swu.astype(BF16), 'sd': swd.astype(BF16),
        'ln2_g': row(ln2_g), 'ln2_b': row(ln2_b),
    }


def kernel(x_prompt, x_sample, c_prompt, c_sample, cache_k, cache_v, state_conv, state_ssm, w_ada, b_ada, w_in, lambda_q1, lambda_k1, lambda_q2, lambda_k2, subln_w, conv_w, conv_b, dt_bias, a_log, d_skip, ssm_norm_w, w_out, ln1_g, ln1_b, router_w, router_bias, exp_w_gate, exp_w_up, exp_w_down, sh_w_gate, sh_w_up, sh_w_down, ln2_g, ln2_b):
    depth = w_ada.shape[0]
    bp, sp, _ = x_prompt.shape
    bs, ss, _ = x_sample.shape
    past = cache_k.shape[2]
    pos_prompt = jnp.arange(sp)
    pos_sample = past + jnp.arange(ss)
    c_rows = bp + bs
    c_pad = (-c_rows) % 8
    c_all = jnp.pad(jnp.concatenate([c_prompt, c_sample], 0), ((0, c_pad), (0, 0)))
    yp, ys = x_prompt, x_sample
    outs = [[] for _ in range(8)]
    for l in range(depth):
        prm = _prep_params(w_in[l], lambda_q1[l], lambda_k1[l], lambda_q2[l], lambda_k2[l], subln_w[l],
                           conv_w[l], conv_b[l], dt_bias[l], a_log[l], d_skip[l], ssm_norm_w[l], w_out[l],
                           ln1_g[l], ln1_b[l], router_w[l], router_bias[l], exp_w_gate[l], exp_w_up[l],
                           exp_w_down[l], sh_w_gate[l], sh_w_up[l], sh_w_down[l], ln2_g[l], ln2_b[l])
        mod = _ada(c_all, w_ada[l], b_ada[l]).reshape(c_rows + c_pad, 6, 1, D_MODEL)
        yp, kp, vp, cp, hp = _layer(yp, mod[:bp], pos_prompt, prm, l, depth, None, None, None, None, False)
        ys, kn, vn, cn, hn = _layer(ys, mod[bp:c_rows], pos_sample, prm, l, depth,
                                    cache_k[l].reshape(bs, past * DA_HEADS, LANES),
                                    cache_v[l].reshape(bs, past * DA_HEADS, LANES),
                                    state_conv[l], state_ssm[l], True)
        for lst, val in zip(outs, (kp, vp, cp, hp, kn, vn, cn, hn)):
            lst.append(val)
    return (yp, ys) + tuple(jnp.stack(o) for o in outs)
```

```python
import functools
import math

import jax
import jax.numpy as jnp
from jax import lax
from jax.experimental import pallas as pl
from jax.experimental.pallas import tpu as pltpu

F32 = jnp.float32
BF16 = jnp.bfloat16

D_MODEL = 1024
CHUNK = 64
CHUNK_SHIFT = 6
HEAD_SHIFT = 6
EPS = 1e-5
LOG2E = math.log2(math.e)
DA_HEADS = 4
DA_HEAD_DIM = 64
DA_WIDTH = DA_HEADS * 2 * DA_HEAD_DIM
ROPE_THETA = 10000.0
SSM_HEADS = 8
SSM_HEAD_DIM = 64
SSM_WIDTH = SSM_HEADS * SSM_HEAD_DIM
SSM_GROUPS = 2
SSM_STATE = 128
CONV_W = 4
CONV_DIM = SSM_WIDTH + 2 * SSM_GROUPS * SSM_STATE
Q_OFF = 0
K_OFF = 512
V_OFF = 1024
Z_OFF = 1536
XBC_OFF = 2048
DT_OFF = 3072
N_EXPERTS = 64
N_EXPERT_GROUPS = 8
EXPERTS_PER_GROUP = 8
TOPK_GROUPS = 4
TOP_K = 8
EXPERT_DIM = 256
ROUTED_SCALE = 2.5

LANES = 128
VMEM_LIMIT = 56 * 1024 * 1024


def _cparams(*sem):
    return pltpu.CompilerParams(dimension_semantics=sem, vmem_limit_bytes=VMEM_LIMIT)


def _dot(a, b):
    return jnp.dot(a, b, preferred_element_type=F32)


def _dot_nt(a, b):
    return lax.dot_general(a, b, (((1,), (1,)), ((), ())), preferred_element_type=F32)


def _split2(a):
    hi = a.astype(BF16)
    lo = (a - hi.astype(F32)).astype(BF16)
    return hi, lo


def _split3(a):
    hi = a.astype(BF16)
    r = a - hi.astype(F32)
    mid = r.astype(BF16)
    lo = (r - mid.astype(F32)).astype(BF16)
    return hi, mid, lo


def _dot3(a, b, dot=_dot):
    ah, al = _split2(a)
    bh, bl = _split2(b)
    return dot(ah, bh) + (dot(ah, bl) + dot(al, bh))


def _dot_exact_lhs(e, a, dot=_dot):
    ah, am, al = _split3(a)
    return dot(e, ah) + (dot(e, am) + dot(e, al))


def _silu(x):
    return x * jax.nn.sigmoid(x)


def _softplus(x):
    return jnp.maximum(x, 0.0) + jnp.log1p(jnp.exp(-jnp.abs(x)))


def _ln(x):
    mu = jnp.mean(x, -1, keepdims=True)
    xc = x - mu
    var = jnp.mean(xc * xc, -1, keepdims=True)
    return xc * lax.rsqrt(var + EPS)


def _ada_kernel(c_ref, w_ref, b_ref, o_ref):
    o_ref[...] = _dot3(_silu(c_ref[...]), w_ref[...]) + b_ref[...]


def _ada(c, w_ada, b_ada):
    rows = c.shape[0]
    n = w_ada.shape[1]
    tn = 1024
    return pl.pallas_call(
        _ada_kernel,
        grid=(n // tn,),
        in_specs=[pl.BlockSpec((rows, D_MODEL), lambda i: (0, 0)),
                  pl.BlockSpec((D_MODEL, tn), lambda i: (0, i)),
                  pl.BlockSpec((1, tn), lambda i: (0, i))],
        out_specs=pl.BlockSpec((rows, tn), lambda i: (0, i)),
        out_shape=jax.ShapeDtypeStruct((rows, n), F32),
        compiler_params=_cparams("arbitrary"),
        name="ada",
    )(c, w_ada, b_ada.reshape(1, n))


def _proj_kernel(x_ref, sc_ref, sh_ref, cos_ref, sin_ref, w_ref, wdt_ref,
                 q_ref, k_ref, kb_ref, v_ref, vb_ref, z_ref, xbc_ref, dt_ref, *, bb, ts, v_transposed):
    tm = bb * ts
    h = _ln(x_ref[...]) * (1.0 + sc_ref[...]) + sh_ref[...]
    h = h.reshape(tm, D_MODEL)
    hb = h.astype(BF16)

    cos = jnp.tile(cos_ref[...], (1, DA_WIDTH // LANES))
    sin = jnp.tile(sin_ref[...], (1, DA_WIDTH // LANES))
    lane = lax.broadcasted_iota(jnp.int32, (tm, DA_WIDTH), 1)
    first = (lane & (DA_HEAD_DIM - 1)) < (DA_HEAD_DIM // 2)

    def rope(t):
        rot = jnp.where(first, pltpu.roll(t, DA_WIDTH - DA_HEAD_DIM // 2, 1),
                        pltpu.roll(t, DA_HEAD_DIM // 2, 1))
        return t * cos + rot * sin

    q = rope(_dot(hb, w_ref[:, Q_OFF:K_OFF])) * (DA_HEAD_DIM ** -0.5 * LOG2E)
    q_ref[...] = q.reshape(bb, ts, DA_WIDTH).astype(BF16)
    def store_heads(ref, t):
        for hd in range(DA_HEADS):
            ref[:, pl.ds(hd, ts, stride=DA_HEADS), :] = t[:, :, hd * LANES:(hd + 1) * LANES]

    k = rope(_dot(hb, w_ref[:, K_OFF:V_OFF])).reshape(bb, ts, DA_WIDTH)
    store_heads(k_ref, k)
    kb_ref[...] = k.astype(BF16)
    v = _dot(hb, w_ref[:, V_OFF:Z_OFF]).reshape(bb, ts, DA_WIDTH)
    store_heads(v_ref, v)
    if v_transposed:
        vb_ref[0, 0] = v.reshape(tm, DA_WIDTH).T.astype(BF16)
    else:
        vb_ref[...] = v.astype(BF16)
    z_ref[...] = _dot(hb, w_ref[:, Z_OFF:XBC_OFF]).reshape(bb, ts, SSM_WIDTH)
    xbc_ref[...] = _dot(hb, w_ref[:, XBC_OFF:DT_OFF]).reshape(bb, ts, CONV_DIM)
    dt_ref[...] = _dot3(h, wdt_ref[...]).reshape(bb, ts, LANES)


def _proj(x, mod4, cos_t, sin_t, w_main, w_dt, *, bb, ts, v_transposed):
    B, S, _ = x.shape
    tm = bb * ts
    grid = (S // ts, B // bb)
    row3 = lambda w: pl.BlockSpec((bb, ts, w), lambda i, b: (b, i, 0))
    modspec = lambda which: pl.BlockSpec((bb, None, 1, D_MODEL), lambda i, b: (b, which, 0, 0))
    if v_transposed:
        vb_shape = jax.ShapeDtypeStruct((B, S // ts, DA_WIDTH, ts), BF16)
        vb_spec = pl.BlockSpec((1, 1, DA_WIDTH, ts), lambda i, b: (b, i, 0, 0))
    else:
        vb_shape = jax.ShapeDtypeStruct((B, S, DA_WIDTH), BF16)
        vb_spec = row3(DA_WIDTH)
    heads_shape = jax.ShapeDtypeStruct((B, S * DA_HEADS, LANES), F32)
    heads_spec = pl.BlockSpec((bb, ts * DA_HEADS, LANES), lambda i, b: (b, i, 0))
    out_shapes = [jax.ShapeDtypeStruct((B, S, DA_WIDTH), BF16),
                  heads_shape,
                  jax.ShapeDtypeStruct((B, S, DA_WIDTH), BF16),
                  heads_shape,
                  vb_shape,
                  jax.ShapeDtypeStruct((B, S, SSM_WIDTH), F32),
                  jax.ShapeDtypeStruct((B, S, CONV_DIM), F32),
                  jax.ShapeDtypeStruct((B, S, LANES), F32)]
    out_specs = [row3(DA_WIDTH), heads_spec, row3(DA_WIDTH), heads_spec,
                 vb_spec, row3(SSM_WIDTH), row3(CONV_DIM), row3(LANES)]
    return pl.pallas_call(
        functools.partial(_proj_kernel, bb=bb, ts=ts, v_transposed=v_transposed),
        grid=grid,
        in_specs=[row3(D_MODEL), modspec(1), modspec(0),
                  pl.BlockSpec((tm, LANES), lambda i, b: (i, 0)),
                  pl.BlockSpec((tm, LANES), lambda i, b: (i, 0)),
                  pl.BlockSpec((D_MODEL, DT_OFF), lambda i, b: (0, 0)),
                  pl.BlockSpec((D_MODEL, LANES), lambda i, b: (0, 0))],
        out_specs=out_specs,
        out_shape=out_shapes,
        compiler_params=_cparams("arbitrary", "arbitrary"),
        name="proj",
    )(x, mod4, mod4, cos_t, sin_t, w_main, w_dt)


def _lambda(lq1, lk1, lq2, lk2, lam_init):
    s1 = jnp.sum(lq1[...] * lk1[...], axis=1, keepdims=True)
    s2 = jnp.sum(lq2[...] * lk2[...], axis=1, keepdims=True)
    return jnp.exp(s1) - jnp.exp(s2) + lam_init


def _subln(o, subw, lam_init):
    ms = jnp.mean(o * o, -1, keepdims=True)
    return o * lax.rsqrt(ms + EPS) * subw * (1.0 - lam_init)


def _attn_prompt_kernel(lq1, lk1, lq2, lk2, subw_ref, q_ref, k_ref, vt_ref, o_ref,
                        acc1, acc2, sa1, sa2, sb1, sb2, mxa, mxb, m1, l1, m2, l2, *, tq, tk, lam_init):
    i = pl.program_id(2)
    qt = q_ref[0].astype(F32).T
    row = lax.broadcasted_iota(jnp.int32, (LANES, tq), 0)
    q1t = jnp.where(row < DA_HEAD_DIM, qt, 0.0).astype(BF16)
    q2t = jnp.where(row >= DA_HEAD_DIM, qt, 0.0).astype(BF16)
    acc1[...] = jnp.zeros_like(acc1)
    acc2[...] = jnp.zeros_like(acc2)
    m1[...] = jnp.full_like(m1, -jnp.inf)
    m2[...] = jnp.full_like(m2, -jnp.inf)
    l1[...] = jnp.zeros_like(l1)
    l2[...] = jnp.zeros_like(l2)
    buf_a = (sa1, sa2, mxa)
    buf_b = (sb1, sb2, mxb)

    def produce(j, buf):
        k = k_ref[0, pl.ds(pl.multiple_of(j * tk, tk), tk), :]
        for half, qh in enumerate((q1t, q2t)):
            s = _dot(k, qh)
            buf[half][...] = s
            buf[2][half:half + 1, :] = jnp.max(s, axis=0, keepdims=True)

    def consume(j, buf, masked):
        vt = vt_ref[0, j]
        if masked:
            key = (j * tk + lax.broadcasted_iota(jnp.int32, (tk, tq), 0)) >> CHUNK_SHIFT
            qry = (i * tq + lax.broadcasted_iota(jnp.int32, (tk, tq), 1)) >> CHUNK_SHIFT
            keep = key <= qry
        for half, (m, l, acc) in enumerate(((m1, l1, acc1), (m2, l2, acc2))):
            st = buf[half][...]
            if masked:
                st = jnp.where(keep, st, -jnp.inf)
                tile_max = jnp.max(st, axis=0, keepdims=True)
            else:
                tile_max = buf[2][half:half + 1, :]
            mo = m[...]
            mn = jnp.maximum(mo, tile_max)
            a = jnp.exp2(mo - mn)
            p = jnp.exp2(st - mn)
            l[...] = a * l[...] + jnp.sum(p, axis=0, keepdims=True)
            m[...] = mn
            acc[...] = a * acc[...] + _dot(vt, p.astype(BF16))

    nfull = (i * tq) // tk
    produce(nfull, buf_a)
    produce(0, buf_b)
    consume(nfull, buf_a, True)

    def pair(jj, carry):
        produce(2 * jj + 1, buf_a)
        consume(2 * jj, buf_b, False)
        produce(2 * jj + 2, buf_b)
        consume(2 * jj + 1, buf_a, False)
        return carry

    lax.fori_loop(0, nfull >> 1, pair, 0)

    @pl.when((nfull & 1) == 1)
    def _():
        consume(nfull - 1, buf_b, False)

    lam = _lambda(lq1, lk1, lq2, lk2, lam_init)
    ot = acc1[...] / l1[...] - lam * (acc2[...] / l2[...])
    ot = ot * lax.rsqrt(jnp.mean(ot * ot, axis=0, keepdims=True) + EPS)
    o_ref[0] = (ot.T * subw_ref[...] * (1.0 - lam_init)).astype(BF16)


def _attn_prompt(qb, kb, vt, lam_vecs, subw, lam_init, *, tq):
    B, S, _ = qb.shape
    nkt, tk = vt.shape[1], vt.shape[3]
    small = pl.BlockSpec((1, DA_HEAD_DIM), lambda b, h, i: (0, 0))
    return pl.pallas_call(
        functools.partial(_attn_prompt_kernel, tq=tq, tk=tk, lam_init=lam_init),
        grid=(B, DA_HEADS, S // tq),
        in_specs=[small, small, small, small,
                  pl.BlockSpec((1, LANES), lambda b, h, i: (0, 0)),
                  pl.BlockSpec((1, tq, LANES), lambda b, h, i: (b, i, h)),
                  pl.BlockSpec((1, S, LANES), lambda b, h, i: (b, 0, h)),
                  pl.BlockSpec((1, nkt, LANES, tk), lambda b, h, i: (b, 0, h, 0))],
        out_specs=pl.BlockSpec((1, tq, LANES), lambda b, h, i: (b, i, h)),
        out_shape=jax.ShapeDtypeStruct((B, S, DA_WIDTH), BF16),
        scratch_shapes=([pltpu.VMEM((LANES, tq), F32)] * 2 + [pltpu.VMEM((tk, tq), F32)] * 4
                        + [pltpu.VMEM((2, tq), F32)] * 2 + [pltpu.VMEM((1, tq), F32)] * 4),
        compiler_params=_cparams("arbitrary", "arbitrary", "arbitrary"),
        name="attn_prompt",
    )(*lam_vecs, subw, qb, kb, vt)


def _attn_sample_kernel(lq1, lk1, lq2, lk2, subw_ref, q_ref, kn_ref, vn_ref, kp_ref, vp_ref, o_ref,
                        *, lam_init):
    s = q_ref.shape[1]
    lam = _lambda(lq1, lk1, lq2, lk2, lam_init)
    lane = lax.broadcasted_iota(jnp.int32, (s, LANES), 1)
    outs = []
    for h in range(DA_HEADS):
        cols = slice(h * LANES, (h + 1) * LANES)
        q = q_ref[0, :, cols]
        zero = jnp.zeros_like(q)
        kn = kn_ref[0, :, cols]
        vn = vn_ref[0, :, cols]
        past = kp_ref.shape[1] // DA_HEADS
        kp = kp_ref[0, pl.ds(h, past, stride=DA_HEADS), :].astype(BF16)
        vp = vp_ref[0, pl.ds(h, past, stride=DA_HEADS), :].astype(BF16)
        probs = []
        for qm in (jnp.where(lane < DA_HEAD_DIM, q, zero), jnp.where(lane >= DA_HEAD_DIM, q, zero)):
            sp = _dot_nt(qm, kp)
            sn = _dot_nt(qm, kn)
            m = jnp.maximum(jnp.max(sp, -1, keepdims=True), jnp.max(sn, -1, keepdims=True))
            pp = jnp.exp2(sp - m)
            pn = jnp.exp2(sn - m)
            l = jnp.sum(pp, -1, keepdims=True) + jnp.sum(pn, -1, keepdims=True)
            probs.append((pp / l, pn / l))
        ap = probs[0][0] - lam * probs[1][0]
        an = probs[0][1] - lam * probs[1][1]
        o = _dot(ap.astype(BF16), vp) + _dot(an.astype(BF16), vn)
        outs.append(_subln(o, subw_ref[...], lam_init))
    o_ref[0] = jnp.concatenate(outs, axis=-1).astype(BF16)


def _attn_sample(qb, kb, vb, k_past, v_past, lam_vecs, subw, lam_init):
    B, s, _ = qb.shape
    past_rows = k_past.shape[1]
    small = pl.BlockSpec((1, DA_HEAD_DIM), lambda b: (0, 0))
    new = pl.BlockSpec((1, s, DA_WIDTH), lambda b: (b, 0, 0))
    old = pl.BlockSpec((1, past_rows, LANES), lambda b: (b, 0, 0))
    return pl.pallas_call(
        functools.partial(_attn_sample_kernel, lam_init=lam_init),
        grid=(B,),
        in_specs=[small, small, small, small, pl.BlockSpec((1, LANES), lambda b: (0, 0)),
                  new, new, new, old, old],
        out_specs=new,
        out_shape=jax.ShapeDtypeStruct((B, s, DA_WIDTH), BF16),
        compiler_params=_cparams("arbitrary"),
        name="attn_sample",
    )(*lam_vecs, subw, qb, kb, vb, k_past, v_past)


def _ssd_kernel(xbc_ref, z_ref, dt_ref, cp_ref, h0_ref, cw_ref, cb_ref, dtb_ref, alog_ref, dsk_ref, nw_ref,
                y_ref, hout_ref, tail_ref, st_ref, *, t_in, T):
    j = pl.program_id(1)
    PAIR = 2 * SSM_HEAD_DIM

    @pl.when(j == 0)
    def _():
        tail_ref[...] = cp_ref[0]
        st_ref[...] = h0_ref[0]

    def rows(ref, width):
        val = ref[0]
        if T == t_in:
            return val
        return jnp.concatenate([val, jnp.zeros((T - t_in, width), F32)], axis=0)

    xb = rows(xbc_ref, CONV_DIM)
    xp = jnp.concatenate([tail_ref[...], xb], axis=0)
    conv = cb_ref[...] + cw_ref[CONV_W - 1:CONV_W, :] * xb
    for sft in range(1, CONV_W):
        conv = conv + cw_ref[CONV_W - 1 - sft:CONV_W - sft, :] * pltpu.roll(xp, sft, 0)[8:, :]
    if T == t_in:
        tail_ref[...] = xb[T - 8:, :]
    xa = _silu(conv)
    xs = xa[:, :SSM_WIDTH]

    dt = _softplus(rows(dt_ref, LANES) + dtb_ref[...])
    if T != t_in:
        valid = lax.broadcasted_iota(jnp.int32, (T, LANES), 0) < t_in
        dt = jnp.where(valid, dt, 0.0)
    da = dt * (-jnp.exp(alog_ref[...]))

    rr = lax.broadcasted_iota(jnp.int32, (T, T), 0)
    cc = lax.broadcasted_iota(jnp.int32, (T, T), 1)
    causal = cc <= rr
    tril = jnp.where(causal, 1.0, 0.0).astype(BF16)
    acs = _dot_exact_lhs(tril, da)
    e16 = jnp.where(lax.broadcasted_iota(jnp.int32, (16, LANES), 0)
                    == lax.broadcasted_iota(jnp.int32, (16, LANES), 1), 1.0, 0.0).astype(BF16)
    acs_t = _dot_exact_lhs(e16, acs, dot=_dot_nt)

    last = acs[T - 1:T, :]
    expand = jnp.where((lax.broadcasted_iota(jnp.int32, (LANES, SSM_WIDTH), 1) >> HEAD_SHIFT)
                       == lax.broadcasted_iota(jnp.int32, (LANES, SSM_WIDTH), 0), 1.0, 0.0).astype(BF16)

    def per_head_lanes(v):
        hi, lo = _split2(v)
        return _dot(hi, expand) + _dot(lo, expand)

    xdt = xs * per_head_lanes(dt)
    xdd = xdt * per_head_lanes(jnp.exp(last - acs))
    eacs = per_head_lanes(jnp.exp(acs))

    head_of_row = lax.broadcasted_iota(jnp.int32, (SSM_WIDTH, LANES), 0) >> HEAD_SHIFT
    sel = head_of_row == lax.broadcasted_iota(jnp.int32, (SSM_WIDTH, LANES), 1)
    rowdec = jnp.sum(jnp.where(sel, jnp.exp(last), 0.0), axis=1, keepdims=True)

    lane = lax.broadcasted_iota(jnp.int32, (T, PAIR), 1)
    ys = []
    for p in range(SSM_HEADS // 2):
        g = (2 * p) // (SSM_HEADS // SSM_GROUPS)
        bm = xa[:, SSM_WIDTH + g * SSM_STATE:SSM_WIDTH + (g + 1) * SSM_STATE].astype(BF16)
        cm = xa[:, SSM_WIDTH + (SSM_GROUPS + g) * SSM_STATE:
                SSM_WIDTH + (SSM_GROUPS + g + 1) * SSM_STATE].astype(BF16)
        cb = _dot_nt(cm, bm)
        cols = slice(p * PAIR, (p + 1) * PAIR)
        xdt_p = xdt[:, cols].astype(BF16)
        yd = []
        for hh in (2 * p, 2 * p + 1):
            seg = acs[:, hh:hh + 1] - acs_t[hh:hh + 1, :]
            lmat = jnp.exp(jnp.where(causal, seg, -jnp.inf))
            yd.append(_dot((cb * lmat).astype(BF16), xdt_p))
        y_diag = jnp.where(lane < SSM_HEAD_DIM, yd[0], yd[1])
        st = st_ref[cols, :]
        y_off = _dot_nt(cm, st.astype(BF16)) * eacs[:, cols]
        ys.append(y_diag + y_off)
        st_ref[cols, :] = st * rowdec[cols, :] + _dot(xdd[:, cols].T.astype(BF16), bm)
    y = jnp.concatenate(ys, axis=-1) + xs * dsk_ref[...]
    y = y * _silu(rows(z_ref, SSM_WIDTH))
    half = SSM_WIDTH // SSM_GROUPS
    outs = []
    for g in range(SSM_GROUPS):
        yg = y[:, g * half:(g + 1) * half]
        outs.append(yg * lax.rsqrt(jnp.mean(yg * yg, -1, keepdims=True) + EPS))
    y = jnp.concatenate(outs, axis=-1) * nw_ref[...]
    y_ref[0] = y[:t_in, :].astype(BF16)

    @pl.when(j == pl.num_programs(1) - 1)
    def _():
        hout_ref[0] = st_ref[...]


def _ssd(xbc, z, dt_raw, conv_prev8, h0, conv_w, conv_b, dtb, alog, dsk, nw, *, t_in, T):
    B, S, _ = xbc.shape
    row = lambda w: pl.BlockSpec((1, t_in, w), lambda b, j: (b, j, 0))
    per_b = lambda r, w: pl.BlockSpec((1, r, w), lambda b, j: (b, 0, 0))
    const = lambda r, w: pl.BlockSpec((r, w), lambda b, j: (0, 0))
    return pl.pallas_call(
        functools.partial(_ssd_kernel, t_in=t_in, T=T),
        grid=(B, S // t_in),
        in_specs=[row(CONV_DIM), row(SSM_WIDTH), row(LANES),
                  per_b(8, CONV_DIM), per_b(SSM_WIDTH, SSM_STATE),
                  const(CONV_W, CONV_DIM), const(1, CONV_DIM), const(1, LANES), const(1, LANES),
                  const(1, SSM_WIDTH), const(1, SSM_WIDTH)],
        out_specs=[row(SSM_WIDTH), per_b(SSM_WIDTH, SSM_STATE)],
        out_shape=[jax.ShapeDtypeStruct((B, S, SSM_WIDTH), BF16),
                   jax.ShapeDtypeStruct((B, SSM_WIDTH, SSM_STATE), F32)],
        scratch_shapes=[pltpu.VMEM((8, CONV_DIM), F32), pltpu.VMEM((SSM_WIDTH, SSM_STATE), F32)],
        compiler_params=_cparams("arbitrary", "arbitrary"),
        name="ssd",
    )(xbc, z, dt_raw, conv_prev8, h0, conv_w, conv_b, dtb, alog, dsk, nw)


def _router(h, rwt_ref, rb_ref):
    tm = h.shape[0]
    G = N_EXPERT_GROUPS
    logits = _dot3(rwt_ref[...], h, dot=_dot_nt)
    score = jax.nn.sigmoid(logits[:N_EXPERTS, :])
    biased = score + rb_ref[:N_EXPERTS, :]
    sc = [score[G * j:G * (j + 1), :] for j in range(EXPERTS_PER_GROUP)]
    v = [biased[G * j:G * (j + 1), :] for j in range(EXPERTS_PER_GROUP)]

    def tree(op, xs):
        while len(xs) > 1:
            xs = [op(xs[a], xs[a + 1]) for a in range(0, len(xs) - 1, 2)] + ([xs[-1]] if len(xs) % 2 else [])
        return xs[0]

    m1 = tree(jnp.maximum, v)
    first = tree(jnp.minimum, [jnp.where(v[j] == m1, float(j), float(EXPERTS_PER_GROUP))
                               for j in range(EXPERTS_PER_GROUP)])
    m2 = tree(jnp.maximum, [jnp.where(first == float(j), -jnp.inf, v[j]) for j in range(EXPERTS_PER_GROUP)])
    grp = m1 + m2

    gid = lax.broadcasted_iota(jnp.int32, (G, tm), 0)
    cnt = jnp.zeros((G, tm), jnp.int32)
    for g2 in range(G):
        row = grp[g2:g2 + 1, :]
        beats = (row > grp) | ((row == grp) & (gid > g2))
        cnt = cnt + beats.astype(jnp.int32)
    gmask = cnt < TOPK_GROUPS

    idx = [(gid * EXPERTS_PER_GROUP + j).astype(F32) for j in range(EXPERTS_PER_GROUP)]
    mv = [jnp.where(gmask, v[j], -jnp.inf) for j in range(EXPERTS_PER_GROUP)]
    w = [jnp.zeros((G, tm), F32) for _ in range(EXPERTS_PER_GROUP)]
    for _ in range(TOP_K):
        top = jnp.max(tree(jnp.maximum, mv), axis=0, keepdims=True)
        who = jnp.min(tree(jnp.minimum, [jnp.where(mv[j] == top, idx[j], float(N_EXPERTS))
                                         for j in range(EXPERTS_PER_GROUP)]), axis=0, keepdims=True)
        for j in range(EXPERTS_PER_GROUP):
            hit = idx[j] == who
            w[j] = jnp.where(hit, sc[j], w[j])
            mv[j] = jnp.where(hit, -jnp.inf, mv[j])
    tot = w[0]
    for j in range(1, EXPERTS_PER_GROUP):
        tot = tot + w[j]
    tot = jnp.sum(tot, axis=0, keepdims=True)
    gates = [w[j] / tot * ROUTED_SCALE for j in range(EXPERTS_PER_GROUP)]
    shared = jnp.where(lax.broadcasted_iota(jnp.int32, (LANES - N_EXPERTS, tm), 0) == 0, 1.0, 0.0)
    by_slot = jnp.concatenate(gates + [shared], axis=0)
    e_id = lax.broadcasted_iota(jnp.int32, (LANES, LANES), 0)
    r_id = lax.broadcasted_iota(jnp.int32, (LANES, LANES), 1)
    src_row = jnp.where(e_id < N_EXPERTS, (e_id & (G - 1)) * EXPERTS_PER_GROUP + (e_id >> 3), e_id)
    perm = jnp.where(r_id == src_row, 1.0, 0.0).astype(BF16)
    return _dot_exact_lhs(perm, by_slot)


def _mix_kernel(x_ref, att_ref, ssm_ref, g1_ref, sc2_ref, sh2_ref, wo_ref, l1g_ref, l1b_ref, rwt_ref, rb_ref,
                x1_ref, h2_ref, gate_ref, *, bb, ts, alpha):
    tm = bb * ts
    att = att_ref[...].reshape(tm, DA_WIDTH)
    ssm = ssm_ref[...].reshape(tm, SSM_WIDTH)
    mix = _dot(att, wo_ref[:DA_WIDTH, :]) + _dot(ssm, wo_ref[DA_WIDTH:, :])
    y = alpha * x_ref[...] + (1.0 + g1_ref[...]) * mix.reshape(bb, ts, D_MODEL)
    x1 = _ln(y) * l1g_ref[...] + l1b_ref[...]
    x1_ref[...] = x1
    h2 = (_ln(x1) * (1.0 + sc2_ref[...]) + sh2_ref[...]).reshape(tm, D_MODEL)
    h2_ref[...] = h2.astype(BF16)
    gate_ref[...] = _router(h2, rwt_ref, rb_ref).T


def _mix(x, att, ssm, mod4, wo, l1g, l1b, rwt, rb, *, bb, ts, alpha):
    B, S, _ = x.shape
    tm = bb * ts
    row3 = lambda w: pl.BlockSpec((bb, ts, w), lambda i, b: (b, i, 0))
    modspec = lambda which: pl.BlockSpec((bb, None, 1, D_MODEL), lambda i, b: (b, which, 0, 0))
    const = lambda r, w: pl.BlockSpec((r, w), lambda i, b: (0, 0))
    flat = lambda w: pl.BlockSpec((tm, w), lambda i, b: (b * (S // ts) + i, 0))
    return pl.pallas_call(
        functools.partial(_mix_kernel, bb=bb, ts=ts, alpha=alpha),
        grid=(S // ts, B // bb),
        in_specs=[row3(D_MODEL), row3(DA_WIDTH), row3(SSM_WIDTH), modspec(2), modspec(4), modspec(3),
                  const(D_MODEL, D_MODEL), const(1, D_MODEL), const(1, D_MODEL),
                  const(LANES, D_MODEL), const(LANES, 1)],
        out_specs=[row3(D_MODEL), flat(D_MODEL), flat(LANES)],
        out_shape=[jax.ShapeDtypeStruct((B, S, D_MODEL), F32),
                   jax.ShapeDtypeStruct((B * S, D_MODEL), BF16),
                   jax.ShapeDtypeStruct((B * S, LANES), F32)],
        compiler_params=_cparams("arbitrary", "arbitrary"),
        name="mix",
    )(x, att, ssm, mod4, mod4, mod4, wo, l1g, l1b, rwt, rb)


N_EXPERTS_ALL = N_EXPERTS + 1
EXPERTS_PER_STEP = 5


def _moe_kernel(h_ref, gate_ref, x1_ref, g2_ref, wg_ref, wu_ref, wd_ref,
                l2g_ref, l2b_ref, o_ref, acc_ref, *, bb, ts, alpha):
    tm = bb * ts
    e = pl.program_id(2)
    h = h_ref[...]

    @pl.when(e == 0)
    def _():
        acc_ref[...] = jnp.zeros_like(acc_ref)

    gate = gate_ref[...]
    lane = lax.broadcasted_iota(jnp.int32, (tm, LANES), 1)
    hids = []
    for s in range(EXPERTS_PER_STEP):
        col = e * EXPERTS_PER_STEP + s
        gcol = jnp.sum(jnp.where(lane == col, gate, 0.0), axis=1, keepdims=True)
        hid = _silu(_dot(h, wg_ref[s])) * _dot(h, wu_ref[s]) * gcol
        hids.append(hid.astype(BF16))
    hid = jnp.concatenate(hids, axis=-1)
    acc_ref[...] += _dot(hid, wd_ref[...].reshape(EXPERTS_PER_STEP * EXPERT_DIM, D_MODEL))

    @pl.when(e == pl.num_programs(2) - 1)
    def _():
        ffn = acc_ref[...].reshape(bb, ts, D_MODEL)
        y = alpha * x1_ref[...] + (1.0 + g2_ref[...]) * ffn
        o_ref[...] = _ln(y) * l2g_ref[...] + l2b_ref[...]


def _moe(h2, gate, x1, mod4, wg, wu, wd, l2g, l2b, *, bb, ts, alpha):
    B, S, _ = x1.shape
    tm = bb * ts
    row3 = lambda w: pl.BlockSpec((bb, ts, w), lambda i, b, e: (b, i, 0))
    flat = lambda w: pl.BlockSpec((tm, w), lambda i, b, e: (b * (S // ts) + i, 0))
    const = lambda r, w: pl.BlockSpec((r, w), lambda i, b, e: (0, 0))
    ew_in = pl.BlockSpec((EXPERTS_PER_STEP, D_MODEL, EXPERT_DIM), lambda i, b, e: (e, 0, 0))
    ew_out = pl.BlockSpec((EXPERTS_PER_STEP, EXPERT_DIM, D_MODEL), lambda i, b, e: (e, 0, 0))
    return pl.pallas_call(
        functools.partial(_moe_kernel, bb=bb, ts=ts, alpha=alpha),
        grid=(S // ts, B // bb, N_EXPERTS_ALL // EXPERTS_PER_STEP),
        in_specs=[flat(D_MODEL), flat(LANES), row3(D_MODEL),
                  pl.BlockSpec((bb, None, 1, D_MODEL), lambda i, b, e: (b, 5, 0, 0)),
                  ew_in, ew_in, ew_out,
                  const(1, D_MODEL), const(1, D_MODEL)],
        out_specs=row3(D_MODEL),
        out_shape=jax.ShapeDtypeStruct((B, S, D_MODEL), F32),
        scratch_shapes=[pltpu.VMEM((tm, D_MODEL), F32)],
        compiler_params=_cparams("arbitrary", "arbitrary", "arbitrary"),
        name="moe",
    )(h2, gate, x1, mod4, wg, wu, wd, l2g, l2b)


def _rope_tables(pos):
    half = DA_HEAD_DIM // 2
    inv = ROPE_THETA ** (-jnp.arange(half, dtype=F32) / half)
    ang = pos.astype(F32)[:, None] * inv[None, :]
    cos = jnp.cos(ang)
    sin = jnp.sin(ang)
    reps = LANES // DA_HEAD_DIM
    cos_t = jnp.tile(jnp.concatenate([cos, cos], -1), (1, reps))
    sin_t = jnp.tile(jnp.concatenate([-sin, sin], -1), (1, reps))
    return cos_t, sin_t


def _layer(x, mod4, pos, prm, layer_idx, depth, k_past, v_past, conv_prev, ssm_prev, sample):
    B, S, _ = x.shape
    alpha = (2 * depth) ** 0.25
    lam_init = 0.8 - 0.6 * math.exp(-0.3 * layer_idx)
    cos_t, sin_t = _rope_tables(pos)
    if sample:
        bb, ts = B, S
        cos_t = jnp.tile(cos_t, (B, 1))
        sin_t = jnp.tile(sin_t, (B, 1))
    else:
        bb, ts = 1, min(512, S)
    qb, k, kb, v, vb, z, xbc, dt_raw = _proj(x, mod4, cos_t, sin_t, prm['w_main'], prm['w_dt'],
                                             bb=bb, ts=ts, v_transposed=not sample)
    if sample:
        att = _attn_sample(qb, kb, vb, k_past, v_past, prm['lam_vecs'], prm['subln_w'], lam_init)
        conv_prev8 = jnp.pad(conv_prev, ((0, 0), (8 - (CONV_W - 1), 0), (0, 0)))
        h0 = ssm_prev.reshape(B, SSM_WIDTH, SSM_STATE)
        t_in, T = S, max(S, LANES)
        new_conv = jnp.concatenate([conv_prev, xbc], axis=1)[:, -(CONV_W - 1):]
    else:
        att = _attn_prompt(qb, kb, vb, prm['lam_vecs'], prm['subln_w'], lam_init, tq=min(512, S))
        conv_prev8 = jnp.zeros((B, 8, CONV_DIM), F32)
        h0 = jnp.zeros((B, SSM_WIDTH, SSM_STATE), F32)
        t_in = T = min(256, S)
        new_conv = xbc[:, -(CONV_W - 1):]
    ssm, h_last = _ssd(xbc, z, dt_raw, conv_prev8, h0, prm['conv_w'], prm['conv_b'], prm['dt_bias'],
                       prm['a_log'], prm['d_skip'], prm['ssm_norm_w'], t_in=t_in, T=T)
    x1, h2, gate = _mix(x, att, ssm, mod4, prm['w_out'], prm['ln1_g'], prm['ln1_b'], prm['router_wt'],
                        prm['router_b'], bb=bb, ts=ts, alpha=alpha)
    if not sample:
        ts = min(1024, S)
    y = _moe(h2, gate, x1, mod4, prm['wg'], prm['wu'], prm['wd'],
             prm['ln2_g'], prm['ln2_b'], bb=bb, ts=ts, alpha=alpha)
    return (y, k.reshape(B, S, DA_HEADS, 2 * DA_HEAD_DIM), v.reshape(B, S, DA_HEADS, 2 * DA_HEAD_DIM),
            new_conv, h_last.reshape(B, SSM_HEADS, SSM_HEAD_DIM, SSM_STATE))


def _prep_params(w_in, lq1, lk1, lq2, lk2, subln_w, conv_w, conv_b, dt_bias, a_log, d_skip, ssm_norm_w, w_out,
                 ln1_g, ln1_b, router_w, router_bias, ewg, ewu, ewd, swg, swu, swd, ln2_g, ln2_b):
    pad8 = lambda a: jnp.pad(a.reshape(1, -1), ((0, 0), (0, LANES - a.shape[-1])))
    row = lambda a: a.reshape(1, -1)
    perm = (jnp.arange(N_EXPERTS) % N_EXPERT_GROUPS) * EXPERTS_PER_GROUP + jnp.arange(N_EXPERTS) // N_EXPERT_GROUPS
    rwt = jnp.pad(router_w.T[perm], ((0, LANES - N_EXPERTS), (0, 0)))
    rb = jnp.pad(router_bias[perm], (0, LANES - N_EXPERTS)).reshape(LANES, 1)
    with_shared = lambda w, s: jnp.concatenate([w.astype(BF16), s.astype(BF16)[None]], axis=0)
    return {
        'w_main': w_in[:, :DT_OFF].astype(BF16),
        'w_dt': jnp.pad(w_in[:, DT_OFF:], ((0, 0), (0, LANES - SSM_HEADS))),
        'lam_vecs': (row(lq1), row(lk1), row(lq2), row(lk2)),
        'subln_w': row(subln_w),
        'conv_w': conv_w, 'conv_b': row(conv_b),
        'dt_bias': pad8(dt_bias), 'a_log': pad8(a_log),
        'd_skip': row(jnp.repeat(d_skip, SSM_HEAD_DIM)), 'ssm_norm_w': row(ssm_norm_w),
        'w_out': w_out.astype(BF16), 'ln1_g': row(ln1_g), 'ln1_b': row(ln1_b),
        'router_wt': rwt, 'router_b': rb,
        'wg': with_shared(ewg, swg), 'wu': with_shared(ewu, swu), 'wd': with_shared(ewd, swd),
        'ln2_g': row(ln2_g), 'ln2_b': row(ln2_b),
    }


def kernel(x_prompt, x_sample, c_prompt, c_sample, cache_k, cache_v, state_conv, state_ssm, w_ada, b_ada, w_in, lambda_q1, lambda_k1, lambda_q2, lambda_k2, subln_w, conv_w, conv_b, dt_bias, a_log, d_skip, ssm_norm_w, w_out, ln1_g, ln1_b, router_w, router_bias, exp_w_gate, exp_w_up, exp_w_down, sh_w_gate, sh_w_up, sh_w_down, ln2_g, ln2_b):
    depth = w_ada.shape[0]
    bp, sp, _ = x_prompt.shape
    bs, ss, _ = x_sample.shape
    past = cache_k.shape[2]
    pos_prompt = jnp.arange(sp)
    pos_sample = past + jnp.arange(ss)
    c_rows = bp + bs
    c_pad = (-c_rows) % 8
    c_all = jnp.pad(jnp.concatenate([c_prompt, c_sample], 0), ((0, c_pad), (0, 0)))
    yp, ys = x_prompt, x_sample
    outs = [[] for _ in range(8)]
    for l in range(depth):
        prm = _prep_params(w_in[l], lambda_q1[l], lambda_k1[l], lambda_q2[l], lambda_k2[l], subln_w[l],
                           conv_w[l], conv_b[l], dt_bias[l], a_log[l], d_skip[l], ssm_norm_w[l], w_out[l],
                           ln1_g[l], ln1_b[l], router_w[l], router_bias[l], exp_w_gate[l], exp_w_up[l],
                           exp_w_down[l], sh_w_gate[l], sh_w_up[l], sh_w_down[l], ln2_g[l], ln2_b[l])
        mod = _ada(c_all, w_ada[l], b_ada[l]).reshape(c_rows + c_pad, 6, 1, D_MODEL)
        yp, kp, vp, cp, hp = _layer(yp, mod[:bp], pos_prompt, prm, l, depth, None, None, None, None, False)
        ys, kn, vn, cn, hn = _layer(ys, mod[bp:c_rows], pos_sample, prm, l, depth,
                                    cache_k[l].reshape(bs, past * DA_HEADS, LANES),
                                    cache_v[l].reshape(bs, past * DA_HEADS, LANES),
                                    state_conv[l], state_ssm[l], True)
        for lst, val in zip(outs, (kp, vp, cp, hp, kn, vn, cn, hn)):
            lst.append(val)
    return (yp, ys) + tuple(jnp.stack(o) for o in outs)
```

```python
import functools
import math

import jax
import jax.numpy as jnp
from jax import lax
from jax.experimental import pallas as pl
from jax.experimental.pallas import tpu as pltpu

F32 = jnp.float32
BF16 = jnp.bfloat16

D_MODEL = 1024
CHUNK = 64
CHUNK_SHIFT = 6
HEAD_SHIFT = 6
EPS = 1e-5
LOG2E = math.log2(math.e)
DA_HEADS = 4
DA_HEAD_DIM = 64
DA_WIDTH = DA_HEADS * 2 * DA_HEAD_DIM
ROPE_THETA = 10000.0
SSM_HEADS = 8
SSM_HEAD_DIM = 64
SSM_WIDTH = SSM_HEADS * SSM_HEAD_DIM
SSM_GROUPS = 2
SSM_STATE = 128
CONV_W = 4
CONV_DIM = SSM_WIDTH + 2 * SSM_GROUPS * SSM_STATE
Q_OFF = 0
K_OFF = 512
V_OFF = 1024
Z_OFF = 1536
XBC_OFF = 2048
DT_OFF = 3072
N_EXPERTS = 64
N_EXPERT_GROUPS = 8
EXPERTS_PER_GROUP = 8
TOPK_GROUPS = 4
TOP_K = 8
EXPERT_DIM = 256
ROUTED_SCALE = 2.5

LANES = 128
VMEM_LIMIT = 56 * 1024 * 1024


def _cparams(*sem):
    return pltpu.CompilerParams(dimension_semantics=sem, vmem_limit_bytes=VMEM_LIMIT)


def _dot(a, b):
    return jnp.dot(a, b, preferred_element_type=F32)


def _dot_nt(a, b):
    return lax.dot_general(a, b, (((1,), (1,)), ((), ())), preferred_element_type=F32)


def _split2(a):
    hi = a.astype(BF16)
    lo = (a - hi.astype(F32)).astype(BF16)
    return hi, lo


def _split3(a):
    hi = a.astype(BF16)
    r = a - hi.astype(F32)
    mid = r.astype(BF16)
    lo = (r - mid.astype(F32)).astype(BF16)
    return hi, mid, lo


def _dot3(a, b, dot=_dot):
    ah, al = _split2(a)
    bh, bl = _split2(b)
    return dot(ah, bh) + (dot(ah, bl) + dot(al, bh))


def _dot_exact_lhs(e, a, dot=_dot):
    ah, am, al = _split3(a)
    return dot(e, ah) + (dot(e, am) + dot(e, al))


def _silu(x):
    return x * jax.nn.sigmoid(x)


def _softplus(x):
    return jnp.maximum(x, 0.0) + jnp.log1p(jnp.exp(-jnp.abs(x)))


def _ln(x):
    mu = jnp.mean(x, -1, keepdims=True)
    xc = x - mu
    var = jnp.mean(xc * xc, -1, keepdims=True)
    return xc * lax.rsqrt(var + EPS)


def _ada_kernel(c_ref, w_ref, b_ref, o_ref):
    o_ref[...] = _dot3(_silu(c_ref[...]), w_ref[...]) + b_ref[...]


def _ada(c, w_ada, b_ada):
    rows = c.shape[0]
    n = w_ada.shape[1]
    tn = 1024
    return pl.pallas_call(
        _ada_kernel,
        grid=(n // tn,),
        in_specs=[pl.BlockSpec((rows, D_MODEL), lambda i: (0, 0)),
                  pl.BlockSpec((D_MODEL, tn), lambda i: (0, i)),
                  pl.BlockSpec((1, tn), lambda i: (0, i))],
        out_specs=pl.BlockSpec((rows, tn), lambda i: (0, i)),
        out_shape=jax.ShapeDtypeStruct((rows, n), F32),
        compiler_params=_cparams("arbitrary"),
        name="ada",
    )(c, w_ada, b_ada.reshape(1, n))


def _proj_core(x_ref, sc_ref, sh_ref, cos_ref, sin_ref, w_ref, wdt_ref,
               q_ref, k_ref, kb_ref, v_ref, vb_ref, *, bb, ts, v_transposed):
    tm = bb * ts
    h = _ln(x_ref[...]) * (1.0 + sc_ref[...]) + sh_ref[...]
    h = h.reshape(tm, D_MODEL)
    hb = h.astype(BF16)

    cos = jnp.tile(cos_ref[...], (1, DA_WIDTH // LANES))
    sin = jnp.tile(sin_ref[...], (1, DA_WIDTH // LANES))
    lane = lax.broadcasted_iota(jnp.int32, (tm, DA_WIDTH), 1)
    first = (lane & (DA_HEAD_DIM - 1)) < (DA_HEAD_DIM // 2)

    def rope(t):
        rot = jnp.where(first, pltpu.roll(t, DA_WIDTH - DA_HEAD_DIM // 2, 1),
                        pltpu.roll(t, DA_HEAD_DIM // 2, 1))
        return t * cos + rot * sin

    q = rope(_dot(hb, w_ref[:, Q_OFF:K_OFF])) * (DA_HEAD_DIM ** -0.5 * LOG2E)
    q_ref[...] = q.reshape(bb, ts, DA_WIDTH).astype(BF16)
    def store_heads(ref, t):
        for hd in range(DA_HEADS):
            ref[:, pl.ds(hd, ts, stride=DA_HEADS), :] = t[:, :, hd * LANES:(hd + 1) * LANES]

    k = rope(_dot(hb, w_ref[:, K_OFF:V_OFF])).reshape(bb, ts, DA_WIDTH)
    store_heads(k_ref, k)
    kb_ref[...] = k.astype(BF16)
    v = _dot(hb, w_ref[:, V_OFF:Z_OFF]).reshape(bb, ts, DA_WIDTH)
    store_heads(v_ref, v)
    if v_transposed:
        vb_ref[0, 0] = v.reshape(tm, DA_WIDTH).T.astype(BF16)
    else:
        vb_ref[...] = v.astype(BF16)
    z = _dot(hb, w_ref[:, Z_OFF:XBC_OFF])
    xbc = _dot(hb, w_ref[:, XBC_OFF:DT_OFF])
    return z, xbc, _dot3(h, wdt_ref[...])


def _proj_kernel(*refs, bb, ts, v_transposed):
    z_ref, xbc_ref, dt_ref = refs[-3:]
    z, xbc, dt_raw = _proj_core(*refs[:-3], bb=bb, ts=ts, v_transposed=v_transposed)
    z_ref[...] = z.reshape(bb, ts, SSM_WIDTH)
    xbc_ref[...] = xbc.reshape(bb, ts, CONV_DIM)
    dt_ref[...] = dt_raw.reshape(bb, ts, LANES)


def _proj_ssd_kernel(x_ref, sc_ref, sh_ref, cos_ref, sin_ref, w_ref, wdt_ref,
                     cw_ref, cb_ref, dtb_ref, alog_ref, dsk_ref, nw_ref,
                     q_ref, k_ref, kb_ref, v_ref, vb_ref, y_ref, hout_ref, tout_ref, tail_ref, st_ref, *, ts, T):
    @pl.when(pl.program_id(1) == 0)
    def _():
        tail_ref[...] = jnp.zeros_like(tail_ref)
        st_ref[...] = jnp.zeros_like(st_ref)

    z, xbc, dt_raw = _proj_core(x_ref, sc_ref, sh_ref, cos_ref, sin_ref, w_ref, wdt_ref,
                                q_ref, k_ref, kb_ref, v_ref, vb_ref, bb=1, ts=ts, v_transposed=True)
    for c in range(ts // T):
        rows = slice(c * T, (c + 1) * T)
        y = _ssd_chunk(xbc[rows], z[rows], dt_raw[rows], tail_ref, st_ref,
                       cw_ref, cb_ref, dtb_ref, alog_ref, dsk_ref, nw_ref, T=T, n_valid=T)
        y_ref[0, rows, :] = y.astype(BF16)
    @pl.when(pl.program_id(1) == pl.num_programs(1) - 1)
    def _():
        hout_ref[0] = st_ref[...]
        tout_ref[0] = tail_ref[...]


def _proj_specs(B, S, bb, ts, v_transposed, batch_major=False):
    tm = bb * ts
    spec = lambda shape, f: pl.BlockSpec(shape, (lambda b, i: f(i, b)) if batch_major else f)
    row3 = lambda w: spec((bb, ts, w), lambda i, b: (b, i, 0))
    modspec = lambda which: spec((bb, None, 1, D_MODEL), lambda i, b: (b, which, 0, 0))
    if v_transposed:
        vb_shape = jax.ShapeDtypeStruct((B, S // ts, DA_WIDTH, ts), BF16)
        vb_spec = spec((1, 1, DA_WIDTH, ts), lambda i, b: (b, i, 0, 0))
    else:
        vb_shape = jax.ShapeDtypeStruct((B, S, DA_WIDTH), BF16)
        vb_spec = row3(DA_WIDTH)
    heads_shape = jax.ShapeDtypeStruct((B, S * DA_HEADS, LANES), F32)
    heads_spec = spec((bb, ts * DA_HEADS, LANES), lambda i, b: (b, i, 0))
    in_specs = [row3(D_MODEL), modspec(1), modspec(0),
                spec((tm, LANES), lambda i, b: (i, 0)),
                spec((tm, LANES), lambda i, b: (i, 0)),
                spec((D_MODEL, DT_OFF), lambda i, b: (0, 0)),
                spec((D_MODEL, LANES), lambda i, b: (0, 0))]
    out_shapes = [jax.ShapeDtypeStruct((B, S, DA_WIDTH), BF16),
                  heads_shape,
                  jax.ShapeDtypeStruct((B, S, DA_WIDTH), BF16),
                  heads_shape,
                  vb_shape]
    out_specs = [row3(DA_WIDTH), heads_spec, row3(DA_WIDTH), heads_spec, vb_spec]
    return row3, in_specs, out_shapes, out_specs


def _proj(x, mod4, cos_t, sin_t, w_main, w_dt, *, bb, ts, v_transposed):
    B, S, _ = x.shape
    row3, in_specs, out_shapes, out_specs = _proj_specs(B, S, bb, ts, v_transposed)
    out_shapes += [jax.ShapeDtypeStruct((B, S, SSM_WIDTH), F32),
                   jax.ShapeDtypeStruct((B, S, CONV_DIM), F32),
                   jax.ShapeDtypeStruct((B, S, LANES), F32)]
    out_specs += [row3(SSM_WIDTH), row3(CONV_DIM), row3(LANES)]
    return pl.pallas_call(
        functools.partial(_proj_kernel, bb=bb, ts=ts, v_transposed=v_transposed),
        grid=(S // ts, B // bb),
        in_specs=in_specs,
        out_specs=out_specs,
        out_shape=out_shapes,
        compiler_params=_cparams("arbitrary", "arbitrary"),
        name="proj",
    )(x, mod4, mod4, cos_t, sin_t, w_main, w_dt)


def _proj_ssd(x, mod4, cos_t, sin_t, w_main, w_dt, conv_w, conv_b, dtb, alog, dsk, nw, *, ts, T):
    B, S, _ = x.shape
    row3, in_specs, out_shapes, out_specs = _proj_specs(B, S, 1, ts, True, batch_major=True)
    const = lambda r, w: pl.BlockSpec((r, w), lambda b, i: (0, 0))
    per_b = lambda r, w: pl.BlockSpec((1, r, w), lambda b, i: (b, 0, 0))
    in_specs += [const(CONV_W, CONV_DIM), const(1, CONV_DIM), const(1, LANES), const(1, LANES),
                 const(1, SSM_WIDTH), const(1, SSM_WIDTH)]
    out_shapes += [jax.ShapeDtypeStruct((B, S, SSM_WIDTH), BF16),
                   jax.ShapeDtypeStruct((B, SSM_WIDTH, SSM_STATE), F32),
                   jax.ShapeDtypeStruct((B, 8, CONV_DIM), F32)]
    out_specs += [row3(SSM_WIDTH), per_b(SSM_WIDTH, SSM_STATE), per_b(8, CONV_DIM)]
    return pl.pallas_call(
        functools.partial(_proj_ssd_kernel, ts=ts, T=T),
        grid=(B, S // ts),
        in_specs=in_specs,
        out_specs=out_specs,
        out_shape=out_shapes,
        scratch_shapes=[pltpu.VMEM((8, CONV_DIM), F32), pltpu.VMEM((SSM_WIDTH, SSM_STATE), F32)],
        compiler_params=_cparams("arbitrary", "arbitrary"),
        name="proj_ssd",
    )(x, mod4, mod4, cos_t, sin_t, w_main, w_dt, conv_w, conv_b, dtb, alog, dsk, nw)


def _lambda(lq1, lk1, lq2, lk2, lam_init):
    s1 = jnp.sum(lq1[...] * lk1[...], axis=1, keepdims=True)
    s2 = jnp.sum(lq2[...] * lk2[...], axis=1, keepdims=True)
    return jnp.exp(s1) - jnp.exp(s2) + lam_init


def _subln(o, subw, lam_init):
    ms = jnp.mean(o * o, -1, keepdims=True)
    return o * lax.rsqrt(ms + EPS) * subw * (1.0 - lam_init)


def _attn_prompt_kernel(lq1, lk1, lq2, lk2, subw_ref, q_ref, k_ref, vt_ref, o_ref,
                        acc1, acc2, sa1, sa2, sb1, sb2, mxa, mxb, m1, l1, m2, l2, *, tq, tk, lam_init):
    i = pl.program_id(2)
    qt = q_ref[0].astype(F32).T
    row = lax.broadcasted_iota(jnp.int32, (LANES, tq), 0)
    q1t = jnp.where(row < DA_HEAD_DIM, qt, 0.0).astype(BF16)
    q2t = jnp.where(row >= DA_HEAD_DIM, qt, 0.0).astype(BF16)
    acc1[...] = jnp.zeros_like(acc1)
    acc2[...] = jnp.zeros_like(acc2)
    m1[...] = jnp.full_like(m1, -jnp.inf)
    m2[...] = jnp.full_like(m2, -jnp.inf)
    l1[...] = jnp.zeros_like(l1)
    l2[...] = jnp.zeros_like(l2)
    buf_a = (sa1, sa2, mxa)
    buf_b = (sb1, sb2, mxb)

    def produce(j, buf):
        k = k_ref[0, pl.ds(pl.multiple_of(j * tk, tk), tk), :]
        for half, qh in enumerate((q1t, q2t)):
            s = _dot(k, qh)
            buf[half][...] = s
            buf[2][half:half + 1, :] = jnp.max(s, axis=0, keepdims=True)

    def consume(j, buf, masked):
        vt = vt_ref[0, j]
        if masked:
            key = (j * tk + lax.broadcasted_iota(jnp.int32, (tk, tq), 0)) >> CHUNK_SHIFT
            qry = (i * tq + lax.broadcasted_iota(jnp.int32, (tk, tq), 1)) >> CHUNK_SHIFT
            keep = key <= qry
        for half, (m, l, acc) in enumerate(((m1, l1, acc1), (m2, l2, acc2))):
            st = buf[half][...]
            if masked:
                st = jnp.where(keep, st, -jnp.inf)
                tile_max = jnp.max(st, axis=0, keepdims=True)
            else:
                tile_max = buf[2][half:half + 1, :]
            mo = m[...]
            mn = jnp.maximum(mo, tile_max)
            a = jnp.exp2(mo - mn)
            p = jnp.exp2(st - mn)
            l[...] = a * l[...] + jnp.sum(p, axis=0, keepdims=True)
            m[...] = mn
            acc[...] = a * acc[...] + _dot(vt, p.astype(BF16))

    nfull = (i * tq) // tk
    produce(nfull, buf_a)
    produce(0, buf_b)
    consume(nfull, buf_a, True)

    def pair(jj, carry):
        produce(2 * jj + 1, buf_a)
        consume(2 * jj, buf_b, False)
        produce(2 * jj + 2, buf_b)
        consume(2 * jj + 1, buf_a, False)
        return carry

    lax.fori_loop(0, nfull >> 1, pair, 0)

    @pl.when((nfull & 1) == 1)
    def _():
        consume(nfull - 1, buf_b, False)

    lam = _lambda(lq1, lk1, lq2, lk2, lam_init)
    ot = acc1[...] / l1[...] - lam * (acc2[...] / l2[...])
    ot = ot * lax.rsqrt(jnp.mean(ot * ot, axis=0, keepdims=True) + EPS)
    o_ref[0] = (ot.T * subw_ref[...] * (1.0 - lam_init)).astype(BF16)


def _attn_prompt(qb, kb, vt, lam_vecs, subw, lam_init, *, tq):
    B, S, _ = qb.shape
    nkt, tk = vt.shape[1], vt.shape[3]
    small = pl.BlockSpec((1, DA_HEAD_DIM), lambda b, h, i: (0, 0))
    return pl.pallas_call(
        functools.partial(_attn_prompt_kernel, tq=tq, tk=tk, lam_init=lam_init),
        grid=(B, DA_HEADS, S // tq),
        in_specs=[small, small, small, small,
                  pl.BlockSpec((1, LANES), lambda b, h, i: (0, 0)),
                  pl.BlockSpec((1, tq, LANES), lambda b, h, i: (b, i, h)),
                  pl.BlockSpec((1, S, LANES), lambda b, h, i: (b, 0, h)),
                  pl.BlockSpec((1, nkt, LANES, tk), lambda b, h, i: (b, 0, h, 0))],
        out_specs=pl.BlockSpec((1, tq, LANES), lambda b, h, i: (b, i, h)),
        out_shape=jax.ShapeDtypeStruct((B, S, DA_WIDTH), BF16),
        scratch_shapes=([pltpu.VMEM((LANES, tq), F32)] * 2 + [pltpu.VMEM((tk, tq), F32)] * 4
                        + [pltpu.VMEM((2, tq), F32)] * 2 + [pltpu.VMEM((1, tq), F32)] * 4),
        compiler_params=_cparams("arbitrary", "arbitrary", "arbitrary"),
        name="attn_prompt",
    )(*lam_vecs, subw, qb, kb, vt)


def _attn_sample_kernel(lq1, lk1, lq2, lk2, subw_ref, q_ref, kn_ref, vn_ref, kp_ref, vp_ref, o_ref,
                        *, lam_init):
    s = q_ref.shape[1]
    lam = _lambda(lq1, lk1, lq2, lk2, lam_init)
    lane = lax.broadcasted_iota(jnp.int32, (s, LANES), 1)
    outs = []
    for h in range(DA_HEADS):
        cols = slice(h * LANES, (h + 1) * LANES)
        q = q_ref[0, :, cols]
        zero = jnp.zeros_like(q)
        kn = kn_ref[0, :, cols]
        vn = vn_ref[0, :, cols]
        past = kp_ref.shape[1] // DA_HEADS
        kp = kp_ref[0, pl.ds(h, past, stride=DA_HEADS), :].astype(BF16)
        vp = vp_ref[0, pl.ds(h, past, stride=DA_HEADS), :].astype(BF16)
        probs = []
        for qm in (jnp.where(lane < DA_HEAD_DIM, q, zero), jnp.where(lane >= DA_HEAD_DIM, q, zero)):
            sp = _dot_nt(qm, kp)
            sn = _dot_nt(qm, kn)
            m = jnp.maximum(jnp.max(sp, -1, keepdims=True), jnp.max(sn, -1, keepdims=True))
            pp = jnp.exp2(sp - m)
            pn = jnp.exp2(sn - m)
            l = jnp.sum(pp, -1, keepdims=True) + jnp.sum(pn, -1, keepdims=True)
            probs.append((pp / l, pn / l))
        ap = probs[0][0] - lam * probs[1][0]
        an = probs[0][1] - lam * probs[1][1]
        o = _dot(ap.astype(BF16), vp) + _dot(an.astype(BF16), vn)
        outs.append(_subln(o, subw_ref[...], lam_init))
    o_ref[0] = jnp.concatenate(outs, axis=-1).astype(BF16)


def _attn_sample(qb, kb, vb, k_past, v_past, lam_vecs, subw, lam_init):
    B, s, _ = qb.shape
    past_rows = k_past.shape[1]
    small = pl.BlockSpec((1, DA_HEAD_DIM), lambda b: (0, 0))
    new = pl.BlockSpec((1, s, DA_WIDTH), lambda b: (b, 0, 0))
    old = pl.BlockSpec((1, past_rows, LANES), lambda b: (b, 0, 0))
    return pl.pallas_call(
        functools.partial(_attn_sample_kernel, lam_init=lam_init),
        grid=(B,),
        in_specs=[small, small, small, small, pl.BlockSpec((1, LANES), lambda b: (0, 0)),
                  new, new, new, old, old],
        out_specs=new,
        out_shape=jax.ShapeDtypeStruct((B, s, DA_WIDTH), BF16),
        compiler_params=_cparams("arbitrary"),
        name="attn_sample",
    )(*lam_vecs, subw, qb, kb, vb, k_past, v_past)


def _ssd_kernel(xbc_ref, z_ref, dt_ref, cp_ref, h0_ref, cw_ref, cb_ref, dtb_ref, alog_ref, dsk_ref, nw_ref,
                y_ref, hout_ref, tail_ref, st_ref, *, t_in, T):
    j = pl.program_id(1)

    @pl.when(j == 0)
    def _():
        tail_ref[...] = cp_ref[0]
        st_ref[...] = h0_ref[0]

    def rows(ref, width):
        val = ref[0]
        if T == t_in:
            return val
        return jnp.concatenate([val, jnp.zeros((T - t_in, width), F32)], axis=0)

    y = _ssd_chunk(rows(xbc_ref, CONV_DIM), rows(z_ref, SSM_WIDTH), rows(dt_ref, LANES), tail_ref, st_ref,
                   cw_ref, cb_ref, dtb_ref, alog_ref, dsk_ref, nw_ref, T=T, n_valid=t_in)
    y_ref[0] = y[:t_in, :].astype(BF16)

    @pl.when(j == pl.num_programs(1) - 1)
    def _():
        hout_ref[0] = st_ref[...]


def _ssd_chunk(xb, zb, dt_raw, tail_ref, st_ref, cw_ref, cb_ref, dtb_ref, alog_ref, dsk_ref, nw_ref, *, T, n_valid):
    PAIR = 2 * SSM_HEAD_DIM
    xp = jnp.concatenate([tail_ref[...], xb], axis=0)
    conv = cb_ref[...] + cw_ref[CONV_W - 1:CONV_W, :] * xb
    for sft in range(1, CONV_W):
        conv = conv + cw_ref[CONV_W - 1 - sft:CONV_W - sft, :] * pltpu.roll(xp, sft, 0)[8:, :]
    if T == n_valid:
        tail_ref[...] = xb[T - 8:, :]
    xa = _silu(conv)
    xs = xa[:, :SSM_WIDTH]

    dt = _softplus(dt_raw + dtb_ref[...])
    if T != n_valid:
        valid = lax.broadcasted_iota(jnp.int32, (T, LANES), 0) < n_valid
        dt = jnp.where(valid, dt, 0.0)
    da = dt * (-jnp.exp(alog_ref[...]))

    rr = lax.broadcasted_iota(jnp.int32, (T, T), 0)
    cc = lax.broadcasted_iota(jnp.int32, (T, T), 1)
    causal = cc <= rr
    tril = jnp.where(causal, 1.0, 0.0).astype(BF16)
    acs = _dot_exact_lhs(tril, da)
    e16 = jnp.where(lax.broadcasted_iota(jnp.int32, (16, LANES), 0)
                    == lax.broadcasted_iota(jnp.int32, (16, LANES), 1), 1.0, 0.0).astype(BF16)
    acs_t = _dot_exact_lhs(e16, acs, dot=_dot_nt)

    last = acs[T - 1:T, :]
    expand = jnp.where((lax.broadcasted_iota(jnp.int32, (LANES, SSM_WIDTH), 1) >> HEAD_SHIFT)
                       == lax.broadcasted_iota(jnp.int32, (LANES, SSM_WIDTH), 0), 1.0, 0.0).astype(BF16)

    def per_head_lanes(v):
        hi, lo = _split2(v)
        return _dot(hi, expand) + _dot(lo, expand)

    xdt = xs * per_head_lanes(dt)
    xdd = xdt * per_head_lanes(jnp.exp(last - acs))
    eacs = per_head_lanes(jnp.exp(acs))

    head_of_row = lax.broadcasted_iota(jnp.int32, (SSM_WIDTH, LANES), 0) >> HEAD_SHIFT
    sel = head_of_row == lax.broadcasted_iota(jnp.int32, (SSM_WIDTH, LANES), 1)
    rowdec = jnp.sum(jnp.where(sel, jnp.exp(last), 0.0), axis=1, keepdims=True)

    lane = lax.broadcasted_iota(jnp.int32, (T, PAIR), 1)
    ys = []
    for p in range(SSM_HEADS // 2):
        g = (2 * p) // (SSM_HEADS // SSM_GROUPS)
        bm = xa[:, SSM_WIDTH + g * SSM_STATE:SSM_WIDTH + (g + 1) * SSM_STATE].astype(BF16)
        cm = xa[:, SSM_WIDTH + (SSM_GROUPS + g) * SSM_STATE:
                SSM_WIDTH + (SSM_GROUPS + g + 1) * SSM_STATE].astype(BF16)
        cb = _dot_nt(cm, bm)
        cols = slice(p * PAIR, (p + 1) * PAIR)
        xdt_p = xdt[:, cols].astype(BF16)
        yd = []
        for hh in (2 * p, 2 * p + 1):
            seg = acs[:, hh:hh + 1] - acs_t[hh:hh + 1, :]
            lmat = jnp.exp(jnp.where(causal, seg, -jnp.inf))
            yd.append(_dot((cb * lmat).astype(BF16), xdt_p))
        y_diag = jnp.where(lane < SSM_HEAD_DIM, yd[0], yd[1])
        st = st_ref[cols, :]
        y_off = _dot_nt(cm, st.astype(BF16)) * eacs[:, cols]
        ys.append(y_diag + y_off)
        st_ref[cols, :] = st * rowdec[cols, :] + _dot(xdd[:, cols].T.astype(BF16), bm)
    y = jnp.concatenate(ys, axis=-1) + xs * dsk_ref[...]
    y = y * _silu(zb)
    half = SSM_WIDTH // SSM_GROUPS
    outs = []
    for g in range(SSM_GROUPS):
        yg = y[:, g * half:(g + 1) * half]
        outs.append(yg * lax.rsqrt(jnp.mean(yg * yg, -1, keepdims=True) + EPS))
    return jnp.concatenate(outs, axis=-1) * nw_ref[...]


def _ssd(xbc, z, dt_raw, conv_prev8, h0, conv_w, conv_b, dtb, alog, dsk, nw, *, t_in, T):
    B, S, _ = xbc.shape
    row = lambda w: pl.BlockSpec((1, t_in, w), lambda b, j: (b, j, 0))
    per_b = lambda r, w: pl.BlockSpec((1, r, w), lambda b, j: (b, 0, 0))
    const = lambda r, w: pl.BlockSpec((r, w), lambda b, j: (0, 0))
    return pl.pallas_call(
        functools.partial(_ssd_kernel, t_in=t_in, T=T),
        grid=(B, S // t_in),
        in_specs=[row(CONV_DIM), row(SSM_WIDTH), row(LANES),
                  per_b(8, CONV_DIM), per_b(SSM_WIDTH, SSM_STATE),
                  const(CONV_W, CONV_DIM), const(1, CONV_DIM), const(1, LANES), const(1, LANES),
                  const(1, SSM_WIDTH), const(1, SSM_WIDTH)],
        out_specs=[row(SSM_WIDTH), per_b(SSM_WIDTH, SSM_STATE)],
        out_shape=[jax.ShapeDtypeStruct((B, S, SSM_WIDTH), BF16),
                   jax.ShapeDtypeStruct((B, SSM_WIDTH, SSM_STATE), F32)],
        scratch_shapes=[pltpu.VMEM((8, CONV_DIM), F32), pltpu.VMEM((SSM_WIDTH, SSM_STATE), F32)],
        compiler_params=_cparams("arbitrary", "arbitrary"),
        name="ssd",
    )(xbc, z, dt_raw, conv_prev8, h0, conv_w, conv_b, dtb, alog, dsk, nw)


def _router(h, rwt_ref, rb_ref):
    tm = h.shape[0]
    G = N_EXPERT_GROUPS
    logits = _dot3(rwt_ref[...], h, dot=_dot_nt)
    score = jax.nn.sigmoid(logits[:N_EXPERTS, :])
    biased = score + rb_ref[:N_EXPERTS, :]
    sc = [score[G * j:G * (j + 1), :] for j in range(EXPERTS_PER_GROUP)]
    v = [biased[G * j:G * (j + 1), :] for j in range(EXPERTS_PER_GROUP)]

    def tree(op, xs):
        while len(xs) > 1:
            xs = [op(xs[a], xs[a + 1]) for a in range(0, len(xs) - 1, 2)] + ([xs[-1]] if len(xs) % 2 else [])
        return xs[0]

    m1 = tree(jnp.maximum, v)
    first = tree(jnp.minimum, [jnp.where(v[j] == m1, float(j), float(EXPERTS_PER_GROUP))
                               for j in range(EXPERTS_PER_GROUP)])
    m2 = tree(jnp.maximum, [jnp.where(first == float(j), -jnp.inf, v[j]) for j in range(EXPERTS_PER_GROUP)])
    grp = m1 + m2

    gid = lax.broadcasted_iota(jnp.int32, (G, tm), 0)
    cnt = jnp.zeros((G, tm), jnp.int32)
    for g2 in range(G):
        row = grp[g2:g2 + 1, :]
        beats = (row > grp) | ((row == grp) & (gid > g2))
        cnt = cnt + beats.astype(jnp.int32)
    gmask = cnt < TOPK_GROUPS

    idx = [(gid * EXPERTS_PER_GROUP + j).astype(F32) for j in range(EXPERTS_PER_GROUP)]
    mv = [jnp.where(gmask, v[j], -jnp.inf) for j in range(EXPERTS_PER_GROUP)]
    w = [jnp.zeros((G, tm), F32) for _ in range(EXPERTS_PER_GROUP)]
    for _ in range(TOP_K):
        top = jnp.max(tree(jnp.maximum, mv), axis=0, keepdims=True)
        who = jnp.min(tree(jnp.minimum, [jnp.where(mv[j] == top, idx[j], float(N_EXPERTS))
                                         for j in range(EXPERTS_PER_GROUP)]), axis=0, keepdims=True)
        for j in range(EXPERTS_PER_GROUP):
            hit = idx[j] == who
            w[j] = jnp.where(hit, sc[j], w[j])
            mv[j] = jnp.where(hit, -jnp.inf, mv[j])
    tot = w[0]
    for j in range(1, EXPERTS_PER_GROUP):
        tot = tot + w[j]
    tot = jnp.sum(tot, axis=0, keepdims=True)
    gates = [w[j] / tot * ROUTED_SCALE for j in range(EXPERTS_PER_GROUP)]
    shared = jnp.where(lax.broadcasted_iota(jnp.int32, (LANES - N_EXPERTS, tm), 0) == 0, 1.0, 0.0)
    by_slot = jnp.concatenate(gates + [shared], axis=0)
    e_id = lax.broadcasted_iota(jnp.int32, (LANES, LANES), 0)
    r_id = lax.broadcasted_iota(jnp.int32, (LANES, LANES), 1)
    src_row = jnp.where(e_id < N_EXPERTS, (e_id & (G - 1)) * EXPERTS_PER_GROUP + (e_id >> 3), e_id)
    perm = jnp.where(r_id == src_row, 1.0, 0.0).astype(BF16)
    return _dot_exact_lhs(perm, by_slot)


def _mix_kernel(x_ref, att_ref, ssm_ref, g1_ref, sc2_ref, sh2_ref, wo_ref, l1g_ref, l1b_ref, rwt_ref, rb_ref,
                x1_ref, h2_ref, gate_ref, *, bb, ts, alpha):
    tm = bb * ts
    att = att_ref[...].reshape(tm, DA_WIDTH)
    ssm = ssm_ref[...].reshape(tm, SSM_WIDTH)
    mix = _dot(att, wo_ref[:DA_WIDTH, :]) + _dot(ssm, wo_ref[DA_WIDTH:, :])
    y = alpha * x_ref[...] + (1.0 + g1_ref[...]) * mix.reshape(bb, ts, D_MODEL)
    x1 = _ln(y) * l1g_ref[...] + l1b_ref[...]
    x1_ref[...] = x1
    h2 = (_ln(x1) * (1.0 + sc2_ref[...]) + sh2_ref[...]).reshape(tm, D_MODEL)
    h2_ref[...] = h2.astype(BF16)
    gate_ref[...] = _router(h2, rwt_ref, rb_ref).T


def _mix(x, att, ssm, mod4, wo, l1g, l1b, rwt, rb, *, bb, ts, alpha):
    B, S, _ = x.shape
    tm = bb * ts
    row3 = lambda w: pl.BlockSpec((bb, ts, w), lambda i, b: (b, i, 0))
    modspec = lambda which: pl.BlockSpec((bb, None, 1, D_MODEL), lambda i, b: (b, which, 0, 0))
    const = lambda r, w: pl.BlockSpec((r, w), lambda i, b: (0, 0))
    flat = lambda w: pl.BlockSpec((tm, w), lambda i, b: (b * (S // ts) + i, 0))
    return pl.pallas_call(
        functools.partial(_mix_kernel, bb=bb, ts=ts, alpha=alpha),
        grid=(S // ts, B // bb),
        in_specs=[row3(D_MODEL), row3(DA_WIDTH), row3(SSM_WIDTH), modspec(2), modspec(4), modspec(3),
                  const(D_MODEL, D_MODEL), const(1, D_MODEL), const(1, D_MODEL),
                  const(LANES, D_MODEL), const(LANES, 1)],
        out_specs=[row3(D_MODEL), flat(D_MODEL), flat(LANES)],
        out_shape=[jax.ShapeDtypeStruct((B, S, D_MODEL), F32),
                   jax.ShapeDtypeStruct((B * S, D_MODEL), BF16),
                   jax.ShapeDtypeStruct((B * S, LANES), F32)],
        compiler_params=_cparams("arbitrary", "arbitrary"),
        name="mix",
    )(x, att, ssm, mod4, mod4, mod4, wo, l1g, l1b, rwt, rb)


N_EXPERTS_ALL = N_EXPERTS + 1
EXPERTS_PER_STEP = 5


def _moe_kernel(h_ref, gate_ref, x1_ref, g2_ref, wg_ref, wu_ref, wd_ref,
                l2g_ref, l2b_ref, o_ref, acc_ref, *, bb, ts, alpha):
    tm = bb * ts
    e = pl.program_id(2)
    h = h_ref[...]

    @pl.when(e == 0)
    def _():
        acc_ref[...] = jnp.zeros_like(acc_ref)

    gate = gate_ref[...]
    lane = lax.broadcasted_iota(jnp.int32, (tm, LANES), 1)
    hids = []
    for s in range(EXPERTS_PER_STEP):
        col = e * EXPERTS_PER_STEP + s
        gcol = jnp.sum(jnp.where(lane == col, gate, 0.0), axis=1, keepdims=True)
        hid = _silu(_dot(h, wg_ref[s])) * _dot(h, wu_ref[s]) * gcol
        hids.append(hid.astype(BF16))
    hid = jnp.concatenate(hids, axis=-1)
    acc_ref[...] += _dot(hid, wd_ref[...].reshape(EXPERTS_PER_STEP * EXPERT_DIM, D_MODEL))

    @pl.when(e == pl.num_programs(2) - 1)
    def _():
        ffn = acc_ref[...].reshape(bb, ts, D_MODEL)
        y = alpha * x1_ref[...] + (1.0 + g2_ref[...]) * ffn
        o_ref[...] = _ln(y) * l2g_ref[...] + l2b_ref[...]


def _moe(h2, gate, x1, mod4, wg, wu, wd, l2g, l2b, *, bb, ts, alpha):
    B, S, _ = x1.shape
    tm = bb * ts
    row3 = lambda w: pl.BlockSpec((bb, ts, w), lambda i, b, e: (b, i, 0))
    flat = lambda w: pl.BlockSpec((tm, w), lambda i, b, e: (b * (S // ts) + i, 0))
    const = lambda r, w: pl.BlockSpec((r, w), lambda i, b, e: (0, 0))
    ew_in = pl.BlockSpec((EXPERTS_PER_STEP, D_MODEL, EXPERT_DIM), lambda i, b, e: (e, 0, 0))
    ew_out = pl.BlockSpec((EXPERTS_PER_STEP, EXPERT_DIM, D_MODEL), lambda i, b, e: (e, 0, 0))
    return pl.pallas_call(
        functools.partial(_moe_kernel, bb=bb, ts=ts, alpha=alpha),
        grid=(S // ts, B // bb, N_EXPERTS_ALL // EXPERTS_PER_STEP),
        in_specs=[flat(D_MODEL), flat(LANES), row3(D_MODEL),
                  pl.BlockSpec((bb, None, 1, D_MODEL), lambda i, b, e: (b, 5, 0, 0)),
                  ew_in, ew_in, ew_out,
                  const(1, D_MODEL), const(1, D_MODEL)],
        out_specs=row3(D_MODEL),
        out_shape=jax.ShapeDtypeStruct((B, S, D_MODEL), F32),
        scratch_shapes=[pltpu.VMEM((tm, D_MODEL), F32)],
        compiler_params=_cparams("arbitrary", "arbitrary", "arbitrary"),
        name="moe",
    )(h2, gate, x1, mod4, wg, wu, wd, l2g, l2b)


def _rope_tables(pos):
    half = DA_HEAD_DIM // 2
    inv = ROPE_THETA ** (-jnp.arange(half, dtype=F32) / half)
    ang = pos.astype(F32)[:, None] * inv[None, :]
    cos = jnp.cos(ang)
    sin = jnp.sin(ang)
    reps = LANES // DA_HEAD_DIM
    cos_t = jnp.tile(jnp.concatenate([cos, cos], -1), (1, reps))
    sin_t = jnp.tile(jnp.concatenate([-sin, sin], -1), (1, reps))
    return cos_t, sin_t


def _layer(x, mod4, pos, prm, layer_idx, depth, k_past, v_past, conv_prev, ssm_prev, sample):
    B, S, _ = x.shape
    alpha = (2 * depth) ** 0.25
    lam_init = 0.8 - 0.6 * math.exp(-0.3 * layer_idx)
    cos_t, sin_t = _rope_tables(pos)
    if sample:
        bb, ts = B, S
        cos_t = jnp.tile(cos_t, (B, 1))
        sin_t = jnp.tile(sin_t, (B, 1))
    else:
        bb, ts = 1, min(512, S)
    ssd_consts = (prm['conv_w'], prm['conv_b'], prm['dt_bias'], prm['a_log'], prm['d_skip'], prm['ssm_norm_w'])
    if sample:
        qb, k, kb, v, vb, z, xbc, dt_raw = _proj(x, mod4, cos_t, sin_t, prm['w_main'], prm['w_dt'],
                                                 bb=bb, ts=ts, v_transposed=False)
        att = _attn_sample(qb, kb, vb, k_past, v_past, prm['lam_vecs'], prm['subln_w'], lam_init)
        conv_prev8 = jnp.pad(conv_prev, ((0, 0), (8 - (CONV_W - 1), 0), (0, 0)))
        h0 = ssm_prev.reshape(B, SSM_WIDTH, SSM_STATE)
        ssm, h_last = _ssd(xbc, z, dt_raw, conv_prev8, h0, *ssd_consts, t_in=S, T=max(S, LANES))
        new_conv = jnp.concatenate([conv_prev, xbc], axis=1)[:, -(CONV_W - 1):]
    else:
        qb, k, kb, v, vb, ssm, h_last, tail = _proj_ssd(x, mod4, cos_t, sin_t, prm['w_main'], prm['w_dt'],
                                                        *ssd_consts, ts=ts, T=min(256, S))
        att = _attn_prompt(qb, kb, vb, prm['lam_vecs'], prm['subln_w'], lam_init, tq=min(512, S))
        new_conv = tail[:, -(CONV_W - 1):]
    x1, h2, gate = _mix(x, att, ssm, mod4, prm['w_out'], prm['ln1_g'], prm['ln1_b'], prm['router_wt'],
                        prm['router_b'], bb=bb, ts=ts, alpha=alpha)
    if not sample:
        ts = min(1024, S)
    y = _moe(h2, gate, x1, mod4, prm['wg'], prm['wu'], prm['wd'],
             prm['ln2_g'], prm['ln2_b'], bb=bb, ts=ts, alpha=alpha)
    return (y, k.reshape(B, S, DA_HEADS, 2 * DA_HEAD_DIM), v.reshape(B, S, DA_HEADS, 2 * DA_HEAD_DIM),
            new_conv, h_last.reshape(B, SSM_HEADS, SSM_HEAD_DIM, SSM_STATE))


def _prep_params(w_in, lq1, lk1, lq2, lk2, subln_w, conv_w, conv_b, dt_bias, a_log, d_skip, ssm_norm_w, w_out,
                 ln1_g, ln1_b, router_w, router_bias, ewg, ewu, ewd, swg, swu, swd, ln2_g, ln2_b):
    pad8 = lambda a: jnp.pad(a.reshape(1, -1), ((0, 0), (0, LANES - a.shape[-1])))
    row = lambda a: a.reshape(1, -1)
    perm = (jnp.arange(N_EXPERTS) % N_EXPERT_GROUPS) * EXPERTS_PER_GROUP + jnp.arange(N_EXPERTS) // N_EXPERT_GROUPS
    rwt = jnp.pad(router_w.T[perm], ((0, LANES - N_EXPERTS), (0, 0)))
    rb = jnp.pad(router_bias[perm], (0, LANES - N_EXPERTS)).reshape(LANES, 1)
    with_shared = lambda w, s: jnp.pad(w, ((0, 1), (0, 0), (0, 0))).astype(BF16).at[N_EXPERTS].set(s.astype(BF16))
    return {
        'w_main': w_in[:, :DT_OFF].astype(BF16),
        'w_dt': jnp.pad(w_in[:, DT_OFF:], ((0, 0), (0, LANES - SSM_HEADS))),
        'lam_vecs': (row(lq1), row(lk1), row(lq2), row(lk2)),
        'subln_w': row(subln_w),
        'conv_w': conv_w, 'conv_b': row(conv_b),
        'dt_bias': pad8(dt_bias), 'a_log': pad8(a_log),
        'd_skip': row(jnp.repeat(d_skip, SSM_HEAD_DIM)), 'ssm_norm_w': row(ssm_norm_w),
        'w_out': w_out.astype(BF16), 'ln1_g': row(ln1_g), 'ln1_b': row(ln1_b),
        'router_wt': rwt, 'router_b': rb,
        'wg': with_shared(ewg, swg), 'wu': with_shared(ewu, swu), 'wd': with_shared(ewd, swd),
        'ln2_g': row(ln2_g), 'ln2_b': row(ln2_b),
    }


def kernel(x_prompt, x_sample, c_prompt, c_sample, cache_k, cache_v, state_conv, state_ssm, w_ada, b_ada, w_in, lambda_q1, lambda_k1, lambda_q2, lambda_k2, subln_w, conv_w, conv_b, dt_bias, a_log, d_skip, ssm_norm_w, w_out, ln1_g, ln1_b, router_w, router_bias, exp_w_gate, exp_w_up, exp_w_down, sh_w_gate, sh_w_up, sh_w_down, ln2_g, ln2_b):
    depth = w_ada.shape[0]
    bp, sp, _ = x_prompt.shape
    bs, ss, _ = x_sample.shape
    past = cache_k.shape[2]
    pos_prompt = jnp.arange(sp)
    pos_sample = past + jnp.arange(ss)
    c_rows = bp + bs
    c_pad = (-c_rows) % 8
    c_all = jnp.pad(jnp.concatenate([c_prompt, c_sample], 0), ((0, c_pad), (0, 0)))
    yp, ys = x_prompt, x_sample
    outs = [[] for _ in range(8)]
    for l in range(depth):
        prm = _prep_params(w_in[l], lambda_q1[l], lambda_k1[l], lambda_q2[l], lambda_k2[l], subln_w[l],
                           conv_w[l], conv_b[l], dt_bias[l], a_log[l], d_skip[l], ssm_norm_w[l], w_out[l],
                           ln1_g[l], ln1_b[l], router_w[l], router_bias[l], exp_w_gate[l], exp_w_up[l],
                           exp_w_down[l], sh_w_gate[l], sh_w_up[l], sh_w_down[l], ln2_g[l], ln2_b[l])
        mod = _ada(c_all, w_ada[l], b_ada[l]).reshape(c_rows + c_pad, 6, 1, D_MODEL)
        yp, kp, vp, cp, hp = _layer(yp, mod[:bp], pos_prompt, prm, l, depth, None, None, None, None, False)
        ys, kn, vn, cn, hn = _layer(ys, mod[bp:c_rows], pos_sample, prm, l, depth,
                                    cache_k[l].reshape(bs, past * DA_HEADS, LANES),
                                    cache_v[l].reshape(bs, past * DA_HEADS, LANES),
                                    state_conv[l], state_ssm[l], True)
        for lst, val in zip(outs, (kp, vp, cp, hp, kn, vn, cn, hn)):
            lst.append(val)
    return (yp, ys) + tuple(jnp.stack(o) for o in outs)
```

```python
import functools
import math

import jax
import jax.numpy as jnp
from jax import lax
from jax.experimental import pallas as pl
from jax.experimental.pallas import tpu as pltpu

F32 = jnp.float32
BF16 = jnp.bfloat16

D_MODEL = 1024
CHUNK = 64
CHUNK_SHIFT = 6
HEAD_SHIFT = 6
EPS = 1e-5
LOG2E = math.log2(math.e)
DA_HEADS = 4
DA_HEAD_DIM = 64
DA_WIDTH = DA_HEADS * 2 * DA_HEAD_DIM
ROPE_THETA = 10000.0
SSM_HEADS = 8
SSM_HEAD_DIM = 64
SSM_WIDTH = SSM_HEADS * SSM_HEAD_DIM
SSM_GROUPS = 2
SSM_STATE = 128
CONV_W = 4
CONV_DIM = SSM_WIDTH + 2 * SSM_GROUPS * SSM_STATE
Q_OFF = 0
K_OFF = 512
V_OFF = 1024
Z_OFF = 1536
XBC_OFF = 2048
DT_OFF = 3072
N_EXPERTS = 64
N_EXPERT_GROUPS = 8
EXPERTS_PER_GROUP = 8
TOPK_GROUPS = 4
TOP_K = 8
EXPERT_DIM = 256
ROUTED_SCALE = 2.5

LANES = 128
VMEM_LIMIT = 56 * 1024 * 1024


def _cparams(*sem):
    return pltpu.CompilerParams(dimension_semantics=sem, vmem_limit_bytes=VMEM_LIMIT)


def _dot(a, b):
    return jnp.dot(a, b, preferred_element_type=F32)


def _dot_nt(a, b):
    return lax.dot_general(a, b, (((1,), (1,)), ((), ())), preferred_element_type=F32)


def _split2(a):
    hi = a.astype(BF16)
    lo = (a - hi.astype(F32)).astype(BF16)
    return hi, lo


def _split3(a):
    hi = a.astype(BF16)
    r = a - hi.astype(F32)
    mid = r.astype(BF16)
    lo = (r - mid.astype(F32)).astype(BF16)
    return hi, mid, lo


def _dot3(a, b, dot=_dot):
    ah, al = _split2(a)
    bh, bl = _split2(b)
    return dot(ah, bh) + (dot(ah, bl) + dot(al, bh))


def _dot_exact_lhs(e, a, dot=_dot):
    ah, am, al = _split3(a)
    return dot(e, ah) + (dot(e, am) + dot(e, al))


def _silu(x):
    return x * jax.nn.sigmoid(x)


def _softplus(x):
    return jnp.maximum(x, 0.0) + jnp.log1p(jnp.exp(-jnp.abs(x)))


def _ln(x):
    mu = jnp.mean(x, -1, keepdims=True)
    xc = x - mu
    var = jnp.mean(xc * xc, -1, keepdims=True)
    return xc * lax.rsqrt(var + EPS)


def _ada_kernel(c_ref, w_ref, b_ref, o_ref):
    o_ref[...] = _dot3(_silu(c_ref[...]), w_ref[...]) + b_ref[...]


def _ada(c, w_ada, b_ada):
    rows = c.shape[0]
    n = w_ada.shape[1]
    tn = 1024
    return pl.pallas_call(
        _ada_kernel,
        grid=(n // tn,),
        in_specs=[pl.BlockSpec((rows, D_MODEL), lambda i: (0, 0)),
                  pl.BlockSpec((D_MODEL, tn), lambda i: (0, i)),
                  pl.BlockSpec((1, tn), lambda i: (0, i))],
        out_specs=pl.BlockSpec((rows, tn), lambda i: (0, i)),
        out_shape=jax.ShapeDtypeStruct((rows, n), F32),
        compiler_params=_cparams("arbitrary"),
        name="ada",
    )(c, w_ada, b_ada.reshape(1, n))


def _proj_core(x_ref, sc_ref, sh_ref, cos_ref, sin_ref, w_ref, wdt_ref,
               q_ref, k_ref, kb_ref, v_ref, vb_ref, *, bb, ts, v_transposed):
    tm = bb * ts
    h = _ln(x_ref[...]) * (1.0 + sc_ref[...]) + sh_ref[...]
    h = h.reshape(tm, D_MODEL)
    hb = h.astype(BF16)

    cos = jnp.tile(cos_ref[...], (1, DA_WIDTH // LANES))
    sin = jnp.tile(sin_ref[...], (1, DA_WIDTH // LANES))
    lane = lax.broadcasted_iota(jnp.int32, (tm, DA_WIDTH), 1)
    first = (lane & (DA_HEAD_DIM - 1)) < (DA_HEAD_DIM // 2)

    def rope(t):
        rot = jnp.where(first, pltpu.roll(t, DA_WIDTH - DA_HEAD_DIM // 2, 1),
                        pltpu.roll(t, DA_HEAD_DIM // 2, 1))
        return t * cos + rot * sin

    q = rope(_dot(hb, w_ref[:, Q_OFF:K_OFF])) * (DA_HEAD_DIM ** -0.5 * LOG2E)
    q_ref[...] = q.reshape(bb, ts, DA_WIDTH).astype(BF16)
    def store_heads(ref, t):
        for hd in range(DA_HEADS):
            ref[:, pl.ds(hd, ts, stride=DA_HEADS), :] = t[:, :, hd * LANES:(hd + 1) * LANES]

    k = rope(_dot(hb, w_ref[:, K_OFF:V_OFF])).reshape(bb, ts, DA_WIDTH)
    store_heads(k_ref, k)
    kb_ref[...] = k.astype(BF16)
    v = _dot(hb, w_ref[:, V_OFF:Z_OFF]).reshape(bb, ts, DA_WIDTH)
    store_heads(v_ref, v)
    if v_transposed:
        vb_ref[0, 0] = v.reshape(tm, DA_WIDTH).T.astype(BF16)
    else:
        vb_ref[...] = v.astype(BF16)
    z = _dot(hb, w_ref[:, Z_OFF:XBC_OFF])
    xbc = _dot(hb, w_ref[:, XBC_OFF:DT_OFF])
    return z, xbc, _dot3(h, wdt_ref[...])


def _proj_kernel(*refs, bb, ts, v_transposed):
    z_ref, xbc_ref, dt_ref = refs[-3:]
    z, xbc, dt_raw = _proj_core(*refs[:-3], bb=bb, ts=ts, v_transposed=v_transposed)
    z_ref[...] = z.reshape(bb, ts, SSM_WIDTH)
    xbc_ref[...] = xbc.reshape(bb, ts, CONV_DIM)
    dt_ref[...] = dt_raw.reshape(bb, ts, LANES)


def _proj_ssd_kernel(x_ref, sc_ref, sh_ref, cos_ref, sin_ref, w_ref, wdt_ref,
                     cw_ref, cb_ref, dtb_ref, alog_ref, dsk_ref, nw_ref,
                     q_ref, k_ref, kb_ref, v_ref, vb_ref, y_ref, hout_ref, tout_ref, tail_ref, st_ref, *, ts, T):
    @pl.when(pl.program_id(1) == 0)
    def _():
        tail_ref[...] = jnp.zeros_like(tail_ref)
        st_ref[...] = jnp.zeros_like(st_ref)

    z, xbc, dt_raw = _proj_core(x_ref, sc_ref, sh_ref, cos_ref, sin_ref, w_ref, wdt_ref,
                                q_ref, k_ref, kb_ref, v_ref, vb_ref, bb=1, ts=ts, v_transposed=True)
    for c in range(ts // T):
        rows = slice(c * T, (c + 1) * T)
        y = _ssd_chunk(xbc[rows], z[rows], dt_raw[rows], tail_ref, st_ref,
                       cw_ref, cb_ref, dtb_ref, alog_ref, dsk_ref, nw_ref, T=T, n_valid=T)
        y_ref[0, rows, :] = y.astype(BF16)
    @pl.when(pl.program_id(1) == pl.num_programs(1) - 1)
    def _():
        hout_ref[0] = st_ref[...]
        tout_ref[0] = tail_ref[...]


def _proj_specs(B, S, bb, ts, v_transposed, batch_major=False):
    tm = bb * ts
    spec = lambda shape, f: pl.BlockSpec(shape, (lambda b, i: f(i, b)) if batch_major else f)
    row3 = lambda w: spec((bb, ts, w), lambda i, b: (b, i, 0))
    modspec = lambda which: spec((bb, None, 1, D_MODEL), lambda i, b: (b, which, 0, 0))
    if v_transposed:
        vb_shape = jax.ShapeDtypeStruct((B, S // ts, DA_WIDTH, ts), BF16)
        vb_spec = spec((1, 1, DA_WIDTH, ts), lambda i, b: (b, i, 0, 0))
    else:
        vb_shape = jax.ShapeDtypeStruct((B, S, DA_WIDTH), BF16)
        vb_spec = row3(DA_WIDTH)
    heads_shape = jax.ShapeDtypeStruct((B, S * DA_HEADS, LANES), F32)
    heads_spec = spec((bb, ts * DA_HEADS, LANES), lambda i, b: (b, i, 0))
    in_specs = [row3(D_MODEL), modspec(1), modspec(0),
                spec((tm, LANES), lambda i, b: (i, 0)),
                spec((tm, LANES), lambda i, b: (i, 0)),
                spec((D_MODEL, DT_OFF), lambda i, b: (0, 0)),
                spec((D_MODEL, LANES), lambda i, b: (0, 0))]
    out_shapes = [jax.ShapeDtypeStruct((B, S, DA_WIDTH), BF16),
                  heads_shape,
                  jax.ShapeDtypeStruct((B, S, DA_WIDTH), BF16),
                  heads_shape,
                  vb_shape]
    out_specs = [row3(DA_WIDTH), heads_spec, row3(DA_WIDTH), heads_spec, vb_spec]
    return row3, in_specs, out_shapes, out_specs


def _proj(x, mod4, cos_t, sin_t, w_main, w_dt, *, bb, ts, v_transposed):
    B, S, _ = x.shape
    row3, in_specs, out_shapes, out_specs = _proj_specs(B, S, bb, ts, v_transposed)
    out_shapes += [jax.ShapeDtypeStruct((B, S, SSM_WIDTH), F32),
                   jax.ShapeDtypeStruct((B, S, CONV_DIM), F32),
                   jax.ShapeDtypeStruct((B, S, LANES), F32)]
    out_specs += [row3(SSM_WIDTH), row3(CONV_DIM), row3(LANES)]
    return pl.pallas_call(
        functools.partial(_proj_kernel, bb=bb, ts=ts, v_transposed=v_transposed),
        grid=(S // ts, B // bb),
        in_specs=in_specs,
        out_specs=out_specs,
        out_shape=out_shapes,
        compiler_params=_cparams("arbitrary", "arbitrary"),
        name="proj",
    )(x, mod4, mod4, cos_t, sin_t, w_main, w_dt)


def _proj_ssd(x, mod4, cos_t, sin_t, w_main, w_dt, conv_w, conv_b, dtb, alog, dsk, nw, *, ts, T):
    B, S, _ = x.shape
    row3, in_specs, out_shapes, out_specs = _proj_specs(B, S, 1, ts, True, batch_major=True)
    const = lambda r, w: pl.BlockSpec((r, w), lambda b, i: (0, 0))
    per_b = lambda r, w: pl.BlockSpec((1, r, w), lambda b, i: (b, 0, 0))
    in_specs += [const(CONV_W, CONV_DIM), const(1, CONV_DIM), const(1, LANES), const(1, LANES),
                 const(1, SSM_WIDTH), const(1, SSM_WIDTH)]
    out_shapes += [jax.ShapeDtypeStruct((B, S, SSM_WIDTH), BF16),
                   jax.ShapeDtypeStruct((B, SSM_WIDTH, SSM_STATE), F32),
                   jax.ShapeDtypeStruct((B, 8, CONV_DIM), F32)]
    out_specs += [row3(SSM_WIDTH), per_b(SSM_WIDTH, SSM_STATE), per_b(8, CONV_DIM)]
    return pl.pallas_call(
        functools.partial(_proj_ssd_kernel, ts=ts, T=T),
        grid=(B, S // ts),
        in_specs=in_specs,
        out_specs=out_specs,
        out_shape=out_shapes,
        scratch_shapes=[pltpu.VMEM((8, CONV_DIM), F32), pltpu.VMEM((SSM_WIDTH, SSM_STATE), F32)],
        compiler_params=_cparams("arbitrary", "arbitrary"),
        name="proj_ssd",
    )(x, mod4, mod4, cos_t, sin_t, w_main, w_dt, conv_w, conv_b, dtb, alog, dsk, nw)


def _lambda(lq1, lk1, lq2, lk2, lam_init):
    s1 = jnp.sum(lq1[...] * lk1[...], axis=1, keepdims=True)
    s2 = jnp.sum(lq2[...] * lk2[...], axis=1, keepdims=True)
    return jnp.exp(s1) - jnp.exp(s2) + lam_init


def _subln(o, subw, lam_init):
    ms = jnp.mean(o * o, -1, keepdims=True)
    return o * lax.rsqrt(ms + EPS) * subw * (1.0 - lam_init)


def _attn_prompt_kernel(lq1, lk1, lq2, lk2, subw_ref, q_ref, k_ref, vt_ref, o_ref,
                        acc1, acc2, sa1, sa2, sb1, sb2, mxa, mxb, m1, l1, m2, l2, *, tq, tk, lam_init):
    i = pl.program_id(2)
    qt = q_ref[0].astype(F32).T
    row = lax.broadcasted_iota(jnp.int32, (LANES, tq), 0)
    q1t = jnp.where(row < DA_HEAD_DIM, qt, 0.0).astype(BF16)
    q2t = jnp.where(row >= DA_HEAD_DIM, qt, 0.0).astype(BF16)
    acc1[...] = jnp.zeros_like(acc1)
    acc2[...] = jnp.zeros_like(acc2)
    m1[...] = jnp.full_like(m1, -jnp.inf)
    m2[...] = jnp.full_like(m2, -jnp.inf)
    l1[...] = jnp.zeros_like(l1)
    l2[...] = jnp.zeros_like(l2)
    buf_a = (sa1, sa2, mxa)
    buf_b = (sb1, sb2, mxb)

    def produce(j, buf):
        k = k_ref[0, pl.ds(pl.multiple_of(j * tk, tk), tk), :]
        for half, qh in enumerate((q1t, q2t)):
            s = _dot(k, qh)
            buf[half][...] = s
            buf[2][half:half + 1, :] = jnp.max(s, axis=0, keepdims=True)

    def consume(j, buf, masked):
        vt = vt_ref[0, j]
        if masked:
            key = (j * tk + lax.broadcasted_iota(jnp.int32, (tk, tq), 0)) >> CHUNK_SHIFT
            qry = (i * tq + lax.broadcasted_iota(jnp.int32, (tk, tq), 1)) >> CHUNK_SHIFT
            keep = key <= qry
        for half, (m, l, acc) in enumerate(((m1, l1, acc1), (m2, l2, acc2))):
            st = buf[half][...]
            if masked:
                st = jnp.where(keep, st, -jnp.inf)
                tile_max = jnp.max(st, axis=0, keepdims=True)
            else:
                tile_max = buf[2][half:half + 1, :]
            mo = m[...]
            mn = jnp.maximum(mo, tile_max)
            a = jnp.exp2(mo - mn)
            p = jnp.exp2(st - mn)
            l[...] = a * l[...] + jnp.sum(p, axis=0, keepdims=True)
            m[...] = mn
            acc[...] = a * acc[...] + _dot(vt, p.astype(BF16))

    nfull = (i * tq) // tk
    produce(nfull, buf_a)
    produce(0, buf_b)
    consume(nfull, buf_a, True)

    def pair(jj, carry):
        produce(2 * jj + 1, buf_a)
        consume(2 * jj, buf_b, False)
        produce(2 * jj + 2, buf_b)
        consume(2 * jj + 1, buf_a, False)
        return carry

    lax.fori_loop(0, nfull >> 1, pair, 0)

    @pl.when((nfull & 1) == 1)
    def _():
        consume(nfull - 1, buf_b, False)

    lam = _lambda(lq1, lk1, lq2, lk2, lam_init)
    ot = acc1[...] / l1[...] - lam * (acc2[...] / l2[...])
    ot = ot * lax.rsqrt(jnp.mean(ot * ot, axis=0, keepdims=True) + EPS)
    o_ref[0] = (ot.T * subw_ref[...] * (1.0 - lam_init)).astype(BF16)


def _attn_prompt(qb, kb, vt, lam_vecs, subw, lam_init, *, tq):
    B, S, _ = qb.shape
    nkt, tk = vt.shape[1], vt.shape[3]
    small = pl.BlockSpec((1, DA_HEAD_DIM), lambda b, h, i: (0, 0))
    return pl.pallas_call(
        functools.partial(_attn_prompt_kernel, tq=tq, tk=tk, lam_init=lam_init),
        grid=(B, DA_HEADS, S // tq),
        in_specs=[small, small, small, small,
                  pl.BlockSpec((1, LANES), lambda b, h, i: (0, 0)),
                  pl.BlockSpec((1, tq, LANES), lambda b, h, i: (b, i, h)),
                  pl.BlockSpec((1, S, LANES), lambda b, h, i: (b, 0, h)),
                  pl.BlockSpec((1, nkt, LANES, tk), lambda b, h, i: (b, 0, h, 0))],
        out_specs=pl.BlockSpec((1, tq, LANES), lambda b, h, i: (b, i, h)),
        out_shape=jax.ShapeDtypeStruct((B, S, DA_WIDTH), BF16),
        scratch_shapes=([pltpu.VMEM((LANES, tq), F32)] * 2 + [pltpu.VMEM((tk, tq), F32)] * 4
                        + [pltpu.VMEM((2, tq), F32)] * 2 + [pltpu.VMEM((1, tq), F32)] * 4),
        compiler_params=_cparams("arbitrary", "arbitrary", "arbitrary"),
        name="attn_prompt",
    )(*lam_vecs, subw, qb, kb, vt)


def _attn_sample_kernel(lq1, lk1, lq2, lk2, subw_ref, q_ref, kn_ref, vn_ref, kp_ref, vp_ref, o_ref,
                        *, lam_init):
    s = q_ref.shape[1]
    lam = _lambda(lq1, lk1, lq2, lk2, lam_init)
    lane = lax.broadcasted_iota(jnp.int32, (s, LANES), 1)
    outs = []
    for h in range(DA_HEADS):
        cols = slice(h * LANES, (h + 1) * LANES)
        q = q_ref[0, :, cols]
        zero = jnp.zeros_like(q)
        kn = kn_ref[0, :, cols]
        vn = vn_ref[0, :, cols]
        past = kp_ref.shape[1] // DA_HEADS
        kp = kp_ref[0, pl.ds(h, past, stride=DA_HEADS), :].astype(BF16)
        vp = vp_ref[0, pl.ds(h, past, stride=DA_HEADS), :].astype(BF16)
        probs = []
        for qm in (jnp.where(lane < DA_HEAD_DIM, q, zero), jnp.where(lane >= DA_HEAD_DIM, q, zero)):
            sp = _dot_nt(qm, kp)
            sn = _dot_nt(qm, kn)
            m = jnp.maximum(jnp.max(sp, -1, keepdims=True), jnp.max(sn, -1, keepdims=True))
            pp = jnp.exp2(sp - m)
            pn = jnp.exp2(sn - m)
            l = jnp.sum(pp, -1, keepdims=True) + jnp.sum(pn, -1, keepdims=True)
            probs.append((pp / l, pn / l))
        ap = probs[0][0] - lam * probs[1][0]
        an = probs[0][1] - lam * probs[1][1]
        o = _dot(ap.astype(BF16), vp) + _dot(an.astype(BF16), vn)
        outs.append(_subln(o, subw_ref[...], lam_init))
    o_ref[0] = jnp.concatenate(outs, axis=-1).astype(BF16)


def _attn_sample(qb, kb, vb, k_past, v_past, lam_vecs, subw, lam_init):
    B, s, _ = qb.shape
    past_rows = k_past.shape[1]
    small = pl.BlockSpec((1, DA_HEAD_DIM), lambda b: (0, 0))
    new = pl.BlockSpec((1, s, DA_WIDTH), lambda b: (b, 0, 0))
    old = pl.BlockSpec((1, past_rows, LANES), lambda b: (b, 0, 0))
    return pl.pallas_call(
        functools.partial(_attn_sample_kernel, lam_init=lam_init),
        grid=(B,),
        in_specs=[small, small, small, small, pl.BlockSpec((1, LANES), lambda b: (0, 0)),
                  new, new, new, old, old],
        out_specs=new,
        out_shape=jax.ShapeDtypeStruct((B, s, DA_WIDTH), BF16),
        compiler_params=_cparams("arbitrary"),
        name="attn_sample",
    )(*lam_vecs, subw, qb, kb, vb, k_past, v_past)


def _ssd_kernel(xbc_ref, z_ref, dt_ref, cp_ref, h0_ref, cw_ref, cb_ref, dtb_ref, alog_ref, dsk_ref, nw_ref,
                y_ref, hout_ref, tail_ref, st_ref, *, t_in, T):
    j = pl.program_id(1)

    @pl.when(j == 0)
    def _():
        tail_ref[...] = cp_ref[0]
        st_ref[...] = h0_ref[0]

    def rows(ref, width):
        val = ref[0]
        if T == t_in:
            return val
        return jnp.concatenate([val, jnp.zeros((T - t_in, width), F32)], axis=0)

    y = _ssd_chunk(rows(xbc_ref, CONV_DIM), rows(z_ref, SSM_WIDTH), rows(dt_ref, LANES), tail_ref, st_ref,
                   cw_ref, cb_ref, dtb_ref, alog_ref, dsk_ref, nw_ref, T=T, n_valid=t_in)
    y_ref[0] = y[:t_in, :].astype(BF16)

    @pl.when(j == pl.num_programs(1) - 1)
    def _():
        hout_ref[0] = st_ref[...]


def _ssd_chunk(xb, zb, dt_raw, tail_ref, st_ref, cw_ref, cb_ref, dtb_ref, alog_ref, dsk_ref, nw_ref, *, T, n_valid):
    PAIR = 2 * SSM_HEAD_DIM
    xp = jnp.concatenate([tail_ref[...], xb], axis=0)
    conv = cb_ref[...] + cw_ref[CONV_W - 1:CONV_W, :] * xb
    for sft in range(1, CONV_W):
        conv = conv + cw_ref[CONV_W - 1 - sft:CONV_W - sft, :] * pltpu.roll(xp, sft, 0)[8:, :]
    if T == n_valid:
        tail_ref[...] = xb[T - 8:, :]
    xa = _silu(conv)
    xs = xa[:, :SSM_WIDTH]

    dt = _softplus(dt_raw + dtb_ref[...])
    if T != n_valid:
        valid = lax.broadcasted_iota(jnp.int32, (T, LANES), 0) < n_valid
        dt = jnp.where(valid, dt, 0.0)
    da = dt * (-jnp.exp(alog_ref[...]))

    rr = lax.broadcasted_iota(jnp.int32, (T, T), 0)
    cc = lax.broadcasted_iota(jnp.int32, (T, T), 1)
    causal = cc <= rr
    tril = jnp.where(causal, 1.0, 0.0).astype(BF16)
    acs = _dot_exact_lhs(tril, da)
    e16 = jnp.where(lax.broadcasted_iota(jnp.int32, (16, LANES), 0)
                    == lax.broadcasted_iota(jnp.int32, (16, LANES), 1), 1.0, 0.0).astype(BF16)
    acs_t = _dot_exact_lhs(e16, acs, dot=_dot_nt)

    last = acs[T - 1:T, :]
    expand = jnp.where((lax.broadcasted_iota(jnp.int32, (LANES, SSM_WIDTH), 1) >> HEAD_SHIFT)
                       == lax.broadcasted_iota(jnp.int32, (LANES, SSM_WIDTH), 0), 1.0, 0.0).astype(BF16)

    def per_head_lanes(v):
        hi, lo = _split2(v)
        return _dot(hi, expand) + _dot(lo, expand)

    xdt = xs * per_head_lanes(dt)
    xdd = xdt * per_head_lanes(jnp.exp(last - acs))
    eacs = per_head_lanes(jnp.exp(acs))

    head_of_row = lax.broadcasted_iota(jnp.int32, (SSM_WIDTH, LANES), 0) >> HEAD_SHIFT
    sel = head_of_row == lax.broadcasted_iota(jnp.int32, (SSM_WIDTH, LANES), 1)
    rowdec = jnp.sum(jnp.where(sel, jnp.exp(last), 0.0), axis=1, keepdims=True)

    lane = lax.broadcasted_iota(jnp.int32, (T, PAIR), 1)
    ys = []
    for p in range(SSM_HEADS // 2):
        g = (2 * p) // (SSM_HEADS // SSM_GROUPS)
        bm = xa[:, SSM_WIDTH + g * SSM_STATE:SSM_WIDTH + (g + 1) * SSM_STATE].astype(BF16)
        cm = xa[:, SSM_WIDTH + (SSM_GROUPS + g) * SSM_STATE:
                SSM_WIDTH + (SSM_GROUPS + g + 1) * SSM_STATE].astype(BF16)
        cb = _dot_nt(cm, bm)
        cols = slice(p * PAIR, (p + 1) * PAIR)
        xdt_p = xdt[:, cols].astype(BF16)
        yd = []
        for hh in (2 * p, 2 * p + 1):
            seg = acs[:, hh:hh + 1] - acs_t[hh:hh + 1, :]
            lmat = jnp.exp(jnp.where(causal, seg, -jnp.inf))
            yd.append(_dot((cb * lmat).astype(BF16), xdt_p))
        y_diag = jnp.where(lane < SSM_HEAD_DIM, yd[0], yd[1])
        st = st_ref[cols, :]
        y_off = _dot_nt(cm, st.astype(BF16)) * eacs[:, cols]
        ys.append(y_diag + y_off)
        st_ref[cols, :] = st * rowdec[cols, :] + _dot(xdd[:, cols].T.astype(BF16), bm)
    y = jnp.concatenate(ys, axis=-1) + xs * dsk_ref[...]
    y = y * _silu(zb)
    half = SSM_WIDTH // SSM_GROUPS
    outs = []
    for g in range(SSM_GROUPS):
        yg = y[:, g * half:(g + 1) * half]
        outs.append(yg * lax.rsqrt(jnp.mean(yg * yg, -1, keepdims=True) + EPS))
    return jnp.concatenate(outs, axis=-1) * nw_ref[...]


def _ssd(xbc, z, dt_raw, conv_prev8, h0, conv_w, conv_b, dtb, alog, dsk, nw, *, t_in, T):
    B, S, _ = xbc.shape
    row = lambda w: pl.BlockSpec((1, t_in, w), lambda b, j: (b, j, 0))
    per_b = lambda r, w: pl.BlockSpec((1, r, w), lambda b, j: (b, 0, 0))
    const = lambda r, w: pl.BlockSpec((r, w), lambda b, j: (0, 0))
    return pl.pallas_call(
        functools.partial(_ssd_kernel, t_in=t_in, T=T),
        grid=(B, S // t_in),
        in_specs=[row(CONV_DIM), row(SSM_WIDTH), row(LANES),
                  per_b(8, CONV_DIM), per_b(SSM_WIDTH, SSM_STATE),
                  const(CONV_W, CONV_DIM), const(1, CONV_DIM), const(1, LANES), const(1, LANES),
                  const(1, SSM_WIDTH), const(1, SSM_WIDTH)],
        out_specs=[row(SSM_WIDTH), per_b(SSM_WIDTH, SSM_STATE)],
        out_shape=[jax.ShapeDtypeStruct((B, S, SSM_WIDTH), BF16),
                   jax.ShapeDtypeStruct((B, SSM_WIDTH, SSM_STATE), F32)],
        scratch_shapes=[pltpu.VMEM((8, CONV_DIM), F32), pltpu.VMEM((SSM_WIDTH, SSM_STATE), F32)],
        compiler_params=_cparams("arbitrary", "arbitrary"),
        name="ssd",
    )(xbc, z, dt_raw, conv_prev8, h0, conv_w, conv_b, dtb, alog, dsk, nw)


def _router(h, rwt_ref, rb_ref):
    tm = h.shape[0]
    G = N_EXPERT_GROUPS
    logits = _dot3(rwt_ref[...], h, dot=_dot_nt)
    score = jax.nn.sigmoid(logits[:N_EXPERTS, :])
    biased = score + rb_ref[:N_EXPERTS, :]
    sc = [score[G * j:G * (j + 1), :] for j in range(EXPERTS_PER_GROUP)]
    v = [biased[G * j:G * (j + 1), :] for j in range(EXPERTS_PER_GROUP)]

    def tree(op, xs):
        while len(xs) > 1:
            xs = [op(xs[a], xs[a + 1]) for a in range(0, len(xs) - 1, 2)] + ([xs[-1]] if len(xs) % 2 else [])
        return xs[0]

    m1 = tree(jnp.maximum, v)
    first = tree(jnp.minimum, [jnp.where(v[j] == m1, float(j), float(EXPERTS_PER_GROUP))
                               for j in range(EXPERTS_PER_GROUP)])
    m2 = tree(jnp.maximum, [jnp.where(first == float(j), -jnp.inf, v[j]) for j in range(EXPERTS_PER_GROUP)])
    grp = m1 + m2

    gid = lax.broadcasted_iota(jnp.int32, (G, tm), 0)
    cnt = jnp.zeros((G, tm), jnp.int32)
    for g2 in range(G):
        row = grp[g2:g2 + 1, :]
        beats = (row > grp) | ((row == grp) & (gid > g2))
        cnt = cnt + beats.astype(jnp.int32)
    gmask = cnt < TOPK_GROUPS

    idx = [(gid * EXPERTS_PER_GROUP + j).astype(F32) for j in range(EXPERTS_PER_GROUP)]
    mv = [jnp.where(gmask, v[j], -jnp.inf) for j in range(EXPERTS_PER_GROUP)]
    w = [jnp.zeros((G, tm), F32) for _ in range(EXPERTS_PER_GROUP)]
    for _ in range(TOP_K):
        top = jnp.max(tree(jnp.maximum, mv), axis=0, keepdims=True)
        who = jnp.min(tree(jnp.minimum, [jnp.where(mv[j] == top, idx[j], float(N_EXPERTS))
                                         for j in range(EXPERTS_PER_GROUP)]), axis=0, keepdims=True)
        for j in range(EXPERTS_PER_GROUP):
            hit = idx[j] == who
            w[j] = jnp.where(hit, sc[j], w[j])
            mv[j] = jnp.where(hit, -jnp.inf, mv[j])
    tot = w[0]
    for j in range(1, EXPERTS_PER_GROUP):
        tot = tot + w[j]
    tot = jnp.sum(tot, axis=0, keepdims=True)
    gates = [w[j] / tot * ROUTED_SCALE for j in range(EXPERTS_PER_GROUP)]
    shared = jnp.where(lax.broadcasted_iota(jnp.int32, (LANES - N_EXPERTS, tm), 0) == 0, 1.0, 0.0)
    by_slot = jnp.concatenate(gates + [shared], axis=0)
    e_id = lax.broadcasted_iota(jnp.int32, (LANES, LANES), 0)
    r_id = lax.broadcasted_iota(jnp.int32, (LANES, LANES), 1)
    src_row = jnp.where(e_id < N_EXPERTS, (e_id & (G - 1)) * EXPERTS_PER_GROUP + (e_id >> 3), e_id)
    perm = jnp.where(r_id == src_row, 1.0, 0.0).astype(BF16)
    return _dot_exact_lhs(perm, by_slot)


def _mix_kernel(x_ref, att_ref, ssm_ref, g1_ref, sc2_ref, sh2_ref, wo_ref, l1g_ref, l1b_ref, rwt_ref, rb_ref,
                x1_ref, h2_ref, gate_ref, *, bb, ts, alpha):
    tm = bb * ts
    att = att_ref[...].reshape(tm, DA_WIDTH)
    ssm = ssm_ref[...].reshape(tm, SSM_WIDTH)
    mix = _dot(att, wo_ref[:DA_WIDTH, :]) + _dot(ssm, wo_ref[DA_WIDTH:, :])
    y = alpha * x_ref[...] + (1.0 + g1_ref[...]) * mix.reshape(bb, ts, D_MODEL)
    x1 = _ln(y) * l1g_ref[...] + l1b_ref[...]
    x1_ref[...] = x1
    h2 = (_ln(x1) * (1.0 + sc2_ref[...]) + sh2_ref[...]).reshape(tm, D_MODEL)
    h2_ref[...] = h2.astype(BF16)
    gate_ref[...] = _router(h2, rwt_ref, rb_ref).T


def _mix(x, att, ssm, mod4, wo, l1g, l1b, rwt, rb, *, bb, ts, alpha):
    B, S, _ = x.shape
    tm = bb * ts
    row3 = lambda w: pl.BlockSpec((bb, ts, w), lambda i, b: (b, i, 0))
    modspec = lambda which: pl.BlockSpec((bb, None, 1, D_MODEL), lambda i, b: (b, which, 0, 0))
    const = lambda r, w: pl.BlockSpec((r, w), lambda i, b: (0, 0))
    flat = lambda w: pl.BlockSpec((tm, w), lambda i, b: (b * (S // ts) + i, 0))
    return pl.pallas_call(
        functools.partial(_mix_kernel, bb=bb, ts=ts, alpha=alpha),
        grid=(S // ts, B // bb),
        in_specs=[row3(D_MODEL), row3(DA_WIDTH), row3(SSM_WIDTH), modspec(2), modspec(4), modspec(3),
                  const(D_MODEL, D_MODEL), const(1, D_MODEL), const(1, D_MODEL),
                  const(LANES, D_MODEL), const(LANES, 1)],
        out_specs=[row3(D_MODEL), flat(D_MODEL), flat(LANES)],
        out_shape=[jax.ShapeDtypeStruct((B, S, D_MODEL), F32),
                   jax.ShapeDtypeStruct((B * S, D_MODEL), BF16),
                   jax.ShapeDtypeStruct((B * S, LANES), F32)],
        compiler_params=_cparams("arbitrary", "arbitrary"),
        name="mix",
    )(x, att, ssm, mod4, mod4, mod4, wo, l1g, l1b, rwt, rb)


N_EXPERTS_ALL = N_EXPERTS + 1
EXPERTS_PER_STEP = 5


def _moe_kernel(h_ref, gate_ref, x1_ref, g2_ref, wg_ref, wu_ref, wd_ref,
                l2g_ref, l2b_ref, o_ref, acc_ref, *, bb, ts, alpha):
    tm = bb * ts
    e = pl.program_id(2)
    h = h_ref[...]

    @pl.when(e == 0)
    def _():
        acc_ref[...] = jnp.zeros_like(acc_ref)

    gate = gate_ref[...]
    lane = lax.broadcasted_iota(jnp.int32, (tm, LANES), 1)
    hids = []
    for s in range(EXPERTS_PER_STEP):
        col = e * EXPERTS_PER_STEP + s
        gcol = jnp.sum(jnp.where(lane == col, gate, 0.0), axis=1, keepdims=True)
        hid = _silu(_dot(h, wg_ref[s])) * _dot(h, wu_ref[s]) * gcol
        hids.append(hid.astype(BF16))
    hid = jnp.concatenate(hids, axis=-1)
    acc_ref[...] += _dot(hid, wd_ref[...].reshape(EXPERTS_PER_STEP * EXPERT_DIM, D_MODEL))

    @pl.when(e == pl.num_programs(2) - 1)
    def _():
        ffn = acc_ref[...].reshape(bb, ts, D_MODEL)
        y = alpha * x1_ref[...] + (1.0 + g2_ref[...]) * ffn
        o_ref[...] = _ln(y) * l2g_ref[...] + l2b_ref[...]


def _moe(h2, gate, x1, mod4, wg, wu, wd, l2g, l2b, *, bb, ts, alpha):
    B, S, _ = x1.shape
    tm = bb * ts
    row3 = lambda w: pl.BlockSpec((bb, ts, w), lambda i, b, e: (b, i, 0))
    flat = lambda w: pl.BlockSpec((tm, w), lambda i, b, e: (b * (S // ts) + i, 0))
    const = lambda r, w: pl.BlockSpec((r, w), lambda i, b, e: (0, 0))
    ew_in = pl.BlockSpec((EXPERTS_PER_STEP, D_MODEL, EXPERT_DIM), lambda i, b, e: (e, 0, 0))
    ew_out = pl.BlockSpec((EXPERTS_PER_STEP, EXPERT_DIM, D_MODEL), lambda i, b, e: (e, 0, 0))
    return pl.pallas_call(
        functools.partial(_moe_kernel, bb=bb, ts=ts, alpha=alpha),
        grid=(S // ts, B // bb, N_EXPERTS_ALL // EXPERTS_PER_STEP),
        in_specs=[flat(D_MODEL), flat(LANES), row3(D_MODEL),
                  pl.BlockSpec((bb, None, 1, D_MODEL), lambda i, b, e: (b, 5, 0, 0)),
                  ew_in, ew_in, ew_out,
                  const(1, D_MODEL), const(1, D_MODEL)],
        out_specs=row3(D_MODEL),
        out_shape=jax.ShapeDtypeStruct((B, S, D_MODEL), F32),
        scratch_shapes=[pltpu.VMEM((tm, D_MODEL), F32)],
        compiler_params=_cparams("arbitrary", "arbitrary", "arbitrary"),
        name="moe",
    )(h2, gate, x1, mod4, wg, wu, wd, l2g, l2b)


def _rope_tables(pos):
    half = DA_HEAD_DIM // 2
    inv = ROPE_THETA ** (-jnp.arange(half, dtype=F32) / half)
    ang = pos.astype(F32)[:, None] * inv[None, :]
    cos = jnp.cos(ang)
    sin = jnp.sin(ang)
    reps = LANES // DA_HEAD_DIM
    cos_t = jnp.tile(jnp.concatenate([cos, cos], -1), (1, reps))
    sin_t = jnp.tile(jnp.concatenate([-sin, sin], -1), (1, reps))
    return cos_t, sin_t


def _layer(x, mod4, pos, prm, layer_idx, depth, k_past, v_past, conv_prev, ssm_prev, sample):
    B, S, _ = x.shape
    alpha = (2 * depth) ** 0.25
    lam_init = 0.8 - 0.6 * math.exp(-0.3 * layer_idx)
    cos_t, sin_t = _rope_tables(pos)
    if sample:
        bb, ts = B, S
        cos_t = jnp.tile(cos_t, (B, 1))
        sin_t = jnp.tile(sin_t, (B, 1))
    else:
        bb, ts = 1, min(512, S)
    ssd_consts = (prm['conv_w'], prm['conv_b'], prm['dt_bias'], prm['a_log'], prm['d_skip'], prm['ssm_norm_w'])
    if sample:
        qb, k, kb, v, vb, z, xbc, dt_raw = _proj(x, mod4, cos_t, sin_t, prm['w_main'], prm['w_dt'],
                                                 bb=bb, ts=ts, v_transposed=False)
        att = _attn_sample(qb, kb, vb, k_past, v_past, prm['lam_vecs'], prm['subln_w'], lam_init)
        conv_prev8 = jnp.pad(conv_prev, ((0, 0), (8 - (CONV_W - 1), 0), (0, 0)))
        h0 = ssm_prev.reshape(B, SSM_WIDTH, SSM_STATE)
        ssm, h_last = _ssd(xbc, z, dt_raw, conv_prev8, h0, *ssd_consts, t_in=S, T=max(S, LANES))
        new_conv = jnp.concatenate([conv_prev, xbc], axis=1)[:, -(CONV_W - 1):]
    else:
        qb, k, kb, v, vb, ssm, h_last, tail = _proj_ssd(x, mod4, cos_t, sin_t, prm['w_main'], prm['w_dt'],
                                                        *ssd_consts, ts=ts, T=min(256, S))
        att = _attn_prompt(qb, kb, vb, prm['lam_vecs'], prm['subln_w'], lam_init, tq=min(512, S))
        new_conv = tail[:, -(CONV_W - 1):]
    x1, h2, gate = _mix(x, att, ssm, mod4, prm['w_out'], prm['ln1_g'], prm['ln1_b'], prm['router_wt'],
                        prm['router_b'], bb=bb, ts=ts, alpha=alpha)
    if not sample:
        ts = min(1024, S)
    y = _moe(h2, gate, x1, mod4, prm['wg'], prm['wu'], prm['wd'],
             prm['ln2_g'], prm['ln2_b'], bb=bb, ts=ts, alpha=alpha)
    return (y, k.reshape(B, S, DA_HEADS, 2 * DA_HEAD_DIM), v.reshape(B, S, DA_HEADS, 2 * DA_HEAD_DIM),
            new_conv, h_last.reshape(B, SSM_HEADS, SSM_HEAD_DIM, SSM_STATE))


CAST_EXPERTS = 8


def _cast_kernel(w_ref, s_ref, o_ref):
    last = pl.num_programs(0) - 1

    @pl.when(pl.program_id(0) < last)
    def _():
        o_ref[...] = w_ref[...].astype(BF16)

    @pl.when(pl.program_id(0) == last)
    def _():
        o_ref[0] = s_ref[...].astype(BF16)


def _cast_with_shared(w, s):
    n, r, c = w.shape
    full = n // CAST_EXPERTS
    return pl.pallas_call(
        _cast_kernel,
        grid=(full + 1,),
        in_specs=[pl.BlockSpec((CAST_EXPERTS, r, c), lambda e: (jnp.minimum(e, full - 1), 0, 0)),
                  pl.BlockSpec((r, c), lambda e: (0, 0))],
        out_specs=pl.BlockSpec((CAST_EXPERTS, r, c), lambda e: (e, 0, 0)),
        out_shape=jax.ShapeDtypeStruct((n + 1, r, c), BF16),
        compiler_params=_cparams("arbitrary"),
        name="cast_experts",
    )(w, s)


def _prep_params(w_in, lq1, lk1, lq2, lk2, subln_w, conv_w, conv_b, dt_bias, a_log, d_skip, ssm_norm_w, w_out,
                 ln1_g, ln1_b, router_w, router_bias, ewg, ewu, ewd, swg, swu, swd, ln2_g, ln2_b):
    pad8 = lambda a: jnp.pad(a.reshape(1, -1), ((0, 0), (0, LANES - a.shape[-1])))
    row = lambda a: a.reshape(1, -1)
    perm = (jnp.arange(N_EXPERTS) % N_EXPERT_GROUPS) * EXPERTS_PER_GROUP + jnp.arange(N_EXPERTS) // N_EXPERT_GROUPS
    rwt = jnp.pad(router_w.T[perm], ((0, LANES - N_EXPERTS), (0, 0)))
    rb = jnp.pad(router_bias[perm], (0, LANES - N_EXPERTS)).reshape(LANES, 1)
    with_shared = _cast_with_shared
    return {
        'w_main': w_in[:, :DT_OFF].astype(BF16),
        'w_dt': jnp.pad(w_in[:, DT_OFF:], ((0, 0), (0, LANES - SSM_HEADS))),
        'lam_vecs': (row(lq1), row(lk1), row(lq2), row(lk2)),
        'subln_w': row(subln_w),
        'conv_w': conv_w, 'conv_b': row(conv_b),
        'dt_bias': pad8(dt_bias), 'a_log': pad8(a_log),
        'd_skip': row(jnp.repeat(d_skip, SSM_HEAD_DIM)), 'ssm_norm_w': row(ssm_norm_w),
        'w_out': w_out.astype(BF16), 'ln1_g': row(ln1_g), 'ln1_b': row(ln1_b),
        'router_wt': rwt, 'router_b': rb,
        'wg': with_shared(ewg, swg), 'wu': with_shared(ewu, swu), 'wd': with_shared(ewd, swd),
        'ln2_g': row(ln2_g), 'ln2_b': row(ln2_b),
    }


def kernel(x_prompt, x_sample, c_prompt, c_sample, cache_k, cache_v, state_conv, state_ssm, w_ada, b_ada, w_in, lambda_q1, lambda_k1, lambda_q2, lambda_k2, subln_w, conv_w, conv_b, dt_bias, a_log, d_skip, ssm_norm_w, w_out, ln1_g, ln1_b, router_w, router_bias, exp_w_gate, exp_w_up, exp_w_down, sh_w_gate, sh_w_up, sh_w_down, ln2_g, ln2_b):
    depth = w_ada.shape[0]
    bp, sp, _ = x_prompt.shape
    bs, ss, _ = x_sample.shape
    past = cache_k.shape[2]
    pos_prompt = jnp.arange(sp)
    pos_sample = past + jnp.arange(ss)
    c_rows = bp + bs
    c_pad = (-c_rows) % 8
    c_all = jnp.pad(jnp.concatenate([c_prompt, c_sample], 0), ((0, c_pad), (0, 0)))
    yp, ys = x_prompt, x_sample
    outs = [[] for _ in range(8)]
    for l in range(depth):
        prm = _prep_params(w_in[l], lambda_q1[l], lambda_k1[l], lambda_q2[l], lambda_k2[l], subln_w[l],
                           conv_w[l], conv_b[l], dt_bias[l], a_log[l], d_skip[l], ssm_norm_w[l], w_out[l],
                           ln1_g[l], ln1_b[l], router_w[l], router_bias[l], exp_w_gate[l], exp_w_up[l],
                           exp_w_down[l], sh_w_gate[l], sh_w_up[l], sh_w_down[l], ln2_g[l], ln2_b[l])
        mod = _ada(c_all, w_ada[l], b_ada[l]).reshape(c_rows + c_pad, 6, 1, D_MODEL)
        yp, kp, vp, cp, hp = _layer(yp, mod[:bp], pos_prompt, prm, l, depth, None, None, None, None, False)
        ys, kn, vn, cn, hn = _layer(ys, mod[bp:c_rows], pos_sample, prm, l, depth,
                                    cache_k[l].reshape(bs, past * DA_HEADS, LANES),
                                    cache_v[l].reshape(bs, past * DA_HEADS, LANES),
                                    state_conv[l], state_ssm[l], True)
        for lst, val in zip(outs, (kp, vp, cp, hp, kn, vn, cn, hn)):
            lst.append(val)
    return (yp, ys) + tuple(jnp.stack(o) for o in outs)
```

```python
import functools
import math

import jax
import jax.numpy as jnp
from jax import lax
from jax.experimental import pallas as pl
from jax.experimental.pallas import tpu as pltpu

F32 = jnp.float32
BF16 = jnp.bfloat16

D_MODEL = 1024
CHUNK = 64
CHUNK_SHIFT = 6
HEAD_SHIFT = 6
EPS = 1e-5
LOG2E = math.log2(math.e)
DA_HEADS = 4
DA_HEAD_DIM = 64
DA_WIDTH = DA_HEADS * 2 * DA_HEAD_DIM
ROPE_THETA = 10000.0
SSM_HEADS = 8
SSM_HEAD_DIM = 64
SSM_WIDTH = SSM_HEADS * SSM_HEAD_DIM
SSM_GROUPS = 2
SSM_STATE = 128
CONV_W = 4
CONV_DIM = SSM_WIDTH + 2 * SSM_GROUPS * SSM_STATE
Q_OFF = 0
K_OFF = 512
V_OFF = 1024
Z_OFF = 1536
XBC_OFF = 2048
DT_OFF = 3072
N_EXPERTS = 64
N_EXPERT_GROUPS = 8
EXPERTS_PER_GROUP = 8
TOPK_GROUPS = 4
TOP_K = 8
EXPERT_DIM = 256
ROUTED_SCALE = 2.5

LANES = 128
VMEM_LIMIT = 56 * 1024 * 1024


def _cparams(*sem):
    return pltpu.CompilerParams(dimension_semantics=sem, vmem_limit_bytes=VMEM_LIMIT)


def _dot(a, b):
    return jnp.dot(a, b, preferred_element_type=F32)


def _dot_nt(a, b):
    return lax.dot_general(a, b, (((1,), (1,)), ((), ())), preferred_element_type=F32)


def _split2(a):
    hi = a.astype(BF16)
    lo = (a - hi.astype(F32)).astype(BF16)
    return hi, lo


def _split3(a):
    hi = a.astype(BF16)
    r = a - hi.astype(F32)
    mid = r.astype(BF16)
    lo = (r - mid.astype(F32)).astype(BF16)
    return hi, mid, lo


def _dot3(a, b, dot=_dot):
    ah, al = _split2(a)
    bh, bl = _split2(b)
    return dot(ah, bh) + (dot(ah, bl) + dot(al, bh))


def _dot_exact_lhs(e, a, dot=_dot):
    ah, am, al = _split3(a)
    return dot(e, ah) + (dot(e, am) + dot(e, al))


def _silu(x):
    return x * jax.nn.sigmoid(x)


def _softplus(x):
    return jnp.maximum(x, 0.0) + jnp.log1p(jnp.exp(-jnp.abs(x)))


def _ln(x):
    mu = jnp.mean(x, -1, keepdims=True)
    xc = x - mu
    var = jnp.mean(xc * xc, -1, keepdims=True)
    return xc * lax.rsqrt(var + EPS)


def _ada_kernel(c_ref, w_ref, b_ref, o_ref):
    o_ref[...] = _dot3(_silu(c_ref[...]), w_ref[...]) + b_ref[...]


def _ada(c, w_ada, b_ada):
    rows = c.shape[0]
    n = w_ada.shape[1]
    tn = 1024
    return pl.pallas_call(
        _ada_kernel,
        grid=(n // tn,),
        in_specs=[pl.BlockSpec((rows, D_MODEL), lambda i: (0, 0)),
                  pl.BlockSpec((D_MODEL, tn), lambda i: (0, i)),
                  pl.BlockSpec((1, tn), lambda i: (0, i))],
        out_specs=pl.BlockSpec((rows, tn), lambda i: (0, i)),
        out_shape=jax.ShapeDtypeStruct((rows, n), F32),
        compiler_params=_cparams("arbitrary"),
        name="ada",
    )(c, w_ada, b_ada.reshape(1, n))


def _proj_hidden(x_ref, sc_ref, sh_ref, tm):
    h = _ln(x_ref[...]) * (1.0 + sc_ref[...]) + sh_ref[...]
    return h.reshape(tm, D_MODEL)


def _proj_ssm_inputs(h, w_ref, wdt_ref):
    hb = h.astype(BF16)
    return _dot(hb, w_ref[:, Z_OFF:XBC_OFF]), _dot(hb, w_ref[:, XBC_OFF:DT_OFF]), _dot3(h, wdt_ref[...])


def _proj_qkv(h, cos_ref, sin_ref, w_ref, q_ref, k_ref, kb_ref, v_ref, vb_ref, *, bb, ts, v_transposed):
    tm = bb * ts
    hb = h.astype(BF16)
    cos = jnp.tile(cos_ref[...], (1, DA_WIDTH // LANES))
    sin = jnp.tile(sin_ref[...], (1, DA_WIDTH // LANES))
    lane = lax.broadcasted_iota(jnp.int32, (tm, DA_WIDTH), 1)
    first = (lane & (DA_HEAD_DIM - 1)) < (DA_HEAD_DIM // 2)

    def rope(t):
        rot = jnp.where(first, pltpu.roll(t, DA_WIDTH - DA_HEAD_DIM // 2, 1),
                        pltpu.roll(t, DA_HEAD_DIM // 2, 1))
        return t * cos + rot * sin

    q = rope(_dot(hb, w_ref[:, Q_OFF:K_OFF])) * (DA_HEAD_DIM ** -0.5 * LOG2E)
    q_ref[...] = q.reshape(bb, ts, DA_WIDTH).astype(BF16)
    def store_heads(ref, t):
        for hd in range(DA_HEADS):
            ref[:, pl.ds(hd, ts, stride=DA_HEADS), :] = t[:, :, hd * LANES:(hd + 1) * LANES]

    k = rope(_dot(hb, w_ref[:, K_OFF:V_OFF])).reshape(bb, ts, DA_WIDTH)
    store_heads(k_ref, k)
    kb_ref[...] = k.astype(BF16)
    v = _dot(hb, w_ref[:, V_OFF:Z_OFF]).reshape(bb, ts, DA_WIDTH)
    store_heads(v_ref, v)
    if v_transposed:
        vb_ref[0, 0] = v.reshape(tm, DA_WIDTH).T.astype(BF16)
    else:
        vb_ref[...] = v.astype(BF16)


def _proj_kernel(x_ref, sc_ref, sh_ref, cos_ref, sin_ref, w_ref, wdt_ref,
                 q_ref, k_ref, kb_ref, v_ref, vb_ref, z_ref, xbc_ref, dt_ref, *, bb, ts, v_transposed):
    h = _proj_hidden(x_ref, sc_ref, sh_ref, bb * ts)
    _proj_qkv(h, cos_ref, sin_ref, w_ref, q_ref, k_ref, kb_ref, v_ref, vb_ref,
              bb=bb, ts=ts, v_transposed=v_transposed)
    z, xbc, dt_raw = _proj_ssm_inputs(h, w_ref, wdt_ref)
    z_ref[...] = z.reshape(bb, ts, SSM_WIDTH)
    xbc_ref[...] = xbc.reshape(bb, ts, CONV_DIM)
    dt_ref[...] = dt_raw.reshape(bb, ts, LANES)


def _proj_ssd_kernel(x_ref, sc_ref, sh_ref, cos_ref, sin_ref, w_ref, wdt_ref,
                     cw_ref, cb_ref, dtb_ref, alog_ref, dsk_ref, nw_ref,
                     q_ref, k_ref, kb_ref, v_ref, vb_ref, y_ref, hout_ref, tout_ref, tail_ref, st_ref, *, ts, T):
    @pl.when(pl.program_id(1) == 0)
    def _():
        tail_ref[...] = jnp.zeros_like(tail_ref)
        st_ref[...] = jnp.zeros_like(st_ref)

    h = _proj_hidden(x_ref, sc_ref, sh_ref, ts)
    z, xbc, dt_raw = _proj_ssm_inputs(h, w_ref, wdt_ref)
    for c in range(ts // T):
        rows = slice(c * T, (c + 1) * T)
        y = _ssd_chunk(xbc[rows], z[rows], dt_raw[rows], tail_ref, st_ref,
                       cw_ref, cb_ref, dtb_ref, alog_ref, dsk_ref, nw_ref, T=T, n_valid=T)
        y_ref[0, rows, :] = y.astype(BF16)
    _proj_qkv(h, cos_ref, sin_ref, w_ref, q_ref, k_ref, kb_ref, v_ref, vb_ref, bb=1, ts=ts, v_transposed=True)
    @pl.when(pl.program_id(1) == pl.num_programs(1) - 1)
    def _():
        hout_ref[0] = st_ref[...]
        tout_ref[0] = tail_ref[...]


def _proj_specs(B, S, bb, ts, v_transposed, batch_major=False):
    tm = bb * ts
    spec = lambda shape, f: pl.BlockSpec(shape, (lambda b, i: f(i, b)) if batch_major else f)
    row3 = lambda w: spec((bb, ts, w), lambda i, b: (b, i, 0))
    modspec = lambda which: spec((bb, None, 1, D_MODEL), lambda i, b: (b, which, 0, 0))
    if v_transposed:
        vb_shape = jax.ShapeDtypeStruct((B, S // ts, DA_WIDTH, ts), BF16)
        vb_spec = spec((1, 1, DA_WIDTH, ts), lambda i, b: (b, i, 0, 0))
    else:
        vb_shape = jax.ShapeDtypeStruct((B, S, DA_WIDTH), BF16)
        vb_spec = row3(DA_WIDTH)
    heads_shape = jax.ShapeDtypeStruct((B, S * DA_HEADS, LANES), F32)
    heads_spec = spec((bb, ts * DA_HEADS, LANES), lambda i, b: (b, i, 0))
    in_specs = [row3(D_MODEL), modspec(1), modspec(0),
                spec((tm, LANES), lambda i, b: (i, 0)),
                spec((tm, LANES), lambda i, b: (i, 0)),
                spec((D_MODEL, DT_OFF), lambda i, b: (0, 0)),
                spec((D_MODEL, LANES), lambda i, b: (0, 0))]
    out_shapes = [jax.ShapeDtypeStruct((B, S, DA_WIDTH), BF16),
                  heads_shape,
                  jax.ShapeDtypeStruct((B, S, DA_WIDTH), BF16),
                  heads_shape,
                  vb_shape]
    out_specs = [row3(DA_WIDTH), heads_spec, row3(DA_WIDTH), heads_spec, vb_spec]
    return row3, in_specs, out_shapes, out_specs


def _proj(x, mod4, cos_t, sin_t, w_main, w_dt, *, bb, ts, v_transposed):
    B, S, _ = x.shape
    row3, in_specs, out_shapes, out_specs = _proj_specs(B, S, bb, ts, v_transposed)
    out_shapes += [jax.ShapeDtypeStruct((B, S, SSM_WIDTH), F32),
                   jax.ShapeDtypeStruct((B, S, CONV_DIM), F32),
                   jax.ShapeDtypeStruct((B, S, LANES), F32)]
    out_specs += [row3(SSM_WIDTH), row3(CONV_DIM), row3(LANES)]
    return pl.pallas_call(
        functools.partial(_proj_kernel, bb=bb, ts=ts, v_transposed=v_transposed),
        grid=(S // ts, B // bb),
        in_specs=in_specs,
        out_specs=out_specs,
        out_shape=out_shapes,
        compiler_params=_cparams("arbitrary", "arbitrary"),
        name="proj",
    )(x, mod4, mod4, cos_t, sin_t, w_main, w_dt)


def _proj_ssd(x, mod4, cos_t, sin_t, w_main, w_dt, conv_w, conv_b, dtb, alog, dsk, nw, *, ts, T):
    B, S, _ = x.shape
    row3, in_specs, out_shapes, out_specs = _proj_specs(B, S, 1, ts, True, batch_major=True)
    const = lambda r, w: pl.BlockSpec((r, w), lambda b, i: (0, 0))
    per_b = lambda r, w: pl.BlockSpec((1, r, w), lambda b, i: (b, 0, 0))
    in_specs += [const(CONV_W, CONV_DIM), const(1, CONV_DIM), const(1, LANES), const(1, LANES),
                 const(1, SSM_WIDTH), const(1, SSM_WIDTH)]
    out_shapes += [jax.ShapeDtypeStruct((B, S, SSM_WIDTH), BF16),
                   jax.ShapeDtypeStruct((B, SSM_WIDTH, SSM_STATE), F32),
                   jax.ShapeDtypeStruct((B, 8, CONV_DIM), F32)]
    out_specs += [row3(SSM_WIDTH), per_b(SSM_WIDTH, SSM_STATE), per_b(8, CONV_DIM)]
    return pl.pallas_call(
        functools.partial(_proj_ssd_kernel, ts=ts, T=T),
        grid=(B, S // ts),
        in_specs=in_specs,
        out_specs=out_specs,
        out_shape=out_shapes,
        scratch_shapes=[pltpu.VMEM((8, CONV_DIM), F32), pltpu.VMEM((SSM_WIDTH, SSM_STATE), F32)],
        compiler_params=_cparams("arbitrary", "arbitrary"),
        name="proj_ssd",
    )(x, mod4, mod4, cos_t, sin_t, w_main, w_dt, conv_w, conv_b, dtb, alog, dsk, nw)


def _lambda(lq1, lk1, lq2, lk2, lam_init):
    s1 = jnp.sum(lq1[...] * lk1[...], axis=1, keepdims=True)
    s2 = jnp.sum(lq2[...] * lk2[...], axis=1, keepdims=True)
    return jnp.exp(s1) - jnp.exp(s2) + lam_init


def _subln(o, subw, lam_init):
    ms = jnp.mean(o * o, -1, keepdims=True)
    return o * lax.rsqrt(ms + EPS) * subw * (1.0 - lam_init)


def _attn_prompt_kernel(lq1, lk1, lq2, lk2, subw_ref, bias_ref, q_ref, k_ref, vt_ref, o_ref,
                        acc1, acc2, sa1, sa2, sb1, sb2, mxa, mxb, m1, l1, m2, l2, *, tq, tk, lam_init):
    i = pl.program_id(2)
    qt = q_ref[0].astype(F32).T
    row = lax.broadcasted_iota(jnp.int32, (LANES, tq), 0)
    q1t = jnp.where(row < DA_HEAD_DIM, qt, 0.0).astype(BF16)
    q2t = jnp.where(row >= DA_HEAD_DIM, qt, 0.0).astype(BF16)
    acc1[...] = jnp.zeros_like(acc1)
    acc2[...] = jnp.zeros_like(acc2)
    m1[...] = jnp.full_like(m1, -jnp.inf)
    m2[...] = jnp.full_like(m2, -jnp.inf)
    l1[...] = jnp.zeros_like(l1)
    l2[...] = jnp.zeros_like(l2)
    buf_a = (sa1, sa2, mxa)
    buf_b = (sb1, sb2, mxb)

    def produce(j, buf):
        k = k_ref[0, pl.ds(pl.multiple_of(j * tk, tk), tk), :]
        for half, qh in enumerate((q1t, q2t)):
            s = _dot(k, qh)
            buf[half][...] = s
            buf[2][half:half + 1, :] = jnp.max(s, axis=0, keepdims=True)

    def consume(j, buf, masked):
        vt = vt_ref[0, j]
        for half, (m, l, acc) in enumerate(((m1, l1, acc1), (m2, l2, acc2))):
            st = buf[half][...]
            if masked:
                st = st + bias_ref[...]
                tile_max = jnp.max(st, axis=0, keepdims=True)
            else:
                tile_max = buf[2][half:half + 1, :]
            mo = m[...]
            mn = jnp.maximum(mo, tile_max)
            a = jnp.exp2(mo - mn)
            p = jnp.exp2(st - mn)
            l[...] = a * l[...] + jnp.sum(p, axis=0, keepdims=True)
            m[...] = mn
            acc[...] = a * acc[...] + _dot(vt, p.astype(BF16))

    nfull = (i * tq) // tk
    produce(nfull, buf_a)
    produce(0, buf_b)
    consume(nfull, buf_a, True)

    def pair(jj, carry):
        produce(2 * jj + 1, buf_a)
        consume(2 * jj, buf_b, False)
        produce(2 * jj + 2, buf_b)
        consume(2 * jj + 1, buf_a, False)
        return carry

    lax.fori_loop(0, nfull >> 1, pair, 0)

    @pl.when((nfull & 1) == 1)
    def _():
        consume(nfull - 1, buf_b, False)

    lam = _lambda(lq1, lk1, lq2, lk2, lam_init)
    ot = acc1[...] / l1[...] - lam * (acc2[...] / l2[...])
    ot = ot * lax.rsqrt(jnp.mean(ot * ot, axis=0, keepdims=True) + EPS)
    o_ref[0] = (ot.T * subw_ref[...] * (1.0 - lam_init)).astype(BF16)


def _attn_prompt(qb, kb, vt, lam_vecs, subw, lam_init, *, tq):
    B, S, _ = qb.shape
    nkt, tk = vt.shape[1], vt.shape[3]
    assert tq == tk, "the masked tile is the aligned diagonal tile"
    key_chunk = lax.broadcasted_iota(jnp.int32, (tk, tq), 0) >> CHUNK_SHIFT
    qry_chunk = lax.broadcasted_iota(jnp.int32, (tk, tq), 1) >> CHUNK_SHIFT
    bias = jnp.where(key_chunk <= qry_chunk, 0.0, -jnp.inf).astype(F32)
    small = pl.BlockSpec((1, DA_HEAD_DIM), lambda b, h, i: (0, 0))
    return pl.pallas_call(
        functools.partial(_attn_prompt_kernel, tq=tq, tk=tk, lam_init=lam_init),
        grid=(B, DA_HEADS, S // tq),
        in_specs=[small, small, small, small,
                  pl.BlockSpec((1, LANES), lambda b, h, i: (0, 0)),
                  pl.BlockSpec((tk, tq), lambda b, h, i: (0, 0)),
                  pl.BlockSpec((1, tq, LANES), lambda b, h, i: (b, i, h)),
                  pl.BlockSpec((1, S, LANES), lambda b, h, i: (b, 0, h)),
                  pl.BlockSpec((1, nkt, LANES, tk), lambda b, h, i: (b, 0, h, 0))],
        out_specs=pl.BlockSpec((1, tq, LANES), lambda b, h, i: (b, i, h)),
        out_shape=jax.ShapeDtypeStruct((B, S, DA_WIDTH), BF16),
        scratch_shapes=([pltpu.VMEM((LANES, tq), F32)] * 2 + [pltpu.VMEM((tk, tq), F32)] * 4
                        + [pltpu.VMEM((2, tq), F32)] * 2 + [pltpu.VMEM((1, tq), F32)] * 4),
        compiler_params=_cparams("arbitrary", "arbitrary", "arbitrary"),
        name="attn_prompt",
    )(*lam_vecs, subw, bias, qb, kb, vt)


def _attn_sample_kernel(lq1, lk1, lq2, lk2, subw_ref, q_ref, kn_ref, vn_ref, kp_ref, vp_ref, o_ref,
                        *, lam_init):
    s = q_ref.shape[1]
    lam = _lambda(lq1, lk1, lq2, lk2, lam_init)
    lane = lax.broadcasted_iota(jnp.int32, (s, LANES), 1)
    outs = []
    for h in range(DA_HEADS):
        cols = slice(h * LANES, (h + 1) * LANES)
        q = q_ref[0, :, cols]
        zero = jnp.zeros_like(q)
        kn = kn_ref[0, :, cols]
        vn = vn_ref[0, :, cols]
        past = kp_ref.shape[1] // DA_HEADS
        kp = kp_ref[0, pl.ds(h, past, stride=DA_HEADS), :].astype(BF16)
        vp = vp_ref[0, pl.ds(h, past, stride=DA_HEADS), :].astype(BF16)
        probs = []
        for qm in (jnp.where(lane < DA_HEAD_DIM, q, zero), jnp.where(lane >= DA_HEAD_DIM, q, zero)):
            sp = _dot_nt(qm, kp)
            sn = _dot_nt(qm, kn)
            m = jnp.maximum(jnp.max(sp, -1, keepdims=True), jnp.max(sn, -1, keepdims=True))
            pp = jnp.exp2(sp - m)
            pn = jnp.exp2(sn - m)
            l = jnp.sum(pp, -1, keepdims=True) + jnp.sum(pn, -1, keepdims=True)
            probs.append((pp / l, pn / l))
        ap = probs[0][0] - lam * probs[1][0]
        an = probs[0][1] - lam * probs[1][1]
        o = _dot(ap.astype(BF16), vp) + _dot(an.astype(BF16), vn)
        outs.append(_subln(o, subw_ref[...], lam_init))
    o_ref[0] = jnp.concatenate(outs, axis=-1).astype(BF16)


def _attn_sample(qb, kb, vb, k_past, v_past, lam_vecs, subw, lam_init):
    B, s, _ = qb.shape
    past_rows = k_past.shape[1]
    small = pl.BlockSpec((1, DA_HEAD_DIM), lambda b: (0, 0))
    new = pl.BlockSpec((1, s, DA_WIDTH), lambda b: (b, 0, 0))
    old = pl.BlockSpec((1, past_rows, LANES), lambda b: (b, 0, 0))
    return pl.pallas_call(
        functools.partial(_attn_sample_kernel, lam_init=lam_init),
        grid=(B,),
        in_specs=[small, small, small, small, pl.BlockSpec((1, LANES), lambda b: (0, 0)),
                  new, new, new, old, old],
        out_specs=new,
        out_shape=jax.ShapeDtypeStruct((B, s, DA_WIDTH), BF16),
        compiler_params=_cparams("arbitrary"),
        name="attn_sample",
    )(*lam_vecs, subw, qb, kb, vb, k_past, v_past)


def _ssd_kernel(xbc_ref, z_ref, dt_ref, cp_ref, h0_ref, cw_ref, cb_ref, dtb_ref, alog_ref, dsk_ref, nw_ref,
                y_ref, hout_ref, tail_ref, st_ref, *, t_in, T):
    j = pl.program_id(1)

    @pl.when(j == 0)
    def _():
        tail_ref[...] = cp_ref[0]
        st_ref[...] = h0_ref[0]

    def rows(ref, width):
        val = ref[0]
        if T == t_in:
            return val
        return jnp.concatenate([val, jnp.zeros((T - t_in, width), F32)], axis=0)

    y = _ssd_chunk(rows(xbc_ref, CONV_DIM), rows(z_ref, SSM_WIDTH), rows(dt_ref, LANES), tail_ref, st_ref,
                   cw_ref, cb_ref, dtb_ref, alog_ref, dsk_ref, nw_ref, T=T, n_valid=t_in)
    y_ref[0] = y[:t_in, :].astype(BF16)

    @pl.when(j == pl.num_programs(1) - 1)
    def _():
        hout_ref[0] = st_ref[...]


def _ssd_chunk(xb, zb, dt_raw, tail_ref, st_ref, cw_ref, cb_ref, dtb_ref, alog_ref, dsk_ref, nw_ref, *, T, n_valid):
    PAIR = 2 * SSM_HEAD_DIM
    xp = jnp.concatenate([tail_ref[...], xb], axis=0)
    conv = cb_ref[...] + cw_ref[CONV_W - 1:CONV_W, :] * xb
    for sft in range(1, CONV_W):
        conv = conv + cw_ref[CONV_W - 1 - sft:CONV_W - sft, :] * pltpu.roll(xp, sft, 0)[8:, :]
    if T == n_valid:
        tail_ref[...] = xb[T - 8:, :]
    xa = _silu(conv)
    xs = xa[:, :SSM_WIDTH]

    dt = _softplus(dt_raw + dtb_ref[...])
    if T != n_valid:
        valid = lax.broadcasted_iota(jnp.int32, (T, LANES), 0) < n_valid
        dt = jnp.where(valid, dt, 0.0)
    da = dt * (-jnp.exp(alog_ref[...]))

    rr = lax.broadcasted_iota(jnp.int32, (T, T), 0)
    cc = lax.broadcasted_iota(jnp.int32, (T, T), 1)
    causal = cc <= rr
    tril = jnp.where(causal, 1.0, 0.0).astype(BF16)
    acs = _dot_exact_lhs(tril, da)
    e16 = jnp.where(lax.broadcasted_iota(jnp.int32, (16, LANES), 0)
                    == lax.broadcasted_iota(jnp.int32, (16, LANES), 1), 1.0, 0.0).astype(BF16)
    acs_t = _dot_exact_lhs(e16, acs, dot=_dot_nt)

    last = acs[T - 1:T, :]
    expand = jnp.where((lax.broadcasted_iota(jnp.int32, (LANES, SSM_WIDTH), 1) >> HEAD_SHIFT)
                       == lax.broadcasted_iota(jnp.int32, (LANES, SSM_WIDTH), 0), 1.0, 0.0).astype(BF16)

    def per_head_lanes(v):
        hi, lo = _split2(v)
        return _dot(hi, expand) + _dot(lo, expand)

    xdt = xs * per_head_lanes(dt)
    xdd = xdt * per_head_lanes(jnp.exp(last - acs))
    eacs = per_head_lanes(jnp.exp(acs))

    head_of_row = lax.broadcasted_iota(jnp.int32, (SSM_WIDTH, LANES), 0) >> HEAD_SHIFT
    sel = head_of_row == lax.broadcasted_iota(jnp.int32, (SSM_WIDTH, LANES), 1)
    rowdec = jnp.sum(jnp.where(sel, jnp.exp(last), 0.0), axis=1, keepdims=True)

    lane = lax.broadcasted_iota(jnp.int32, (T, PAIR), 1)
    ys = []
    for p in range(SSM_HEADS // 2):
        g = (2 * p) // (SSM_HEADS // SSM_GROUPS)
        bm = xa[:, SSM_WIDTH + g * SSM_STATE:SSM_WIDTH + (g + 1) * SSM_STATE].astype(BF16)
        cm = xa[:, SSM_WIDTH + (SSM_GROUPS + g) * SSM_STATE:
                SSM_WIDTH + (SSM_GROUPS + g + 1) * SSM_STATE].astype(BF16)
        cb = _dot_nt(cm, bm)
        cols = slice(p * PAIR, (p + 1) * PAIR)
        xdt_p = xdt[:, cols].astype(BF16)
        yd = []
        for hh in (2 * p, 2 * p + 1):
            seg = acs[:, hh:hh + 1] - acs_t[hh:hh + 1, :]
            lmat = jnp.exp(jnp.where(causal, seg, -jnp.inf))
            yd.append(_dot((cb * lmat).astype(BF16), xdt_p))
        y_diag = jnp.where(lane < SSM_HEAD_DIM, yd[0], yd[1])
        st = st_ref[cols, :]
        y_off = _dot_nt(cm, st.astype(BF16)) * eacs[:, cols]
        ys.append(y_diag + y_off)
        st_ref[cols, :] = st * rowdec[cols, :] + _dot(xdd[:, cols].T.astype(BF16), bm)
    y = jnp.concatenate(ys, axis=-1) + xs * dsk_ref[...]
    y = y * _silu(zb)
    half = SSM_WIDTH // SSM_GROUPS
    outs = []
    for g in range(SSM_GROUPS):
        yg = y[:, g * half:(g + 1) * half]
        outs.append(yg * lax.rsqrt(jnp.mean(yg * yg, -1, keepdims=True) + EPS))
    return jnp.concatenate(outs, axis=-1) * nw_ref[...]


def _ssd(xbc, z, dt_raw, conv_prev8, h0, conv_w, conv_b, dtb, alog, dsk, nw, *, t_in, T):
    B, S, _ = xbc.shape
    row = lambda w: pl.BlockSpec((1, t_in, w), lambda b, j: (b, j, 0))
    per_b = lambda r, w: pl.BlockSpec((1, r, w), lambda b, j: (b, 0, 0))
    const = lambda r, w: pl.BlockSpec((r, w), lambda b, j: (0, 0))
    return pl.pallas_call(
        functools.partial(_ssd_kernel, t_in=t_in, T=T),
        grid=(B, S // t_in),
        in_specs=[row(CONV_DIM), row(SSM_WIDTH), row(LANES),
                  per_b(8, CONV_DIM), per_b(SSM_WIDTH, SSM_STATE),
                  const(CONV_W, CONV_DIM), const(1, CONV_DIM), const(1, LANES), const(1, LANES),
                  const(1, SSM_WIDTH), const(1, SSM_WIDTH)],
        out_specs=[row(SSM_WIDTH), per_b(SSM_WIDTH, SSM_STATE)],
        out_shape=[jax.ShapeDtypeStruct((B, S, SSM_WIDTH), BF16),
                   jax.ShapeDtypeStruct((B, SSM_WIDTH, SSM_STATE), F32)],
        scratch_shapes=[pltpu.VMEM((8, CONV_DIM), F32), pltpu.VMEM((SSM_WIDTH, SSM_STATE), F32)],
        compiler_params=_cparams("arbitrary", "arbitrary"),
        name="ssd",
    )(xbc, z, dt_raw, conv_prev8, h0, conv_w, conv_b, dtb, alog, dsk, nw)


def _router(h, rwt_ref, rb_ref):
    tm = h.shape[0]
    G = N_EXPERT_GROUPS
    logits = _dot3(rwt_ref[...], h, dot=_dot_nt)
    score = jax.nn.sigmoid(logits[:N_EXPERTS, :])
    biased = score + rb_ref[:N_EXPERTS, :]
    sc = [score[G * j:G * (j + 1), :] for j in range(EXPERTS_PER_GROUP)]
    v = [biased[G * j:G * (j + 1), :] for j in range(EXPERTS_PER_GROUP)]

    def tree(op, xs):
        while len(xs) > 1:
            xs = [op(xs[a], xs[a + 1]) for a in range(0, len(xs) - 1, 2)] + ([xs[-1]] if len(xs) % 2 else [])
        return xs[0]

    m1 = tree(jnp.maximum, v)
    first = tree(jnp.minimum, [jnp.where(v[j] == m1, float(j), float(EXPERTS_PER_GROUP))
                               for j in range(EXPERTS_PER_GROUP)])
    m2 = tree(jnp.maximum, [jnp.where(first == float(j), -jnp.inf, v[j]) for j in range(EXPERTS_PER_GROUP)])
    grp = m1 + m2

    gid = lax.broadcasted_iota(jnp.int32, (G, tm), 0)
    cnt = jnp.zeros((G, tm), jnp.int32)
    for g2 in range(G):
        row = grp[g2:g2 + 1, :]
        beats = (row > grp) | ((row == grp) & (gid > g2))
        cnt = cnt + beats.astype(jnp.int32)
    gmask = cnt < TOPK_GROUPS

    idx = [(gid * EXPERTS_PER_GROUP + j).astype(F32) for j in range(EXPERTS_PER_GROUP)]
    mv = [jnp.where(gmask, v[j], -jnp.inf) for j in range(EXPERTS_PER_GROUP)]
    w = [jnp.zeros((G, tm), F32) for _ in range(EXPERTS_PER_GROUP)]
    for _ in range(TOP_K):
        top = jnp.max(tree(jnp.maximum, mv), axis=0, keepdims=True)
        who = jnp.min(tree(jnp.minimum, [jnp.where(mv[j] == top, idx[j], float(N_EXPERTS))
                                         for j in range(EXPERTS_PER_GROUP)]), axis=0, keepdims=True)
        for j in range(EXPERTS_PER_GROUP):
            hit = idx[j] == who
            w[j] = jnp.where(hit, sc[j], w[j])
            mv[j] = jnp.where(hit, -jnp.inf, mv[j])
    tot = w[0]
    for j in range(1, EXPERTS_PER_GROUP):
        tot = tot + w[j]
    tot = jnp.sum(tot, axis=0, keepdims=True)
    gates = [w[j] / tot * ROUTED_SCALE for j in range(EXPERTS_PER_GROUP)]
    shared = jnp.where(lax.broadcasted_iota(jnp.int32, (LANES - N_EXPERTS, tm), 0) == 0, 1.0, 0.0)
    by_slot = jnp.concatenate(gates + [shared], axis=0)
    e_id = lax.broadcasted_iota(jnp.int32, (LANES, LANES), 0)
    r_id = lax.broadcasted_iota(jnp.int32, (LANES, LANES), 1)
    src_row = jnp.where(e_id < N_EXPERTS, (e_id & (G - 1)) * EXPERTS_PER_GROUP + (e_id >> 3), e_id)
    perm = jnp.where(r_id == src_row, 1.0, 0.0).astype(BF16)
    return _dot_exact_lhs(perm, by_slot)


def _mix_kernel(x_ref, att_ref, ssm_ref, g1_ref, sc2_ref, sh2_ref, wo_ref, l1g_ref, l1b_ref, rwt_ref, rb_ref,
                x1_ref, h2_ref, gate_ref, *, bb, ts, alpha):
    tm = bb * ts
    att = att_ref[...].reshape(tm, DA_WIDTH)
    ssm = ssm_ref[...].reshape(tm, SSM_WIDTH)
    mix = _dot(att, wo_ref[:DA_WIDTH, :]) + _dot(ssm, wo_ref[DA_WIDTH:, :])
    y = alpha * x_ref[...] + (1.0 + g1_ref[...]) * mix.reshape(bb, ts, D_MODEL)
    x1 = _ln(y) * l1g_ref[...] + l1b_ref[...]
    x1_ref[...] = x1
    h2 = (_ln(x1) * (1.0 + sc2_ref[...]) + sh2_ref[...]).reshape(tm, D_MODEL)
    h2_ref[...] = h2.astype(BF16)
    gate_ref[...] = _router(h2, rwt_ref, rb_ref).T


def _mix(x, att, ssm, mod4, wo, l1g, l1b, rwt, rb, *, bb, ts, alpha):
    B, S, _ = x.shape
    tm = bb * ts
    row3 = lambda w: pl.BlockSpec((bb, ts, w), lambda i, b: (b, i, 0))
    modspec = lambda which: pl.BlockSpec((bb, None, 1, D_MODEL), lambda i, b: (b, which, 0, 0))
    const = lambda r, w: pl.BlockSpec((r, w), lambda i, b: (0, 0))
    flat = lambda w: pl.BlockSpec((tm, w), lambda i, b: (b * (S // ts) + i, 0))
    return pl.pallas_call(
        functools.partial(_mix_kernel, bb=bb, ts=ts, alpha=alpha),
        grid=(S // ts, B // bb),
        in_specs=[row3(D_MODEL), row3(DA_WIDTH), row3(SSM_WIDTH), modspec(2), modspec(4), modspec(3),
                  const(D_MODEL, D_MODEL), const(1, D_MODEL), const(1, D_MODEL),
                  const(LANES, D_MODEL), const(LANES, 1)],
        out_specs=[row3(D_MODEL), flat(D_MODEL), flat(LANES)],
        out_shape=[jax.ShapeDtypeStruct((B, S, D_MODEL), F32),
                   jax.ShapeDtypeStruct((B * S, D_MODEL), BF16),
                   jax.ShapeDtypeStruct((B * S, LANES), F32)],
        compiler_params=_cparams("arbitrary", "arbitrary"),
        name="mix",
    )(x, att, ssm, mod4, mod4, mod4, wo, l1g, l1b, rwt, rb)


N_EXPERTS_ALL = N_EXPERTS + 1
EXPERTS_PER_STEP = 5


def _moe_kernel(h_ref, gate_ref, x1_ref, g2_ref, wg_ref, wu_ref, wd_ref,
                l2g_ref, l2b_ref, o_ref, acc_ref, *, bb, ts, alpha):
    tm = bb * ts
    e = pl.program_id(2)
    h = h_ref[...]

    @pl.when(e == 0)
    def _():
        acc_ref[...] = jnp.zeros_like(acc_ref)

    gate = gate_ref[...]
    lane = lax.broadcasted_iota(jnp.int32, (tm, LANES), 1)
    hids = []
    for s in range(EXPERTS_PER_STEP):
        col = e * EXPERTS_PER_STEP + s
        gcol = jnp.sum(jnp.where(lane == col, gate, 0.0), axis=1, keepdims=True)
        hid = _silu(_dot(h, wg_ref[s])) * _dot(h, wu_ref[s]) * gcol
        hids.append(hid.astype(BF16))
    hid = jnp.concatenate(hids, axis=-1)
    acc_ref[...] += _dot(hid, wd_ref[...].reshape(EXPERTS_PER_STEP * EXPERT_DIM, D_MODEL))

    @pl.when(e == pl.num_programs(2) - 1)
    def _():
        ffn = acc_ref[...].reshape(bb, ts, D_MODEL)
        y = alpha * x1_ref[...] + (1.0 + g2_ref[...]) * ffn
        o_ref[...] = _ln(y) * l2g_ref[...] + l2b_ref[...]


def _moe(h2, gate, x1, mod4, wg, wu, wd, l2g, l2b, *, bb, ts, alpha):
    B, S, _ = x1.shape
    tm = bb * ts
    row3 = lambda w: pl.BlockSpec((bb, ts, w), lambda i, b, e: (b, i, 0))
    flat = lambda w: pl.BlockSpec((tm, w), lambda i, b, e: (b * (S // ts) + i, 0))
    const = lambda r, w: pl.BlockSpec((r, w), lambda i, b, e: (0, 0))
    ew_in = pl.BlockSpec((EXPERTS_PER_STEP, D_MODEL, EXPERT_DIM), lambda i, b, e: (e, 0, 0))
    ew_out = pl.BlockSpec((EXPERTS_PER_STEP, EXPERT_DIM, D_MODEL), lambda i, b, e: (e, 0, 0))
    return pl.pallas_call(
        functools.partial(_moe_kernel, bb=bb, ts=ts, alpha=alpha),
        grid=(S // ts, B // bb, N_EXPERTS_ALL // EXPERTS_PER_STEP),
        in_specs=[flat(D_MODEL), flat(LANES), row3(D_MODEL),
                  pl.BlockSpec((bb, None, 1, D_MODEL), lambda i, b, e: (b, 5, 0, 0)),
                  ew_in, ew_in, ew_out,
                  const(1, D_MODEL), const(1, D_MODEL)],
        out_specs=row3(D_MODEL),
        out_shape=jax.ShapeDtypeStruct((B, S, D_MODEL), F32),
        scratch_shapes=[pltpu.VMEM((tm, D_MODEL), F32)],
        compiler_params=_cparams("arbitrary", "arbitrary", "arbitrary"),
        name="moe",
    )(h2, gate, x1, mod4, wg, wu, wd, l2g, l2b)


def _rope_tables(pos):
    half = DA_HEAD_DIM // 2
    inv = ROPE_THETA ** (-jnp.arange(half, dtype=F32) / half)
    ang = pos.astype(F32)[:, None] * inv[None, :]
    cos = jnp.cos(ang)
    sin = jnp.sin(ang)
    reps = LANES // DA_HEAD_DIM
    cos_t = jnp.tile(jnp.concatenate([cos, cos], -1), (1, reps))
    sin_t = jnp.tile(jnp.concatenate([-sin, sin], -1), (1, reps))
    return cos_t, sin_t


def _layer(x, mod4, pos, prm, layer_idx, depth, k_past, v_past, conv_prev, ssm_prev, sample):
    B, S, _ = x.shape
    alpha = (2 * depth) ** 0.25
    lam_init = 0.8 - 0.6 * math.exp(-0.3 * layer_idx)
    cos_t, sin_t = _rope_tables(pos)
    if sample:
        bb, ts = B, S
        cos_t = jnp.tile(cos_t, (B, 1))
        sin_t = jnp.tile(sin_t, (B, 1))
    else:
        bb, ts = 1, min(512, S)
    ssd_consts = (prm['conv_w'], prm['conv_b'], prm['dt_bias'], prm['a_log'], prm['d_skip'], prm['ssm_norm_w'])
    if sample:
        qb, k, kb, v, vb, z, xbc, dt_raw = _proj(x, mod4, cos_t, sin_t, prm['w_main'], prm['w_dt'],
                                                 bb=bb, ts=ts, v_transposed=False)
        att = _attn_sample(qb, kb, vb, k_past, v_past, prm['lam_vecs'], prm['subln_w'], lam_init)
        conv_prev8 = jnp.pad(conv_prev, ((0, 0), (8 - (CONV_W - 1), 0), (0, 0)))
        h0 = ssm_prev.reshape(B, SSM_WIDTH, SSM_STATE)
        ssm, h_last = _ssd(xbc, z, dt_raw, conv_prev8, h0, *ssd_consts, t_in=S, T=max(S, LANES))
        new_conv = jnp.concatenate([conv_prev, xbc], axis=1)[:, -(CONV_W - 1):]
    else:
        qb, k, kb, v, vb, ssm, h_last, tail = _proj_ssd(x, mod4, cos_t, sin_t, prm['w_main'], prm['w_dt'],
                                                        *ssd_consts, ts=ts, T=min(256, S))
        att = _attn_prompt(qb, kb, vb, prm['lam_vecs'], prm['subln_w'], lam_init, tq=min(512, S))
        new_conv = tail[:, -(CONV_W - 1):]
    if not sample:
        ts = min(1024, S)
    x1, h2, gate = _mix(x, att, ssm, mod4, prm['w_out'], prm['ln1_g'], prm['ln1_b'], prm['router_wt'],
                        prm['router_b'], bb=bb, ts=ts, alpha=alpha)
    y = _moe(h2, gate, x1, mod4, prm['wg'], prm['wu'], prm['wd'],
             prm['ln2_g'], prm['ln2_b'], bb=bb, ts=ts, alpha=alpha)
    return (y, k.reshape(B, S, DA_HEADS, 2 * DA_HEAD_DIM), v.reshape(B, S, DA_HEADS, 2 * DA_HEAD_DIM),
            new_conv, h_last.reshape(B, SSM_HEADS, SSM_HEAD_DIM, SSM_STATE))


CAST_EXPERTS = 8


def _cast_kernel(w_ref, s_ref, o_ref):
    last = pl.num_programs(0) - 1

    @pl.when(pl.program_id(0) < last)
    def _():
        o_ref[...] = w_ref[...].astype(BF16)

    @pl.when(pl.program_id(0) == last)
    def _():
        o_ref[0] = s_ref[...].astype(BF16)


def _cast_with_shared(w, s):
    n, r, c = w.shape
    full = n // CAST_EXPERTS
    return pl.pallas_call(
        _cast_kernel,
        grid=(full + 1,),
        in_specs=[pl.BlockSpec((CAST_EXPERTS, r, c), lambda e: (jnp.minimum(e, full - 1), 0, 0)),
                  pl.BlockSpec((r, c), lambda e: (0, 0))],
        out_specs=pl.BlockSpec((CAST_EXPERTS, r, c), lambda e: (e, 0, 0)),
        out_shape=jax.ShapeDtypeStruct((n + 1, r, c), BF16),
        compiler_params=_cparams("arbitrary"),
        name="cast_experts",
    )(w, s)


def _prep_params(w_in, lq1, lk1, lq2, lk2, subln_w, conv_w, conv_b, dt_bias, a_log, d_skip, ssm_norm_w, w_out,
                 ln1_g, ln1_b, router_w, router_bias, ewg, ewu, ewd, swg, swu, swd, ln2_g, ln2_b):
    pad8 = lambda a: jnp.pad(a.reshape(1, -1), ((0, 0), (0, LANES - a.shape[-1])))
    row = lambda a: a.reshape(1, -1)
    perm = (jnp.arange(N_EXPERTS) % N_EXPERT_GROUPS) * EXPERTS_PER_GROUP + jnp.arange(N_EXPERTS) // N_EXPERT_GROUPS
    rwt = jnp.pad(router_w.T[perm], ((0, LANES - N_EXPERTS), (0, 0)))
    rb = jnp.pad(router_bias[perm], (0, LANES - N_EXPERTS)).reshape(LANES, 1)
    with_shared = _cast_with_shared
    return {
        'w_main': w_in[:, :DT_OFF].astype(BF16),
        'w_dt': jnp.pad(w_in[:, DT_OFF:], ((0, 0), (0, LANES - SSM_HEADS))),
        'lam_vecs': (row(lq1), row(lk1), row(lq2), row(lk2)),
        'subln_w': row(subln_w),
        'conv_w': conv_w, 'conv_b': row(conv_b),
        'dt_bias': pad8(dt_bias), 'a_log': pad8(a_log),
        'd_skip': row(jnp.repeat(d_skip, SSM_HEAD_DIM)), 'ssm_norm_w': row(ssm_norm_w),
        'w_out': w_out.astype(BF16), 'ln1_g': row(ln1_g), 'ln1_b': row(ln1_b),
        'router_wt': rwt, 'router_b': rb,
        'wg': with_shared(ewg, swg), 'wu': with_shared(ewu, swu), 'wd': with_shared(ewd, swd),
        'ln2_g': row(ln2_g), 'ln2_b': row(ln2_b),
    }


def kernel(x_prompt, x_sample, c_prompt, c_sample, cache_k, cache_v, state_conv, state_ssm, w_ada, b_ada, w_in, lambda_q1, lambda_k1, lambda_q2, lambda_k2, subln_w, conv_w, conv_b, dt_bias, a_log, d_skip, ssm_norm_w, w_out, ln1_g, ln1_b, router_w, router_bias, exp_w_gate, exp_w_up, exp_w_down, sh_w_gate, sh_w_up, sh_w_down, ln2_g, ln2_b):
    depth = w_ada.shape[0]
    bp, sp, _ = x_prompt.shape
    bs, ss, _ = x_sample.shape
    past = cache_k.shape[2]
    pos_prompt = jnp.arange(sp)
    pos_sample = past + jnp.arange(ss)
    c_rows = bp + bs
    c_pad = (-c_rows) % 8
    c_all = jnp.pad(jnp.concatenate([c_prompt, c_sample], 0), ((0, c_pad), (0, 0)))
    yp, ys = x_prompt, x_sample
    outs = [[] for _ in range(8)]
    for l in range(depth):
        prm = _prep_params(w_in[l], lambda_q1[l], lambda_k1[l], lambda_q2[l], lambda_k2[l], subln_w[l],
                           conv_w[l], conv_b[l], dt_bias[l], a_log[l], d_skip[l], ssm_norm_w[l], w_out[l],
                           ln1_g[l], ln1_b[l], router_w[l], router_bias[l], exp_w_gate[l], exp_w_up[l],
                           exp_w_down[l], sh_w_gate[l], sh_w_up[l], sh_w_down[l], ln2_g[l], ln2_b[l])
        mod = _ada(c_all, w_ada[l], b_ada[l]).reshape(c_rows + c_pad, 6, 1, D_MODEL)
        yp, kp, vp, cp, hp = _layer(yp, mod[:bp], pos_prompt, prm, l, depth, None, None, None, None, False)
        ys, kn, vn, cn, hn = _layer(ys, mod[bp:c_rows], pos_sample, prm, l, depth,
                                    cache_k[l].reshape(bs, past * DA_HEADS, LANES),
                                    cache_v[l].reshape(bs, past * DA_HEADS, LANES),
                                    state_conv[l], state_ssm[l], True)
        for lst, val in zip(outs, (kp, vp, cp, hp, kn, vn, cn, hn)):
            lst.append(val)
    return (yp, ys) + tuple(jnp.stack(o) for o in outs)
```

```python
import functools
import math

import jax
import jax.numpy as jnp
from jax import lax
from jax.experimental import pallas as pl
from jax.experimental.pallas import tpu as pltpu

F32 = jnp.float32
BF16 = jnp.bfloat16

D_MODEL = 1024
CHUNK = 64
CHUNK_SHIFT = 6
HEAD_SHIFT = 6
EPS = 1e-5
LOG2E = math.log2(math.e)
DA_HEADS = 4
DA_HEAD_DIM = 64
DA_WIDTH = DA_HEADS * 2 * DA_HEAD_DIM
ROPE_THETA = 10000.0
SSM_HEADS = 8
SSM_HEAD_DIM = 64
SSM_WIDTH = SSM_HEADS * SSM_HEAD_DIM
SSM_GROUPS = 2
SSM_STATE = 128
CONV_W = 4
CONV_DIM = SSM_WIDTH + 2 * SSM_GROUPS * SSM_STATE
Q_OFF = 0
K_OFF = 512
V_OFF = 1024
Z_OFF = 1536
XBC_OFF = 2048
DT_OFF = 3072
N_EXPERTS = 64
N_EXPERT_GROUPS = 8
EXPERTS_PER_GROUP = 8
TOPK_GROUPS = 4
TOP_K = 8
EXPERT_DIM = 256
ROUTED_SCALE = 2.5

LANES = 128
VMEM_LIMIT = 56 * 1024 * 1024


def _cparams(*sem):
    return pltpu.CompilerParams(dimension_semantics=sem, vmem_limit_bytes=VMEM_LIMIT)


def _dot(a, b):
    return jnp.dot(a, b, preferred_element_type=F32)


def _dot_nt(a, b):
    return lax.dot_general(a, b, (((1,), (1,)), ((), ())), preferred_element_type=F32)


def _split2(a):
    hi = a.astype(BF16)
    lo = (a - hi.astype(F32)).astype(BF16)
    return hi, lo


def _split3(a):
    hi = a.astype(BF16)
    r = a - hi.astype(F32)
    mid = r.astype(BF16)
    lo = (r - mid.astype(F32)).astype(BF16)
    return hi, mid, lo


def _dot3(a, b, dot=_dot):
    ah, al = _split2(a)
    bh, bl = _split2(b)
    return dot(ah, bh) + (dot(ah, bl) + dot(al, bh))


def _dot_exact_lhs(e, a, dot=_dot):
    ah, am, al = _split3(a)
    return dot(e, ah) + (dot(e, am) + dot(e, al))


def _silu(x):
    return x * jax.nn.sigmoid(x)


def _softplus(x):
    return jnp.maximum(x, 0.0) + jnp.log1p(jnp.exp(-jnp.abs(x)))


def _ln(x):
    mu = jnp.mean(x, -1, keepdims=True)
    xc = x - mu
    var = jnp.mean(xc * xc, -1, keepdims=True)
    return xc * lax.rsqrt(var + EPS)


def _ada_kernel(c_ref, w_ref, b_ref, o_ref):
    o_ref[...] = _dot3(_silu(c_ref[...]), w_ref[...]) + b_ref[...]


def _ada(c, w_ada, b_ada):
    rows = c.shape[0]
    n = w_ada.shape[1]
    tn = 1024
    return pl.pallas_call(
        _ada_kernel,
        grid=(n // tn,),
        in_specs=[pl.BlockSpec((rows, D_MODEL), lambda i: (0, 0)),
                  pl.BlockSpec((D_MODEL, tn), lambda i: (0, i)),
                  pl.BlockSpec((1, tn), lambda i: (0, i))],
        out_specs=pl.BlockSpec((rows, tn), lambda i: (0, i)),
        out_shape=jax.ShapeDtypeStruct((rows, n), F32),
        compiler_params=_cparams("arbitrary"),
        name="ada",
    )(c, w_ada, b_ada.reshape(1, n))


def _proj_hidden(x_ref, sc_ref, sh_ref, tm):
    h = _ln(x_ref[...]) * (1.0 + sc_ref[...]) + sh_ref[...]
    return h.reshape(tm, D_MODEL)


def _proj_ssm_inputs(h, w_ref, wdt_ref):
    hb = h.astype(BF16)
    return _dot(hb, w_ref[:, Z_OFF:XBC_OFF]), _dot(hb, w_ref[:, XBC_OFF:DT_OFF]), _dot3(h, wdt_ref[...])


def _proj_qkv(h, cos_ref, sin_ref, w_ref, q_ref, k_ref, kb_ref, v_ref, vb_ref, *, bb, ts, v_transposed):
    tm = bb * ts
    hb = h.astype(BF16)
    cos = jnp.tile(cos_ref[...], (1, DA_WIDTH // LANES))
    sin = jnp.tile(sin_ref[...], (1, DA_WIDTH // LANES))
    lane = lax.broadcasted_iota(jnp.int32, (tm, DA_WIDTH), 1)
    first = (lane & (DA_HEAD_DIM - 1)) < (DA_HEAD_DIM // 2)

    def rope(t):
        rot = jnp.where(first, pltpu.roll(t, DA_WIDTH - DA_HEAD_DIM // 2, 1),
                        pltpu.roll(t, DA_HEAD_DIM // 2, 1))
        return t * cos + rot * sin

    q = rope(_dot(hb, w_ref[:, Q_OFF:K_OFF])) * (DA_HEAD_DIM ** -0.5 * LOG2E)
    q_ref[...] = q.reshape(bb, ts, DA_WIDTH).astype(BF16)
    def store_heads(ref, t):
        for hd in range(DA_HEADS):
            ref[:, pl.ds(hd, ts, stride=DA_HEADS), :] = t[:, :, hd * LANES:(hd + 1) * LANES]

    k = rope(_dot(hb, w_ref[:, K_OFF:V_OFF])).reshape(bb, ts, DA_WIDTH)
    store_heads(k_ref, k)
    kb_ref[...] = k.astype(BF16)
    v = _dot(hb, w_ref[:, V_OFF:Z_OFF]).reshape(bb, ts, DA_WIDTH)
    store_heads(v_ref, v)
    if v_transposed:
        vb_ref[0, 0] = v.reshape(tm, DA_WIDTH).T.astype(BF16)
    else:
        vb_ref[...] = v.astype(BF16)


def _proj_kernel(x_ref, sc_ref, sh_ref, cos_ref, sin_ref, w_ref, wdt_ref,
                 q_ref, k_ref, kb_ref, v_ref, vb_ref, z_ref, xbc_ref, dt_ref, *, bb, ts, v_transposed):
    h = _proj_hidden(x_ref, sc_ref, sh_ref, bb * ts)
    _proj_qkv(h, cos_ref, sin_ref, w_ref, q_ref, k_ref, kb_ref, v_ref, vb_ref,
              bb=bb, ts=ts, v_transposed=v_transposed)
    z, xbc, dt_raw = _proj_ssm_inputs(h, w_ref, wdt_ref)
    z_ref[...] = z.reshape(bb, ts, SSM_WIDTH)
    xbc_ref[...] = xbc.reshape(bb, ts, CONV_DIM)
    dt_ref[...] = dt_raw.reshape(bb, ts, LANES)


def _proj_ssd_kernel(x_ref, sc_ref, sh_ref, cos_ref, sin_ref, w_ref, wdt_ref,
                     cw_ref, cb_ref, dtb_ref, alog_ref, dsk_ref, nw_ref,
                     q_ref, k_ref, kb_ref, v_ref, vb_ref, y_ref, hout_ref, tout_ref, tail_ref, st_ref, *, ts, T):
    @pl.when(pl.program_id(1) == 0)
    def _():
        tail_ref[...] = jnp.zeros_like(tail_ref)
        st_ref[...] = jnp.zeros_like(st_ref)

    h = _proj_hidden(x_ref, sc_ref, sh_ref, ts)
    z, xbc, dt_raw = _proj_ssm_inputs(h, w_ref, wdt_ref)
    for c in range(ts // T):
        rows = slice(c * T, (c + 1) * T)
        y = _ssd_chunk(xbc[rows], z[rows], dt_raw[rows], tail_ref, st_ref,
                       cw_ref, cb_ref, dtb_ref, alog_ref, dsk_ref, nw_ref, T=T, n_valid=T)
        y_ref[0, rows, :] = y.astype(BF16)
    _proj_qkv(h, cos_ref, sin_ref, w_ref, q_ref, k_ref, kb_ref, v_ref, vb_ref, bb=1, ts=ts, v_transposed=True)
    @pl.when(pl.program_id(1) == pl.num_programs(1) - 1)
    def _():
        hout_ref[0] = st_ref[...]
        tout_ref[0] = tail_ref[...]


def _proj_specs(B, S, bb, ts, v_transposed, batch_major=False):
    tm = bb * ts
    spec = lambda shape, f: pl.BlockSpec(shape, (lambda b, i: f(i, b)) if batch_major else f)
    row3 = lambda w: spec((bb, ts, w), lambda i, b: (b, i, 0))
    modspec = lambda which: spec((bb, None, 1, D_MODEL), lambda i, b: (b, which, 0, 0))
    if v_transposed:
        vb_shape = jax.ShapeDtypeStruct((B, S // ts, DA_WIDTH, ts), BF16)
        vb_spec = spec((1, 1, DA_WIDTH, ts), lambda i, b: (b, i, 0, 0))
    else:
        vb_shape = jax.ShapeDtypeStruct((B, S, DA_WIDTH), BF16)
        vb_spec = row3(DA_WIDTH)
    heads_shape = jax.ShapeDtypeStruct((B, S * DA_HEADS, LANES), F32)
    heads_spec = spec((bb, ts * DA_HEADS, LANES), lambda i, b: (b, i, 0))
    in_specs = [row3(D_MODEL), modspec(1), modspec(0),
                spec((tm, LANES), lambda i, b: (i, 0)),
                spec((tm, LANES), lambda i, b: (i, 0)),
                spec((D_MODEL, DT_OFF), lambda i, b: (0, 0)),
                spec((D_MODEL, LANES), lambda i, b: (0, 0))]
    out_shapes = [jax.ShapeDtypeStruct((B, S, DA_WIDTH), BF16),
                  heads_shape,
                  jax.ShapeDtypeStruct((B, S, DA_WIDTH), BF16),
                  heads_shape,
                  vb_shape]
    out_specs = [row3(DA_WIDTH), heads_spec, row3(DA_WIDTH), heads_spec, vb_spec]
    return row3, in_specs, out_shapes, out_specs


def _proj(x, mod4, cos_t, sin_t, w_main, w_dt, *, bb, ts, v_transposed):
    B, S, _ = x.shape
    row3, in_specs, out_shapes, out_specs = _proj_specs(B, S, bb, ts, v_transposed)
    out_shapes += [jax.ShapeDtypeStruct((B, S, SSM_WIDTH), F32),
                   jax.ShapeDtypeStruct((B, S, CONV_DIM), F32),
                   jax.ShapeDtypeStruct((B, S, LANES), F32)]
    out_specs += [row3(SSM_WIDTH), row3(CONV_DIM), row3(LANES)]
    return pl.pallas_call(
        functools.partial(_proj_kernel, bb=bb, ts=ts, v_transposed=v_transposed),
        grid=(S // ts, B // bb),
        in_specs=in_specs,
        out_specs=out_specs,
        out_shape=out_shapes,
        compiler_params=_cparams("arbitrary", "arbitrary"),
        name="proj",
    )(x, mod4, mod4, cos_t, sin_t, w_main, w_dt)


def _proj_ssd(x, mod4, cos_t, sin_t, w_main, w_dt, conv_w, conv_b, dtb, alog, dsk, nw, *, ts, T):
    B, S, _ = x.shape
    row3, in_specs, out_shapes, out_specs = _proj_specs(B, S, 1, ts, True, batch_major=True)
    const = lambda r, w: pl.BlockSpec((r, w), lambda b, i: (0, 0))
    per_b = lambda r, w: pl.BlockSpec((1, r, w), lambda b, i: (b, 0, 0))
    in_specs += [const(CONV_W, CONV_DIM), const(1, CONV_DIM), const(1, LANES), const(1, LANES),
                 const(1, SSM_WIDTH), const(1, SSM_WIDTH)]
    out_shapes += [jax.ShapeDtypeStruct((B, S, SSM_WIDTH), BF16),
                   jax.ShapeDtypeStruct((B, SSM_WIDTH, SSM_STATE), F32),
                   jax.ShapeDtypeStruct((B, 8, CONV_DIM), F32)]
    out_specs += [row3(SSM_WIDTH), per_b(SSM_WIDTH, SSM_STATE), per_b(8, CONV_DIM)]
    return pl.pallas_call(
        functools.partial(_proj_ssd_kernel, ts=ts, T=T),
        grid=(B, S // ts),
        in_specs=in_specs,
        out_specs=out_specs,
        out_shape=out_shapes,
        scratch_shapes=[pltpu.VMEM((8, CONV_DIM), F32), pltpu.VMEM((SSM_WIDTH, SSM_STATE), F32)],
        compiler_params=_cparams("arbitrary", "arbitrary"),
        name="proj_ssd",
    )(x, mod4, mod4, cos_t, sin_t, w_main, w_dt, conv_w, conv_b, dtb, alog, dsk, nw)


def _lambda(lq1, lk1, lq2, lk2, lam_init):
    s1 = jnp.sum(lq1[...] * lk1[...], axis=1, keepdims=True)
    s2 = jnp.sum(lq2[...] * lk2[...], axis=1, keepdims=True)
    return jnp.exp(s1) - jnp.exp(s2) + lam_init


def _subln(o, subw, lam_init):
    ms = jnp.mean(o * o, -1, keepdims=True)
    return o * lax.rsqrt(ms + EPS) * subw * (1.0 - lam_init)


def _attn_prompt_kernel(lq1, lk1, lq2, lk2, subw_ref, bias_ref, q_ref, k_ref, vt_ref, o_ref,
                        acc1, acc2, sa1, sa2, sb1, sb2, mxa, mxb, m1, l1, m2, l2, *, tq, tk, lam_init):
    for hd in range(q_ref.shape[2] // LANES):
        _attn_prompt_head(lq1, lk1, lq2, lk2, subw_ref, bias_ref, q_ref, k_ref, vt_ref, o_ref,
                          acc1, acc2, sa1, sa2, sb1, sb2, mxa, mxb, m1, l1, m2, l2,
                          cols=slice(hd * LANES, (hd + 1) * LANES), tq=tq, tk=tk, lam_init=lam_init)


def _attn_prompt_head(lq1, lk1, lq2, lk2, subw_ref, bias_ref, q_ref, k_ref, vt_ref, o_ref,
                      acc1, acc2, sa1, sa2, sb1, sb2, mxa, mxb, m1, l1, m2, l2, *, cols, tq, tk, lam_init):
    i = pl.program_id(1)
    qt = q_ref[0, :, cols].astype(F32).T
    row = lax.broadcasted_iota(jnp.int32, (LANES, tq), 0)
    q1t = jnp.where(row < DA_HEAD_DIM, qt, 0.0).astype(BF16)
    q2t = jnp.where(row >= DA_HEAD_DIM, qt, 0.0).astype(BF16)
    acc1[...] = jnp.zeros_like(acc1)
    acc2[...] = jnp.zeros_like(acc2)
    m1[...] = jnp.full_like(m1, -jnp.inf)
    m2[...] = jnp.full_like(m2, -jnp.inf)
    l1[...] = jnp.zeros_like(l1)
    l2[...] = jnp.zeros_like(l2)
    buf_a = (sa1, sa2, mxa)
    buf_b = (sb1, sb2, mxb)

    def produce(j, buf):
        k = k_ref[0, pl.ds(pl.multiple_of(j * tk, tk), tk), cols]
        for half, qh in enumerate((q1t, q2t)):
            s = _dot(k, qh)
            buf[half][...] = s
            buf[2][half:half + 1, :] = jnp.max(s, axis=0, keepdims=True)

    def consume(j, buf, masked):
        vt = vt_ref[0, j, cols, :]
        for half, (m, l, acc) in enumerate(((m1, l1, acc1), (m2, l2, acc2))):
            st = buf[half][...]
            if masked:
                st = st + bias_ref[...]
                tile_max = jnp.max(st, axis=0, keepdims=True)
            else:
                tile_max = buf[2][half:half + 1, :]
            mo = m[...]
            mn = jnp.maximum(mo, tile_max)
            a = jnp.exp2(mo - mn)
            p = jnp.exp2(st - mn)
            l[...] = a * l[...] + jnp.sum(p, axis=0, keepdims=True)
            m[...] = mn
            acc[...] = a * acc[...] + _dot(vt, p.astype(BF16))

    nfull = (i * tq) // tk
    produce(nfull, buf_a)
    produce(0, buf_b)
    consume(nfull, buf_a, True)

    def pair(jj, carry):
        produce(2 * jj + 1, buf_a)
        consume(2 * jj, buf_b, False)
        produce(2 * jj + 2, buf_b)
        consume(2 * jj + 1, buf_a, False)
        return carry

    lax.fori_loop(0, nfull >> 1, pair, 0)

    @pl.when((nfull & 1) == 1)
    def _():
        consume(nfull - 1, buf_b, False)

    lam = _lambda(lq1, lk1, lq2, lk2, lam_init)
    ot = acc1[...] / l1[...] - lam * (acc2[...] / l2[...])
    ot = ot * lax.rsqrt(jnp.mean(ot * ot, axis=0, keepdims=True) + EPS)
    o_ref[0, :, cols] = (ot.T * subw_ref[...] * (1.0 - lam_init)).astype(BF16)


def _attn_prompt(qb, kb, vt, lam_vecs, subw, lam_init, *, tq):
    B, S, _ = qb.shape
    nkt, tk = vt.shape[1], vt.shape[3]
    assert tq == tk, "the masked tile is the aligned diagonal tile"
    key_chunk = lax.broadcasted_iota(jnp.int32, (tk, tq), 0) >> CHUNK_SHIFT
    qry_chunk = lax.broadcasted_iota(jnp.int32, (tk, tq), 1) >> CHUNK_SHIFT
    bias = jnp.where(key_chunk <= qry_chunk, 0.0, -jnp.inf).astype(F32)
    small = pl.BlockSpec((1, DA_HEAD_DIM), lambda b, i: (0, 0))
    return pl.pallas_call(
        functools.partial(_attn_prompt_kernel, tq=tq, tk=tk, lam_init=lam_init),
        grid=(B, S // tq),
        in_specs=[small, small, small, small,
                  pl.BlockSpec((1, LANES), lambda b, i: (0, 0)),
                  pl.BlockSpec((tk, tq), lambda b, i: (0, 0)),
                  pl.BlockSpec((1, tq, DA_WIDTH), lambda b, i: (b, i, 0)),
                  pl.BlockSpec((1, S, DA_WIDTH), lambda b, i: (b, 0, 0)),
                  pl.BlockSpec((1, nkt, DA_WIDTH, tk), lambda b, i: (b, 0, 0, 0))],
        out_specs=pl.BlockSpec((1, tq, DA_WIDTH), lambda b, i: (b, i, 0)),
        out_shape=jax.ShapeDtypeStruct((B, S, DA_WIDTH), BF16),
        scratch_shapes=([pltpu.VMEM((LANES, tq), F32)] * 2 + [pltpu.VMEM((tk, tq), F32)] * 4
                        + [pltpu.VMEM((2, tq), F32)] * 2 + [pltpu.VMEM((1, tq), F32)] * 4),
        compiler_params=_cparams("arbitrary", "arbitrary"),
        name="attn_prompt",
    )(*lam_vecs, subw, bias, qb, kb, vt)


def _attn_sample_kernel(lq1, lk1, lq2, lk2, subw_ref, q_ref, kn_ref, vn_ref, kp_ref, vp_ref, o_ref,
                        *, lam_init):
    s = q_ref.shape[1]
    lam = _lambda(lq1, lk1, lq2, lk2, lam_init)
    lane = lax.broadcasted_iota(jnp.int32, (s, LANES), 1)
    outs = []
    for h in range(DA_HEADS):
        cols = slice(h * LANES, (h + 1) * LANES)
        q = q_ref[0, :, cols]
        zero = jnp.zeros_like(q)
        kn = kn_ref[0, :, cols]
        vn = vn_ref[0, :, cols]
        past = kp_ref.shape[1] // DA_HEADS
        kp = kp_ref[0, pl.ds(h, past, stride=DA_HEADS), :].astype(BF16)
        vp = vp_ref[0, pl.ds(h, past, stride=DA_HEADS), :].astype(BF16)
        probs = []
        for qm in (jnp.where(lane < DA_HEAD_DIM, q, zero), jnp.where(lane >= DA_HEAD_DIM, q, zero)):
            sp = _dot_nt(qm, kp)
            sn = _dot_nt(qm, kn)
            m = jnp.maximum(jnp.max(sp, -1, keepdims=True), jnp.max(sn, -1, keepdims=True))
            pp = jnp.exp2(sp - m)
            pn = jnp.exp2(sn - m)
            l = jnp.sum(pp, -1, keepdims=True) + jnp.sum(pn, -1, keepdims=True)
            probs.append((pp / l, pn / l))
        ap = probs[0][0] - lam * probs[1][0]
        an = probs[0][1] - lam * probs[1][1]
        o = _dot(ap.astype(BF16), vp) + _dot(an.astype(BF16), vn)
        outs.append(_subln(o, subw_ref[...], lam_init))
    o_ref[0] = jnp.concatenate(outs, axis=-1).astype(BF16)


def _attn_sample(qb, kb, vb, k_past, v_past, lam_vecs, subw, lam_init):
    B, s, _ = qb.shape
    past_rows = k_past.shape[1]
    small = pl.BlockSpec((1, DA_HEAD_DIM), lambda b: (0, 0))
    new = pl.BlockSpec((1, s, DA_WIDTH), lambda b: (b, 0, 0))
    old = pl.BlockSpec((1, past_rows, LANES), lambda b: (b, 0, 0))
    return pl.pallas_call(
        functools.partial(_attn_sample_kernel, lam_init=lam_init),
        grid=(B,),
        in_specs=[small, small, small, small, pl.BlockSpec((1, LANES), lambda b: (0, 0)),
                  new, new, new, old, old],
        out_specs=new,
        out_shape=jax.ShapeDtypeStruct((B, s, DA_WIDTH), BF16),
        compiler_params=_cparams("arbitrary"),
        name="attn_sample",
    )(*lam_vecs, subw, qb, kb, vb, k_past, v_past)


def _ssd_kernel(xbc_ref, z_ref, dt_ref, cp_ref, h0_ref, cw_ref, cb_ref, dtb_ref, alog_ref, dsk_ref, nw_ref,
                y_ref, hout_ref, tail_ref, st_ref, *, t_in, T):
    j = pl.program_id(1)

    @pl.when(j == 0)
    def _():
        tail_ref[...] = cp_ref[0]
        st_ref[...] = h0_ref[0]

    def rows(ref, width):
        val = ref[0]
        if T == t_in:
            return val
        return jnp.concatenate([val, jnp.zeros((T - t_in, width), F32)], axis=0)

    y = _ssd_chunk(rows(xbc_ref, CONV_DIM), rows(z_ref, SSM_WIDTH), rows(dt_ref, LANES), tail_ref, st_ref,
                   cw_ref, cb_ref, dtb_ref, alog_ref, dsk_ref, nw_ref, T=T, n_valid=t_in)
    y_ref[0] = y[:t_in, :].astype(BF16)

    @pl.when(j == pl.num_programs(1) - 1)
    def _():
        hout_ref[0] = st_ref[...]


def _ssd_chunk(xb, zb, dt_raw, tail_ref, st_ref, cw_ref, cb_ref, dtb_ref, alog_ref, dsk_ref, nw_ref, *, T, n_valid):
    PAIR = 2 * SSM_HEAD_DIM
    xp = jnp.concatenate([tail_ref[...], xb], axis=0)
    conv = cb_ref[...] + cw_ref[CONV_W - 1:CONV_W, :] * xb
    for sft in range(1, CONV_W):
        conv = conv + cw_ref[CONV_W - 1 - sft:CONV_W - sft, :] * pltpu.roll(xp, sft, 0)[8:, :]
    if T == n_valid:
        tail_ref[...] = xb[T - 8:, :]
    xa = _silu(conv)
    xs = xa[:, :SSM_WIDTH]

    dt = _softplus(dt_raw + dtb_ref[...])
    if T != n_valid:
        valid = lax.broadcasted_iota(jnp.int32, (T, LANES), 0) < n_valid
        dt = jnp.where(valid, dt, 0.0)
    da = dt * (-jnp.exp(alog_ref[...]))

    rr = lax.broadcasted_iota(jnp.int32, (T, T), 0)
    cc = lax.broadcasted_iota(jnp.int32, (T, T), 1)
    causal = cc <= rr
    tril = jnp.where(causal, 1.0, 0.0).astype(BF16)
    acs = _dot_exact_lhs(tril, da)
    e16 = jnp.where(lax.broadcasted_iota(jnp.int32, (16, LANES), 0)
                    == lax.broadcasted_iota(jnp.int32, (16, LANES), 1), 1.0, 0.0).astype(BF16)
    acs_t = _dot_exact_lhs(e16, acs, dot=_dot_nt)

    last = acs[T - 1:T, :]
    expand = jnp.where((lax.broadcasted_iota(jnp.int32, (LANES, SSM_WIDTH), 1) >> HEAD_SHIFT)
                       == lax.broadcasted_iota(jnp.int32, (LANES, SSM_WIDTH), 0), 1.0, 0.0).astype(BF16)

    def per_head_lanes(v):
        hi, lo = _split2(v)
        return _dot(hi, expand) + _dot(lo, expand)

    xdt = xs * per_head_lanes(dt)
    xdd = xdt * per_head_lanes(jnp.exp(last - acs))
    eacs = per_head_lanes(jnp.exp(acs))

    head_of_row = lax.broadcasted_iota(jnp.int32, (SSM_WIDTH, LANES), 0) >> HEAD_SHIFT
    sel = head_of_row == lax.broadcasted_iota(jnp.int32, (SSM_WIDTH, LANES), 1)
    rowdec = jnp.sum(jnp.where(sel, jnp.exp(last), 0.0), axis=1, keepdims=True)

    lane = lax.broadcasted_iota(jnp.int32, (T, PAIR), 1)
    ys = []
    for p in range(SSM_HEADS // 2):
        g = (2 * p) // (SSM_HEADS // SSM_GROUPS)
        bm = xa[:, SSM_WIDTH + g * SSM_STATE:SSM_WIDTH + (g + 1) * SSM_STATE].astype(BF16)
        cm = xa[:, SSM_WIDTH + (SSM_GROUPS + g) * SSM_STATE:
                SSM_WIDTH + (SSM_GROUPS + g + 1) * SSM_STATE].astype(BF16)
        cb = _dot_nt(cm, bm)
        cols = slice(p * PAIR, (p + 1) * PAIR)
        xdt_p = xdt[:, cols].astype(BF16)
        yd = []
        for hh in (2 * p, 2 * p + 1):
            seg = acs[:, hh:hh + 1] - acs_t[hh:hh + 1, :]
            lmat = jnp.exp(jnp.where(causal, seg, -jnp.inf))
            yd.append(_dot((cb * lmat).astype(BF16), xdt_p))
        y_diag = jnp.where(lane < SSM_HEAD_DIM, yd[0], yd[1])
        st = st_ref[cols, :]
        y_off = _dot_nt(cm, st.astype(BF16)) * eacs[:, cols]
        ys.append(y_diag + y_off)
        st_ref[cols, :] = st * rowdec[cols, :] + _dot(xdd[:, cols].T.astype(BF16), bm)
    y = jnp.concatenate(ys, axis=-1) + xs * dsk_ref[...]
    y = y * _silu(zb)
    half = SSM_WIDTH // SSM_GROUPS
    outs = []
    for g in range(SSM_GROUPS):
        yg = y[:, g * half:(g + 1) * half]
        outs.append(yg * lax.rsqrt(jnp.mean(yg * yg, -1, keepdims=True) + EPS))
    return jnp.concatenate(outs, axis=-1) * nw_ref[...]


def _ssd(xbc, z, dt_raw, conv_prev8, h0, conv_w, conv_b, dtb, alog, dsk, nw, *, t_in, T):
    B, S, _ = xbc.shape
    row = lambda w: pl.BlockSpec((1, t_in, w), lambda b, j: (b, j, 0))
    per_b = lambda r, w: pl.BlockSpec((1, r, w), lambda b, j: (b, 0, 0))
    const = lambda r, w: pl.BlockSpec((r, w), lambda b, j: (0, 0))
    return pl.pallas_call(
        functools.partial(_ssd_kernel, t_in=t_in, T=T),
        grid=(B, S // t_in),
        in_specs=[row(CONV_DIM), row(SSM_WIDTH), row(LANES),
                  per_b(8, CONV_DIM), per_b(SSM_WIDTH, SSM_STATE),
                  const(CONV_W, CONV_DIM), const(1, CONV_DIM), const(1, LANES), const(1, LANES),
                  const(1, SSM_WIDTH), const(1, SSM_WIDTH)],
        out_specs=[row(SSM_WIDTH), per_b(SSM_WIDTH, SSM_STATE)],
        out_shape=[jax.ShapeDtypeStruct((B, S, SSM_WIDTH), BF16),
                   jax.ShapeDtypeStruct((B, SSM_WIDTH, SSM_STATE), F32)],
        scratch_shapes=[pltpu.VMEM((8, CONV_DIM), F32), pltpu.VMEM((SSM_WIDTH, SSM_STATE), F32)],
        compiler_params=_cparams("arbitrary", "arbitrary"),
        name="ssd",
    )(xbc, z, dt_raw, conv_prev8, h0, conv_w, conv_b, dtb, alog, dsk, nw)


def _router(h, rwt_ref, rb_ref):
    tm = h.shape[0]
    G = N_EXPERT_GROUPS
    logits = _dot3(rwt_ref[...], h, dot=_dot_nt)
    score = jax.nn.sigmoid(logits[:N_EXPERTS, :])
    biased = score + rb_ref[:N_EXPERTS, :]
    sc = [score[G * j:G * (j + 1), :] for j in range(EXPERTS_PER_GROUP)]
    v = [biased[G * j:G * (j + 1), :] for j in range(EXPERTS_PER_GROUP)]

    def tree(op, xs):
        while len(xs) > 1:
            xs = [op(xs[a], xs[a + 1]) for a in range(0, len(xs) - 1, 2)] + ([xs[-1]] if len(xs) % 2 else [])
        return xs[0]

    m1 = tree(jnp.maximum, v)
    first = tree(jnp.minimum, [jnp.where(v[j] == m1, float(j), float(EXPERTS_PER_GROUP))
                               for j in range(EXPERTS_PER_GROUP)])
    m2 = tree(jnp.maximum, [jnp.where(first == float(j), -jnp.inf, v[j]) for j in range(EXPERTS_PER_GROUP)])
    grp = m1 + m2

    gid = lax.broadcasted_iota(jnp.int32, (G, tm), 0)
    cnt = jnp.zeros((G, tm), jnp.int32)
    for g2 in range(G):
        row = grp[g2:g2 + 1, :]
        beats = (row > grp) | ((row == grp) & (gid > g2))
        cnt = cnt + beats.astype(jnp.int32)
    gmask = cnt < TOPK_GROUPS

    idx = [(gid * EXPERTS_PER_GROUP + j).astype(F32) for j in range(EXPERTS_PER_GROUP)]
    mv = [jnp.where(gmask, v[j], -jnp.inf) for j in range(EXPERTS_PER_GROUP)]
    w = [jnp.zeros((G, tm), F32) for _ in range(EXPERTS_PER_GROUP)]
    for _ in range(TOP_K):
        top = jnp.max(tree(jnp.maximum, mv), axis=0, keepdims=True)
        who = jnp.min(tree(jnp.minimum, [jnp.where(mv[j] == top, idx[j], float(N_EXPERTS))
                                         for j in range(EXPERTS_PER_GROUP)]), axis=0, keepdims=True)
        for j in range(EXPERTS_PER_GROUP):
            hit = idx[j] == who
            w[j] = jnp.where(hit, sc[j], w[j])
            mv[j] = jnp.where(hit, -jnp.inf, mv[j])
    tot = w[0]
    for j in range(1, EXPERTS_PER_GROUP):
        tot = tot + w[j]
    tot = jnp.sum(tot, axis=0, keepdims=True)
    gates = [w[j] / tot * ROUTED_SCALE for j in range(EXPERTS_PER_GROUP)]
    shared = jnp.where(lax.broadcasted_iota(jnp.int32, (LANES - N_EXPERTS, tm), 0) == 0, 1.0, 0.0)
    by_slot = jnp.concatenate(gates + [shared], axis=0)
    e_id = lax.broadcasted_iota(jnp.int32, (LANES, LANES), 0)
    r_id = lax.broadcasted_iota(jnp.int32, (LANES, LANES), 1)
    src_row = jnp.where(e_id < N_EXPERTS, (e_id & (G - 1)) * EXPERTS_PER_GROUP + (e_id >> 3), e_id)
    perm = jnp.where(r_id == src_row, 1.0, 0.0).astype(BF16)
    return _dot_exact_lhs(perm, by_slot)


def _mix_kernel(x_ref, att_ref, ssm_ref, g1_ref, sc2_ref, sh2_ref, wo_ref, l1g_ref, l1b_ref, rwt_ref, rb_ref,
                x1_ref, h2_ref, gate_ref, *, bb, ts, alpha):
    tm = bb * ts
    att = att_ref[...].reshape(tm, DA_WIDTH)
    ssm = ssm_ref[...].reshape(tm, SSM_WIDTH)
    mix = _dot(att, wo_ref[:DA_WIDTH, :]) + _dot(ssm, wo_ref[DA_WIDTH:, :])
    y = alpha * x_ref[...] + (1.0 + g1_ref[...]) * mix.reshape(bb, ts, D_MODEL)
    x1 = _ln(y) * l1g_ref[...] + l1b_ref[...]
    x1_ref[...] = x1
    h2 = (_ln(x1) * (1.0 + sc2_ref[...]) + sh2_ref[...]).reshape(tm, D_MODEL)
    h2_ref[...] = h2.astype(BF16)
    gate_ref[...] = _router(h2, rwt_ref, rb_ref).T


def _mix(x, att, ssm, mod4, wo, l1g, l1b, rwt, rb, *, bb, ts, alpha):
    B, S, _ = x.shape
    tm = bb * ts
    row3 = lambda w: pl.BlockSpec((bb, ts, w), lambda i, b: (b, i, 0))
    modspec = lambda which: pl.BlockSpec((bb, None, 1, D_MODEL), lambda i, b: (b, which, 0, 0))
    const = lambda r, w: pl.BlockSpec((r, w), lambda i, b: (0, 0))
    flat = lambda w: pl.BlockSpec((tm, w), lambda i, b: (b * (S // ts) + i, 0))
    return pl.pallas_call(
        functools.partial(_mix_kernel, bb=bb, ts=ts, alpha=alpha),
        grid=(S // ts, B // bb),
        in_specs=[row3(D_MODEL), row3(DA_WIDTH), row3(SSM_WIDTH), modspec(2), modspec(4), modspec(3),
                  const(D_MODEL, D_MODEL), const(1, D_MODEL), const(1, D_MODEL),
                  const(LANES, D_MODEL), const(LANES, 1)],
        out_specs=[row3(D_MODEL), flat(D_MODEL), flat(LANES)],
        out_shape=[jax.ShapeDtypeStruct((B, S, D_MODEL), F32),
                   jax.ShapeDtypeStruct((B * S, D_MODEL), BF16),
                   jax.ShapeDtypeStruct((B * S, LANES), F32)],
        compiler_params=_cparams("arbitrary", "arbitrary"),
        name="mix",
    )(x, att, ssm, mod4, mod4, mod4, wo, l1g, l1b, rwt, rb)


N_EXPERTS_ALL = N_EXPERTS + 1
EXPERTS_PER_STEP = 5


def _moe_kernel(h_ref, gate_ref, x1_ref, g2_ref, wg_ref, wu_ref, wd_ref,
                l2g_ref, l2b_ref, o_ref, acc_ref, *, bb, ts, alpha):
    tm = bb * ts
    e = pl.program_id(2)
    h = h_ref[...]

    @pl.when(e == 0)
    def _():
        acc_ref[...] = jnp.zeros_like(acc_ref)

    gate = gate_ref[...]
    lane = lax.broadcasted_iota(jnp.int32, (tm, LANES), 1)
    hids = []
    for s in range(EXPERTS_PER_STEP):
        col = e * EXPERTS_PER_STEP + s
        gcol = jnp.sum(jnp.where(lane == col, gate, 0.0), axis=1, keepdims=True)
        hid = _silu(_dot(h, wg_ref[s])) * _dot(h, wu_ref[s]) * gcol
        hids.append(hid.astype(BF16))
    hid = jnp.concatenate(hids, axis=-1)
    acc_ref[...] += _dot(hid, wd_ref[...].reshape(EXPERTS_PER_STEP * EXPERT_DIM, D_MODEL))

    @pl.when(e == pl.num_programs(2) - 1)
    def _():
        ffn = acc_ref[...].reshape(bb, ts, D_MODEL)
        y = alpha * x1_ref[...] + (1.0 + g2_ref[...]) * ffn
        o_ref[...] = _ln(y) * l2g_ref[...] + l2b_ref[...]


def _moe(h2, gate, x1, mod4, wg, wu, wd, l2g, l2b, *, bb, ts, alpha):
    B, S, _ = x1.shape
    tm = bb * ts
    row3 = lambda w: pl.BlockSpec((bb, ts, w), lambda i, b, e: (b, i, 0))
    flat = lambda w: pl.BlockSpec((tm, w), lambda i, b, e: (b * (S // ts) + i, 0))
    const = lambda r, w: pl.BlockSpec((r, w), lambda i, b, e: (0, 0))
    ew_in = pl.BlockSpec((EXPERTS_PER_STEP, D_MODEL, EXPERT_DIM), lambda i, b, e: (e, 0, 0))
    ew_out = pl.BlockSpec((EXPERTS_PER_STEP, EXPERT_DIM, D_MODEL), lambda i, b, e: (e, 0, 0))
    return pl.pallas_call(
        functools.partial(_moe_kernel, bb=bb, ts=ts, alpha=alpha),
        grid=(S // ts, B // bb, N_EXPERTS_ALL // EXPERTS_PER_STEP),
        in_specs=[flat(D_MODEL), flat(LANES), row3(D_MODEL),
                  pl.BlockSpec((bb, None, 1, D_MODEL), lambda i, b, e: (b, 5, 0, 0)),
                  ew_in, ew_in, ew_out,
                  const(1, D_MODEL), const(1, D_MODEL)],
        out_specs=row3(D_MODEL),
        out_shape=jax.ShapeDtypeStruct((B, S, D_MODEL), F32),
        scratch_shapes=[pltpu.VMEM((tm, D_MODEL), F32)],
        compiler_params=_cparams("arbitrary", "arbitrary", "arbitrary"),
        name="moe",
    )(h2, gate, x1, mod4, wg, wu, wd, l2g, l2b)


def _rope_tables(pos):
    half = DA_HEAD_DIM // 2
    inv = ROPE_THETA ** (-jnp.arange(half, dtype=F32) / half)
    ang = pos.astype(F32)[:, None] * inv[None, :]
    cos = jnp.cos(ang)
    sin = jnp.sin(ang)
    reps = LANES // DA_HEAD_DIM
    cos_t = jnp.tile(jnp.concatenate([cos, cos], -1), (1, reps))
    sin_t = jnp.tile(jnp.concatenate([-sin, sin], -1), (1, reps))
    return cos_t, sin_t


def _layer(x, mod4, pos, prm, layer_idx, depth, k_past, v_past, conv_prev, ssm_prev, sample):
    B, S, _ = x.shape
    alpha = (2 * depth) ** 0.25
    lam_init = 0.8 - 0.6 * math.exp(-0.3 * layer_idx)
    cos_t, sin_t = _rope_tables(pos)
    if sample:
        bb, ts = B, S
        cos_t = jnp.tile(cos_t, (B, 1))
        sin_t = jnp.tile(sin_t, (B, 1))
    else:
        bb, ts = 1, min(512, S)
    ssd_consts = (prm['conv_w'], prm['conv_b'], prm['dt_bias'], prm['a_log'], prm['d_skip'], prm['ssm_norm_w'])
    if sample:
        qb, k, kb, v, vb, z, xbc, dt_raw = _proj(x, mod4, cos_t, sin_t, prm['w_main'], prm['w_dt'],
                                                 bb=bb, ts=ts, v_transposed=False)
        att = _attn_sample(qb, kb, vb, k_past, v_past, prm['lam_vecs'], prm['subln_w'], lam_init)
        conv_prev8 = jnp.pad(conv_prev, ((0, 0), (8 - (CONV_W - 1), 0), (0, 0)))
        h0 = ssm_prev.reshape(B, SSM_WIDTH, SSM_STATE)
        ssm, h_last = _ssd(xbc, z, dt_raw, conv_prev8, h0, *ssd_consts, t_in=S, T=max(S, LANES))
        new_conv = jnp.concatenate([conv_prev, xbc], axis=1)[:, -(CONV_W - 1):]
    else:
        qb, k, kb, v, vb, ssm, h_last, tail = _proj_ssd(x, mod4, cos_t, sin_t, prm['w_main'], prm['w_dt'],
                                                        *ssd_consts, ts=ts, T=min(256, S))
        att = _attn_prompt(qb, kb, vb, prm['lam_vecs'], prm['subln_w'], lam_init, tq=min(512, S))
        new_conv = tail[:, -(CONV_W - 1):]
    if not sample:
        ts = min(1024, S)
    x1, h2, gate = _mix(x, att, ssm, mod4, prm['w_out'], prm['ln1_g'], prm['ln1_b'], prm['router_wt'],
                        prm['router_b'], bb=bb, ts=ts, alpha=alpha)
    y = _moe(h2, gate, x1, mod4, prm['wg'], prm['wu'], prm['wd'],
             prm['ln2_g'], prm['ln2_b'], bb=bb, ts=ts, alpha=alpha)
    return (y, k.reshape(B, S, DA_HEADS, 2 * DA_HEAD_DIM), v.reshape(B, S, DA_HEADS, 2 * DA_HEAD_DIM),
            new_conv, h_last.reshape(B, SSM_HEADS, SSM_HEAD_DIM, SSM_STATE))


CAST_EXPERTS = 8


def _cast_kernel(w_ref, s_ref, o_ref):
    last = pl.num_programs(0) - 1

    @pl.when(pl.program_id(0) < last)
    def _():
        o_ref[...] = w_ref[...].astype(BF16)

    @pl.when(pl.program_id(0) == last)
    def _():
        o_ref[0] = s_ref[...].astype(BF16)


def _cast_with_shared(w, s):
    n, r, c = w.shape
    full = n // CAST_EXPERTS
    return pl.pallas_call(
        _cast_kernel,
        grid=(full + 1,),
        in_specs=[pl.BlockSpec((CAST_EXPERTS, r, c), lambda e: (jnp.minimum(e, full - 1), 0, 0)),
                  pl.BlockSpec((r, c), lambda e: (0, 0))],
        out_specs=pl.BlockSpec((CAST_EXPERTS, r, c), lambda e: (e, 0, 0)),
        out_shape=jax.ShapeDtypeStruct((n + 1, r, c), BF16),
        compiler_params=_cparams("arbitrary"),
        name="cast_experts",
    )(w, s)


def _prep_params(w_in, lq1, lk1, lq2, lk2, subln_w, conv_w, conv_b, dt_bias, a_log, d_skip, ssm_norm_w, w_out,
                 ln1_g, ln1_b, router_w, router_bias, ewg, ewu, ewd, swg, swu, swd, ln2_g, ln2_b):
    pad8 = lambda a: jnp.pad(a.reshape(1, -1), ((0, 0), (0, LANES - a.shape[-1])))
    row = lambda a: a.reshape(1, -1)
    perm = (jnp.arange(N_EXPERTS) % N_EXPERT_GROUPS) * EXPERTS_PER_GROUP + jnp.arange(N_EXPERTS) // N_EXPERT_GROUPS
    rwt = jnp.pad(router_w.T[perm], ((0, LANES - N_EXPERTS), (0, 0)))
    rb = jnp.pad(router_bias[perm], (0, LANES - N_EXPERTS)).reshape(LANES, 1)
    with_shared = _cast_with_shared
    return {
        'w_main': w_in[:, :DT_OFF].astype(BF16),
        'w_dt': jnp.pad(w_in[:, DT_OFF:], ((0, 0), (0, LANES - SSM_HEADS))),
        'lam_vecs': (row(lq1), row(lk1), row(lq2), row(lk2)),
        'subln_w': row(subln_w),
        'conv_w': conv_w, 'conv_b': row(conv_b),
        'dt_bias': pad8(dt_bias), 'a_log': pad8(a_log),
        'd_skip': row(jnp.repeat(d_skip, SSM_HEAD_DIM)), 'ssm_norm_w': row(ssm_norm_w),
        'w_out': w_out.astype(BF16), 'ln1_g': row(ln1_g), 'ln1_b': row(ln1_b),
        'router_wt': rwt, 'router_b': rb,
        'wg': with_shared(ewg, swg), 'wu': with_shared(ewu, swu), 'wd': with_shared(ewd, swd),
        'ln2_g': row(ln2_g), 'ln2_b': row(ln2_b),
    }


def kernel(x_prompt, x_sample, c_prompt, c_sample, cache_k, cache_v, state_conv, state_ssm, w_ada, b_ada, w_in, lambda_q1, lambda_k1, lambda_q2, lambda_k2, subln_w, conv_w, conv_b, dt_bias, a_log, d_skip, ssm_norm_w, w_out, ln1_g, ln1_b, router_w, router_bias, exp_w_gate, exp_w_up, exp_w_down, sh_w_gate, sh_w_up, sh_w_down, ln2_g, ln2_b):
    depth = w_ada.shape[0]
    bp, sp, _ = x_prompt.shape
    bs, ss, _ = x_sample.shape
    past = cache_k.shape[2]
    pos_prompt = jnp.arange(sp)
    pos_sample = past + jnp.arange(ss)
    c_rows = bp + bs
    c_pad = (-c_rows) % 8
    c_all = jnp.pad(jnp.concatenate([c_prompt, c_sample], 0), ((0, c_pad), (0, 0)))
    yp, ys = x_prompt, x_sample
    outs = [[] for _ in range(8)]
    for l in range(depth):
        prm = _prep_params(w_in[l], lambda_q1[l], lambda_k1[l], lambda_q2[l], lambda_k2[l], subln_w[l],
                           conv_w[l], conv_b[l], dt_bias[l], a_log[l], d_skip[l], ssm_norm_w[l], w_out[l],
                           ln1_g[l], ln1_b[l], router_w[l], router_bias[l], exp_w_gate[l], exp_w_up[l],
                           exp_w_down[l], sh_w_gate[l], sh_w_up[l], sh_w_down[l], ln2_g[l], ln2_b[l])
        mod = _ada(c_all, w_ada[l], b_ada[l]).reshape(c_rows + c_pad, 6, 1, D_MODEL)
        yp, kp, vp, cp, hp = _layer(yp, mod[:bp], pos_prompt, prm, l, depth, None, None, None, None, False)
        ys, kn, vn, cn, hn = _layer(ys, mod[bp:c_rows], pos_sample, prm, l, depth,
                                    cache_k[l].reshape(bs, past * DA_HEADS, LANES),
                                    cache_v[l].reshape(bs, past * DA_HEADS, LANES),
                                    state_conv[l], state_ssm[l], True)
        for lst, val in zip(outs, (kp, vp, cp, hp, kn, vn, cn, hn)):
            lst.append(val)
    return (yp, ys) + tuple(jnp.stack(o) for o in outs)
```

```python
import functools
import math

import jax
import jax.numpy as jnp
from jax import lax
from jax.experimental import pallas as pl
from jax.experimental.pallas import tpu as pltpu

F32 = jnp.float32
BF16 = jnp.bfloat16

D_MODEL = 1024
CHUNK = 64
CHUNK_SHIFT = CHUNK.bit_length() - 1
EPS = 1e-5
LOG2E = math.log2(math.e)
DA_HEADS = 4
DA_HEAD_DIM = 64
DA_WIDTH = DA_HEADS * 2 * DA_HEAD_DIM
ROPE_THETA = 10000.0
SSM_HEADS = 8
SSM_HEAD_DIM = 64
HEAD_SHIFT = SSM_HEAD_DIM.bit_length() - 1
SSM_WIDTH = SSM_HEADS * SSM_HEAD_DIM
SSM_GROUPS = 2
SSM_STATE = 128
CONV_W = 4
CONV_DIM = SSM_WIDTH + 2 * SSM_GROUPS * SSM_STATE
Q_OFF = 0
K_OFF = 512
V_OFF = 1024
Z_OFF = 1536
XBC_OFF = 2048
DT_OFF = 3072
N_EXPERTS = 64
N_EXPERT_GROUPS = 8
EXPERTS_PER_GROUP = 8
SLOT_SHIFT = EXPERTS_PER_GROUP.bit_length() - 1
TOPK_GROUPS = 4
TOP_K = 8
EXPERT_DIM = 256
ROUTED_SCALE = 2.5

LANES = 128
VMEM_LIMIT = 56 * 1024 * 1024


def _cparams(*sem):
    return pltpu.CompilerParams(dimension_semantics=sem, vmem_limit_bytes=VMEM_LIMIT)


def _dot(a, b):
    return jnp.dot(a, b, preferred_element_type=F32)


def _dot_nt(a, b):
    return lax.dot_general(a, b, (((1,), (1,)), ((), ())), preferred_element_type=F32)


def _split2(a):
    hi = a.astype(BF16)
    lo = (a - hi.astype(F32)).astype(BF16)
    return hi, lo


def _split3(a):
    hi = a.astype(BF16)
    r = a - hi.astype(F32)
    mid = r.astype(BF16)
    lo = (r - mid.astype(F32)).astype(BF16)
    return hi, mid, lo


def _dot3(a, b, dot=_dot):
    ah, al = _split2(a)
    bh, bl = _split2(b)
    return dot(ah, bh) + (dot(ah, bl) + dot(al, bh))


def _dot_exact_lhs(e, a, dot=_dot):
    ah, am, al = _split3(a)
    return dot(e, ah) + (dot(e, am) + dot(e, al))


def _silu(x):
    return x * jax.nn.sigmoid(x)


def _softplus(x):
    return jnp.maximum(x, 0.0) + jnp.log1p(jnp.exp(-jnp.abs(x)))


def _ln(x):
    mu = jnp.mean(x, -1, keepdims=True)
    xc = x - mu
    var = jnp.mean(xc * xc, -1, keepdims=True)
    return xc * lax.rsqrt(var + EPS)


def _ada_kernel(c_ref, w_ref, b_ref, o_ref):
    o_ref[...] = _dot3(_silu(c_ref[...]), w_ref[...]) + b_ref[...]


def _ada(c, w_ada, b_ada):
    rows = c.shape[0]
    n = w_ada.shape[1]
    tn = 1024
    return pl.pallas_call(
        _ada_kernel,
        grid=(n // tn,),
        in_specs=[pl.BlockSpec((rows, D_MODEL), lambda i: (0, 0)),
                  pl.BlockSpec((D_MODEL, tn), lambda i: (0, i)),
                  pl.BlockSpec((1, tn), lambda i: (0, i))],
        out_specs=pl.BlockSpec((rows, tn), lambda i: (0, i)),
        out_shape=jax.ShapeDtypeStruct((rows, n), F32),
        compiler_params=_cparams("arbitrary"),
        name="ada",
    )(c, w_ada, b_ada.reshape(1, n))


def _proj_hidden(x_ref, sc_ref, sh_ref, tm):
    h = _ln(x_ref[...]) * (1.0 + sc_ref[...]) + sh_ref[...]
    return h.reshape(tm, D_MODEL)


def _proj_ssm_inputs(h, w_ref, wdt_ref):
    hb = h.astype(BF16)
    return _dot(hb, w_ref[:, Z_OFF:XBC_OFF]), _dot(hb, w_ref[:, XBC_OFF:DT_OFF]), _dot3(h, wdt_ref[...])


def _proj_qkv(h, cos_ref, sin_ref, w_ref, q_ref, k_ref, kb_ref, v_ref, vb_ref, *, bb, ts, v_transposed):
    tm = bb * ts
    hb = h.astype(BF16)
    cos = jnp.tile(cos_ref[...], (1, DA_WIDTH // LANES))
    sin = jnp.tile(sin_ref[...], (1, DA_WIDTH // LANES))
    lane = lax.broadcasted_iota(jnp.int32, (tm, DA_WIDTH), 1)
    first = (lane & (DA_HEAD_DIM - 1)) < (DA_HEAD_DIM // 2)

    def rope(t):
        rot = jnp.where(first, pltpu.roll(t, DA_WIDTH - DA_HEAD_DIM // 2, 1),
                        pltpu.roll(t, DA_HEAD_DIM // 2, 1))
        return t * cos + rot * sin

    q = rope(_dot(hb, w_ref[:, Q_OFF:K_OFF])) * (DA_HEAD_DIM ** -0.5 * LOG2E)
    q_ref[...] = q.reshape(bb, ts, DA_WIDTH).astype(BF16)
    def store_heads(ref, t):
        for hd in range(DA_HEADS):
            ref[:, pl.ds(hd, ts, stride=DA_HEADS), :] = t[:, :, hd * LANES:(hd + 1) * LANES]

    k = rope(_dot(hb, w_ref[:, K_OFF:V_OFF])).reshape(bb, ts, DA_WIDTH)
    store_heads(k_ref, k)
    kb_ref[...] = k.astype(BF16)
    v = _dot(hb, w_ref[:, V_OFF:Z_OFF]).reshape(bb, ts, DA_WIDTH)
    store_heads(v_ref, v)
    if v_transposed:
        vb_ref[0, 0] = v.reshape(tm, DA_WIDTH).T.astype(BF16)
    else:
        vb_ref[...] = v.astype(BF16)


def _proj_kernel(x_ref, sc_ref, sh_ref, cos_ref, sin_ref, w_ref, wdt_ref,
                 q_ref, k_ref, kb_ref, v_ref, vb_ref, z_ref, xbc_ref, dt_ref, *, bb, ts, v_transposed):
    h = _proj_hidden(x_ref, sc_ref, sh_ref, bb * ts)
    _proj_qkv(h, cos_ref, sin_ref, w_ref, q_ref, k_ref, kb_ref, v_ref, vb_ref,
              bb=bb, ts=ts, v_transposed=v_transposed)
    z, xbc, dt_raw = _proj_ssm_inputs(h, w_ref, wdt_ref)
    z_ref[...] = z.reshape(bb, ts, SSM_WIDTH)
    xbc_ref[...] = xbc.reshape(bb, ts, CONV_DIM)
    dt_ref[...] = dt_raw.reshape(bb, ts, LANES)


def _proj_ssd_kernel(x_ref, sc_ref, sh_ref, cos_ref, sin_ref, w_ref, wdt_ref,
                     cw_ref, cb_ref, dtb_ref, alog_ref, dsk_ref, nw_ref,
                     q_ref, k_ref, kb_ref, v_ref, vb_ref, y_ref, hout_ref, tout_ref, tail_ref, st_ref, *, ts, T):
    @pl.when(pl.program_id(1) == 0)
    def _():
        tail_ref[...] = jnp.zeros_like(tail_ref)
        st_ref[...] = jnp.zeros_like(st_ref)

    h = _proj_hidden(x_ref, sc_ref, sh_ref, ts)
    z, xbc, dt_raw = _proj_ssm_inputs(h, w_ref, wdt_ref)
    for c in range(ts // T):
        rows = slice(c * T, (c + 1) * T)
        y = _ssd_chunk(xbc[rows], z[rows], dt_raw[rows], tail_ref, st_ref,
                       cw_ref, cb_ref, dtb_ref, alog_ref, dsk_ref, nw_ref, T=T, n_valid=T)
        y_ref[0, rows, :] = y.astype(BF16)
    _proj_qkv(h, cos_ref, sin_ref, w_ref, q_ref, k_ref, kb_ref, v_ref, vb_ref, bb=1, ts=ts, v_transposed=True)
    @pl.when(pl.program_id(1) == pl.num_programs(1) - 1)
    def _():
        hout_ref[0] = st_ref[...]
        tout_ref[0] = tail_ref[...]


def _proj_specs(B, S, bb, ts, v_transposed, batch_major=False):
    tm = bb * ts
    spec = lambda shape, f: pl.BlockSpec(shape, (lambda b, i: f(i, b)) if batch_major else f)
    row3 = lambda w: spec((bb, ts, w), lambda i, b: (b, i, 0))
    modspec = lambda which: spec((bb, None, 1, D_MODEL), lambda i, b: (b, which, 0, 0))
    if v_transposed:
        vb_shape = jax.ShapeDtypeStruct((B, S // ts, DA_WIDTH, ts), BF16)
        vb_spec = spec((1, 1, DA_WIDTH, ts), lambda i, b: (b, i, 0, 0))
    else:
        vb_shape = jax.ShapeDtypeStruct((B, S, DA_WIDTH), BF16)
        vb_spec = row3(DA_WIDTH)
    heads_shape = jax.ShapeDtypeStruct((B, S * DA_HEADS, LANES), F32)
    heads_spec = spec((bb, ts * DA_HEADS, LANES), lambda i, b: (b, i, 0))
    in_specs = [row3(D_MODEL), modspec(1), modspec(0),
                spec((tm, LANES), lambda i, b: (i, 0)),
                spec((tm, LANES), lambda i, b: (i, 0)),
                spec((D_MODEL, DT_OFF), lambda i, b: (0, 0)),
                spec((D_MODEL, LANES), lambda i, b: (0, 0))]
    out_shapes = [jax.ShapeDtypeStruct((B, S, DA_WIDTH), BF16),
                  heads_shape,
                  jax.ShapeDtypeStruct((B, S, DA_WIDTH), BF16),
                  heads_shape,
                  vb_shape]
    out_specs = [row3(DA_WIDTH), heads_spec, row3(DA_WIDTH), heads_spec, vb_spec]
    return row3, in_specs, out_shapes, out_specs


def _proj(x, mod4, cos_t, sin_t, w_main, w_dt, *, bb, ts, v_transposed):
    B, S, _ = x.shape
    row3, in_specs, out_shapes, out_specs = _proj_specs(B, S, bb, ts, v_transposed)
    out_shapes += [jax.ShapeDtypeStruct((B, S, SSM_WIDTH), F32),
                   jax.ShapeDtypeStruct((B, S, CONV_DIM), F32),
                   jax.ShapeDtypeStruct((B, S, LANES), F32)]
    out_specs += [row3(SSM_WIDTH), row3(CONV_DIM), row3(LANES)]
    return pl.pallas_call(
        functools.partial(_proj_kernel, bb=bb, ts=ts, v_transposed=v_transposed),
        grid=(S // ts, B // bb),
        in_specs=in_specs,
        out_specs=out_specs,
        out_shape=out_shapes,
        compiler_params=_cparams("arbitrary", "arbitrary"),
        name="proj",
    )(x, mod4, mod4, cos_t, sin_t, w_main, w_dt)


def _proj_ssd(x, mod4, cos_t, sin_t, w_main, w_dt, conv_w, conv_b, dtb, alog, dsk, nw, *, ts, T):
    B, S, _ = x.shape
    row3, in_specs, out_shapes, out_specs = _proj_specs(B, S, 1, ts, True, batch_major=True)
    const = lambda r, w: pl.BlockSpec((r, w), lambda b, i: (0, 0))
    per_b = lambda r, w: pl.BlockSpec((1, r, w), lambda b, i: (b, 0, 0))
    in_specs += [const(CONV_W, CONV_DIM), const(1, CONV_DIM), const(1, LANES), const(1, LANES),
                 const(1, SSM_WIDTH), const(1, SSM_WIDTH)]
    out_shapes += [jax.ShapeDtypeStruct((B, S, SSM_WIDTH), BF16),
                   jax.ShapeDtypeStruct((B, SSM_WIDTH, SSM_STATE), F32),
                   jax.ShapeDtypeStruct((B, 8, CONV_DIM), F32)]
    out_specs += [row3(SSM_WIDTH), per_b(SSM_WIDTH, SSM_STATE), per_b(8, CONV_DIM)]
    return pl.pallas_call(
        functools.partial(_proj_ssd_kernel, ts=ts, T=T),
        grid=(B, S // ts),
        in_specs=in_specs,
        out_specs=out_specs,
        out_shape=out_shapes,
        scratch_shapes=[pltpu.VMEM((8, CONV_DIM), F32), pltpu.VMEM((SSM_WIDTH, SSM_STATE), F32)],
        compiler_params=_cparams("arbitrary", "arbitrary"),
        name="proj_ssd",
    )(x, mod4, mod4, cos_t, sin_t, w_main, w_dt, conv_w, conv_b, dtb, alog, dsk, nw)


def _lambda(lq1, lk1, lq2, lk2, lam_init):
    s1 = jnp.sum(lq1[...] * lk1[...], axis=1, keepdims=True)
    s2 = jnp.sum(lq2[...] * lk2[...], axis=1, keepdims=True)
    return jnp.exp(s1) - jnp.exp(s2) + lam_init


def _subln(o, subw, lam_init):
    ms = jnp.mean(o * o, -1, keepdims=True)
    return o * lax.rsqrt(ms + EPS) * subw * (1.0 - lam_init)


def _attn_prompt_kernel(lq1, lk1, lq2, lk2, subw_ref, bias_ref, q_ref, k_ref, vt_ref, o_ref,
                        acc1, acc2, sa1, sa2, sb1, sb2, mxa, mxb, m1, l1, m2, l2, *, tq, tk, lam_init):
    for hd in range(q_ref.shape[2] // LANES):
        _attn_prompt_head(lq1, lk1, lq2, lk2, subw_ref, bias_ref, q_ref, k_ref, vt_ref, o_ref,
                          acc1, acc2, sa1, sa2, sb1, sb2, mxa, mxb, m1, l1, m2, l2,
                          cols=slice(hd * LANES, (hd + 1) * LANES), tq=tq, tk=tk, lam_init=lam_init)


def _attn_prompt_head(lq1, lk1, lq2, lk2, subw_ref, bias_ref, q_ref, k_ref, vt_ref, o_ref,
                      acc1, acc2, sa1, sa2, sb1, sb2, mxa, mxb, m1, l1, m2, l2, *, cols, tq, tk, lam_init):
    i = pl.program_id(1)
    qt = q_ref[0, :, cols].astype(F32).T
    row = lax.broadcasted_iota(jnp.int32, (LANES, tq), 0)
    q1t = jnp.where(row < DA_HEAD_DIM, qt, 0.0).astype(BF16)
    q2t = jnp.where(row >= DA_HEAD_DIM, qt, 0.0).astype(BF16)
    acc1[...] = jnp.zeros_like(acc1)
    acc2[...] = jnp.zeros_like(acc2)
    m1[...] = jnp.full_like(m1, -jnp.inf)
    m2[...] = jnp.full_like(m2, -jnp.inf)
    l1[...] = jnp.zeros_like(l1)
    l2[...] = jnp.zeros_like(l2)
    buf_a = (sa1, sa2, mxa)
    buf_b = (sb1, sb2, mxb)

    def produce(j, buf):
        k = k_ref[0, pl.ds(pl.multiple_of(j * tk, tk), tk), cols]
        for half, qh in enumerate((q1t, q2t)):
            s = _dot(k, qh)
            buf[half][...] = s
            buf[2][half:half + 1, :] = jnp.max(s, axis=0, keepdims=True)

    def consume(j, buf, masked):
        vt = vt_ref[0, j, cols, :]
        for half, (m, l, acc) in enumerate(((m1, l1, acc1), (m2, l2, acc2))):
            st = buf[half][...]
            if masked:
                st = st + bias_ref[...]
                tile_max = jnp.max(st, axis=0, keepdims=True)
            else:
                tile_max = buf[2][half:half + 1, :]
            mo = m[...]
            mn = jnp.maximum(mo, tile_max)
            a = jnp.exp2(mo - mn)
            p = jnp.exp2(st - mn)
            l[...] = a * l[...] + jnp.sum(p, axis=0, keepdims=True)
            m[...] = mn
            acc[...] = a * acc[...] + _dot(vt, p.astype(BF16))

    nfull = (i * tq) // tk
    produce(nfull, buf_a)
    produce(0, buf_b)
    consume(nfull, buf_a, True)

    def pair(jj, carry):
        produce(2 * jj + 1, buf_a)
        consume(2 * jj, buf_b, False)
        produce(2 * jj + 2, buf_b)
        consume(2 * jj + 1, buf_a, False)
        return carry

    lax.fori_loop(0, nfull >> 1, pair, 0)

    @pl.when((nfull & 1) == 1)
    def _():
        consume(nfull - 1, buf_b, False)

    lam = _lambda(lq1, lk1, lq2, lk2, lam_init)
    ot = acc1[...] / l1[...] - lam * (acc2[...] / l2[...])
    ot = ot * lax.rsqrt(jnp.mean(ot * ot, axis=0, keepdims=True) + EPS)
    o_ref[0, :, cols] = (ot.T * subw_ref[...] * (1.0 - lam_init)).astype(BF16)


def _attn_prompt(qb, kb, vt, lam_vecs, subw, lam_init, *, tq):
    B, S, _ = qb.shape
    nkt, tk = vt.shape[1], vt.shape[3]
    assert tq == tk, "the masked tile is the aligned diagonal tile"
    key_chunk = lax.broadcasted_iota(jnp.int32, (tk, tq), 0) >> CHUNK_SHIFT
    qry_chunk = lax.broadcasted_iota(jnp.int32, (tk, tq), 1) >> CHUNK_SHIFT
    bias = jnp.where(key_chunk <= qry_chunk, 0.0, -jnp.inf).astype(F32)
    small = pl.BlockSpec((1, DA_HEAD_DIM), lambda b, i: (0, 0))
    return pl.pallas_call(
        functools.partial(_attn_prompt_kernel, tq=tq, tk=tk, lam_init=lam_init),
        grid=(B, S // tq),
        in_specs=[small, small, small, small,
                  pl.BlockSpec((1, LANES), lambda b, i: (0, 0)),
                  pl.BlockSpec((tk, tq), lambda b, i: (0, 0)),
                  pl.BlockSpec((1, tq, DA_WIDTH), lambda b, i: (b, i, 0)),
                  pl.BlockSpec((1, S, DA_WIDTH), lambda b, i: (b, 0, 0)),
                  pl.BlockSpec((1, nkt, DA_WIDTH, tk), lambda b, i: (b, 0, 0, 0))],
        out_specs=pl.BlockSpec((1, tq, DA_WIDTH), lambda b, i: (b, i, 0)),
        out_shape=jax.ShapeDtypeStruct((B, S, DA_WIDTH), BF16),
        scratch_shapes=([pltpu.VMEM((LANES, tq), F32)] * 2 + [pltpu.VMEM((tk, tq), F32)] * 4
                        + [pltpu.VMEM((2, tq), F32)] * 2 + [pltpu.VMEM((1, tq), F32)] * 4),
        compiler_params=_cparams("arbitrary", "arbitrary"),
        name="attn_prompt",
    )(*lam_vecs, subw, bias, qb, kb, vt)


def _attn_sample_kernel(lq1, lk1, lq2, lk2, subw_ref, q_ref, kn_ref, vn_ref, kp_ref, vp_ref, o_ref,
                        *, lam_init):
    s = q_ref.shape[1]
    lam = _lambda(lq1, lk1, lq2, lk2, lam_init)
    lane = lax.broadcasted_iota(jnp.int32, (s, LANES), 1)
    outs = []
    for h in range(DA_HEADS):
        cols = slice(h * LANES, (h + 1) * LANES)
        q = q_ref[0, :, cols]
        zero = jnp.zeros_like(q)
        kn = kn_ref[0, :, cols]
        vn = vn_ref[0, :, cols]
        past = kp_ref.shape[1] // DA_HEADS
        kp = kp_ref[0, pl.ds(h, past, stride=DA_HEADS), :].astype(BF16)
        vp = vp_ref[0, pl.ds(h, past, stride=DA_HEADS), :].astype(BF16)
        probs = []
        for qm in (jnp.where(lane < DA_HEAD_DIM, q, zero), jnp.where(lane >= DA_HEAD_DIM, q, zero)):
            sp = _dot_nt(qm, kp)
            sn = _dot_nt(qm, kn)
            m = jnp.maximum(jnp.max(sp, -1, keepdims=True), jnp.max(sn, -1, keepdims=True))
            pp = jnp.exp2(sp - m)
            pn = jnp.exp2(sn - m)
            l = jnp.sum(pp, -1, keepdims=True) + jnp.sum(pn, -1, keepdims=True)
            probs.append((pp / l, pn / l))
        ap = probs[0][0] - lam * probs[1][0]
        an = probs[0][1] - lam * probs[1][1]
        o = _dot(ap.astype(BF16), vp) + _dot(an.astype(BF16), vn)
        outs.append(_subln(o, subw_ref[...], lam_init))
    o_ref[0] = jnp.concatenate(outs, axis=-1).astype(BF16)


def _attn_sample(qb, kb, vb, k_past, v_past, lam_vecs, subw, lam_init):
    B, s, _ = qb.shape
    past_rows = k_past.shape[1]
    small = pl.BlockSpec((1, DA_HEAD_DIM), lambda b: (0, 0))
    new = pl.BlockSpec((1, s, DA_WIDTH), lambda b: (b, 0, 0))
    old = pl.BlockSpec((1, past_rows, LANES), lambda b: (b, 0, 0))
    return pl.pallas_call(
        functools.partial(_attn_sample_kernel, lam_init=lam_init),
        grid=(B,),
        in_specs=[small, small, small, small, pl.BlockSpec((1, LANES), lambda b: (0, 0)),
                  new, new, new, old, old],
        out_specs=new,
        out_shape=jax.ShapeDtypeStruct((B, s, DA_WIDTH), BF16),
        compiler_params=_cparams("arbitrary"),
        name="attn_sample",
    )(*lam_vecs, subw, qb, kb, vb, k_past, v_past)


def _ssd_kernel(xbc_ref, z_ref, dt_ref, cp_ref, h0_ref, cw_ref, cb_ref, dtb_ref, alog_ref, dsk_ref, nw_ref,
                y_ref, hout_ref, tail_ref, st_ref, *, t_in, T):
    j = pl.program_id(1)

    @pl.when(j == 0)
    def _():
        tail_ref[...] = cp_ref[...]
        st_ref[...] = h0_ref[...]

    def rows(ref, r, width):
        val = ref[r]
        if T == t_in:
            return val
        return jnp.concatenate([val, jnp.zeros((T - t_in, width), F32)], axis=0)

    for r in range(xbc_ref.shape[0]):
        y = _ssd_chunk(rows(xbc_ref, r, CONV_DIM), rows(z_ref, r, SSM_WIDTH), rows(dt_ref, r, LANES),
                       tail_ref.at[r], st_ref.at[r],
                       cw_ref, cb_ref, dtb_ref, alog_ref, dsk_ref, nw_ref, T=T, n_valid=t_in)
        y_ref[r] = y[:t_in, :].astype(BF16)

    @pl.when(j == pl.num_programs(1) - 1)
    def _():
        hout_ref[...] = st_ref[...]


def _ssd_chunk(xb, zb, dt_raw, tail_ref, st_ref, cw_ref, cb_ref, dtb_ref, alog_ref, dsk_ref, nw_ref, *, T, n_valid):
    PAIR = 2 * SSM_HEAD_DIM
    xp = jnp.concatenate([tail_ref[...], xb], axis=0)
    conv = cb_ref[...] + cw_ref[CONV_W - 1:CONV_W, :] * xb
    for sft in range(1, CONV_W):
        conv = conv + cw_ref[CONV_W - 1 - sft:CONV_W - sft, :] * pltpu.roll(xp, sft, 0)[8:, :]
    if T == n_valid:
        tail_ref[...] = xb[T - 8:, :]
    xa = _silu(conv)
    xs = xa[:, :SSM_WIDTH]

    dt = _softplus(dt_raw + dtb_ref[...])
    if T != n_valid:
        valid = lax.broadcasted_iota(jnp.int32, (T, LANES), 0) < n_valid
        dt = jnp.where(valid, dt, 0.0)
    da = dt * (-jnp.exp(alog_ref[...]))

    rr = lax.broadcasted_iota(jnp.int32, (T, T), 0)
    cc = lax.broadcasted_iota(jnp.int32, (T, T), 1)
    causal = cc <= rr
    tril = jnp.where(causal, 1.0, 0.0).astype(BF16)
    acs = _dot_exact_lhs(tril, da)
    e16 = jnp.where(lax.broadcasted_iota(jnp.int32, (16, LANES), 0)
                    == lax.broadcasted_iota(jnp.int32, (16, LANES), 1), 1.0, 0.0).astype(BF16)
    acs_t = _dot_exact_lhs(e16, acs, dot=_dot_nt)

    last = acs[T - 1:T, :]
    expand = jnp.where((lax.broadcasted_iota(jnp.int32, (LANES, SSM_WIDTH), 1) >> HEAD_SHIFT)
                       == lax.broadcasted_iota(jnp.int32, (LANES, SSM_WIDTH), 0), 1.0, 0.0).astype(BF16)

    def per_head_lanes(v):
        hi, lo = _split2(v)
        return _dot(hi, expand) + _dot(lo, expand)

    xdt = xs * per_head_lanes(dt)
    xdd = xdt * per_head_lanes(jnp.exp(last - acs))
    eacs = per_head_lanes(jnp.exp(acs))

    head_of_row = lax.broadcasted_iota(jnp.int32, (SSM_WIDTH, LANES), 0) >> HEAD_SHIFT
    sel = head_of_row == lax.broadcasted_iota(jnp.int32, (SSM_WIDTH, LANES), 1)
    rowdec = jnp.sum(jnp.where(sel, jnp.exp(last), 0.0), axis=1, keepdims=True)

    lane = lax.broadcasted_iota(jnp.int32, (T, PAIR), 1)
    ys = []
    for p in range(SSM_HEADS // 2):
        g = (2 * p) // (SSM_HEADS // SSM_GROUPS)
        bm = xa[:, SSM_WIDTH + g * SSM_STATE:SSM_WIDTH + (g + 1) * SSM_STATE].astype(BF16)
        cm = xa[:, SSM_WIDTH + (SSM_GROUPS + g) * SSM_STATE:
                SSM_WIDTH + (SSM_GROUPS + g + 1) * SSM_STATE].astype(BF16)
        cb = _dot_nt(cm, bm)
        cols = slice(p * PAIR, (p + 1) * PAIR)
        xdt_p = xdt[:, cols].astype(BF16)
        yd = []
        for hh in (2 * p, 2 * p + 1):
            seg = acs[:, hh:hh + 1] - acs_t[hh:hh + 1, :]
            lmat = jnp.exp(jnp.where(causal, seg, -jnp.inf))
            yd.append(_dot((cb * lmat).astype(BF16), xdt_p))
        y_diag = jnp.where(lane < SSM_HEAD_DIM, yd[0], yd[1])
        st = st_ref[cols, :]
        y_off = _dot_nt(cm, st.astype(BF16)) * eacs[:, cols]
        ys.append(y_diag + y_off)
        st_ref[cols, :] = st * rowdec[cols, :] + _dot(xdd[:, cols].T.astype(BF16), bm)
    y = jnp.concatenate(ys, axis=-1) + xs * dsk_ref[...]
    y = y * _silu(zb)
    half = SSM_WIDTH // SSM_GROUPS
    outs = []
    for g in range(SSM_GROUPS):
        yg = y[:, g * half:(g + 1) * half]
        outs.append(yg * lax.rsqrt(jnp.mean(yg * yg, -1, keepdims=True) + EPS))
    return jnp.concatenate(outs, axis=-1) * nw_ref[...]


def _ssd(xbc, z, dt_raw, conv_prev8, h0, conv_w, conv_b, dtb, alog, dsk, nw, *, t_in, T, rb):
    B, S, _ = xbc.shape
    row = lambda w: pl.BlockSpec((rb, t_in, w), lambda b, j: (b, j, 0))
    per_b = lambda r, w: pl.BlockSpec((rb, r, w), lambda b, j: (b, 0, 0))
    const = lambda r, w: pl.BlockSpec((r, w), lambda b, j: (0, 0))
    return pl.pallas_call(
        functools.partial(_ssd_kernel, t_in=t_in, T=T),
        grid=(B // rb, S // t_in),
        in_specs=[row(CONV_DIM), row(SSM_WIDTH), row(LANES),
                  per_b(8, CONV_DIM), per_b(SSM_WIDTH, SSM_STATE),
                  const(CONV_W, CONV_DIM), const(1, CONV_DIM), const(1, LANES), const(1, LANES),
                  const(1, SSM_WIDTH), const(1, SSM_WIDTH)],
        out_specs=[row(SSM_WIDTH), per_b(SSM_WIDTH, SSM_STATE)],
        out_shape=[jax.ShapeDtypeStruct((B, S, SSM_WIDTH), BF16),
                   jax.ShapeDtypeStruct((B, SSM_WIDTH, SSM_STATE), F32)],
        scratch_shapes=[pltpu.VMEM((rb, 8, CONV_DIM), F32), pltpu.VMEM((rb, SSM_WIDTH, SSM_STATE), F32)],
        compiler_params=_cparams("arbitrary", "arbitrary"),
        name="ssd",
    )(xbc, z, dt_raw, conv_prev8, h0, conv_w, conv_b, dtb, alog, dsk, nw)


def _router(h, rwt_ref, rb_ref):
    tm = h.shape[0]
    G = N_EXPERT_GROUPS
    logits = _dot3(rwt_ref[...], h, dot=_dot_nt)
    score = jax.nn.sigmoid(logits[:N_EXPERTS, :])
    biased = score + rb_ref[:N_EXPERTS, :]
    sc = [score[G * j:G * (j + 1), :] for j in range(EXPERTS_PER_GROUP)]
    v = [biased[G * j:G * (j + 1), :] for j in range(EXPERTS_PER_GROUP)]

    def tree(op, xs):
        while len(xs) > 1:
            xs = [op(xs[a], xs[a + 1]) for a in range(0, len(xs) - 1, 2)] + ([xs[-1]] if len(xs) % 2 else [])
        return xs[0]

    m1 = tree(jnp.maximum, v)
    first = tree(jnp.minimum, [jnp.where(v[j] == m1, float(j), float(EXPERTS_PER_GROUP))
                               for j in range(EXPERTS_PER_GROUP)])
    m2 = tree(jnp.maximum, [jnp.where(first == float(j), -jnp.inf, v[j]) for j in range(EXPERTS_PER_GROUP)])
    grp = m1 + m2

    gid = lax.broadcasted_iota(jnp.int32, (G, tm), 0)
    cnt = jnp.zeros((G, tm), jnp.int32)
    for g2 in range(G):
        row = grp[g2:g2 + 1, :]
        beats = (row > grp) | ((row == grp) & (gid > g2))
        cnt = cnt + beats.astype(jnp.int32)
    gmask = cnt < TOPK_GROUPS

    idx = [(gid * EXPERTS_PER_GROUP + j).astype(F32) for j in range(EXPERTS_PER_GROUP)]
    mv = [jnp.where(gmask, v[j], -jnp.inf) for j in range(EXPERTS_PER_GROUP)]
    w = [jnp.zeros((G, tm), F32) for _ in range(EXPERTS_PER_GROUP)]
    for _ in range(TOP_K):
        top = jnp.max(tree(jnp.maximum, mv), axis=0, keepdims=True)
        who = jnp.min(tree(jnp.minimum, [jnp.where(mv[j] == top, idx[j], float(N_EXPERTS))
                                         for j in range(EXPERTS_PER_GROUP)]), axis=0, keepdims=True)
        for j in range(EXPERTS_PER_GROUP):
            hit = idx[j] == who
            w[j] = jnp.where(hit, sc[j], w[j])
            mv[j] = jnp.where(hit, -jnp.inf, mv[j])
    tot = w[0]
    for j in range(1, EXPERTS_PER_GROUP):
        tot = tot + w[j]
    tot = jnp.sum(tot, axis=0, keepdims=True)
    gates = [w[j] / tot * ROUTED_SCALE for j in range(EXPERTS_PER_GROUP)]
    shared = jnp.where(lax.broadcasted_iota(jnp.int32, (LANES - N_EXPERTS, tm), 0) == 0, 1.0, 0.0)
    by_slot = jnp.concatenate(gates + [shared], axis=0)
    e_id = lax.broadcasted_iota(jnp.int32, (LANES, LANES), 0)
    r_id = lax.broadcasted_iota(jnp.int32, (LANES, LANES), 1)
    src_row = jnp.where(e_id < N_EXPERTS,
                        (e_id & (EXPERTS_PER_GROUP - 1)) * G + (e_id >> SLOT_SHIFT), e_id)
    perm = jnp.where(r_id == src_row, 1.0, 0.0).astype(BF16)
    return _dot_exact_lhs(perm, by_slot)


def _mix_kernel(x_ref, att_ref, ssm_ref, g1_ref, sc2_ref, sh2_ref, wo_ref, l1g_ref, l1b_ref, rwt_ref, rb_ref,
                x1_ref, h2_ref, gate_ref, *, bb, ts, alpha):
    tm = bb * ts
    att = att_ref[...].reshape(tm, DA_WIDTH)
    ssm = ssm_ref[...].reshape(tm, SSM_WIDTH)
    mix = _dot(att, wo_ref[:DA_WIDTH, :]) + _dot(ssm, wo_ref[DA_WIDTH:, :])
    y = alpha * x_ref[...] + (1.0 + g1_ref[...]) * mix.reshape(bb, ts, D_MODEL)
    x1 = _ln(y) * l1g_ref[...] + l1b_ref[...]
    x1_ref[...] = x1
    h2 = (_ln(x1) * (1.0 + sc2_ref[...]) + sh2_ref[...]).reshape(tm, D_MODEL)
    h2_ref[...] = h2.astype(BF16)
    gate_ref[...] = _router(h2, rwt_ref, rb_ref).T


def _mix(x, att, ssm, mod4, wo, l1g, l1b, rwt, rb, *, bb, ts, alpha):
    B, S, _ = x.shape
    tm = bb * ts
    row3 = lambda w: pl.BlockSpec((bb, ts, w), lambda i, b: (b, i, 0))
    modspec = lambda which: pl.BlockSpec((bb, None, 1, D_MODEL), lambda i, b: (b, which, 0, 0))
    const = lambda r, w: pl.BlockSpec((r, w), lambda i, b: (0, 0))
    flat = lambda w: pl.BlockSpec((tm, w), lambda i, b: (b * (S // ts) + i, 0))
    return pl.pallas_call(
        functools.partial(_mix_kernel, bb=bb, ts=ts, alpha=alpha),
        grid=(S // ts, B // bb),
        in_specs=[row3(D_MODEL), row3(DA_WIDTH), row3(SSM_WIDTH), modspec(2), modspec(4), modspec(3),
                  const(D_MODEL, D_MODEL), const(1, D_MODEL), const(1, D_MODEL),
                  const(LANES, D_MODEL), const(LANES, 1)],
        out_specs=[row3(D_MODEL), flat(D_MODEL), flat(LANES)],
        out_shape=[jax.ShapeDtypeStruct((B, S, D_MODEL), F32),
                   jax.ShapeDtypeStruct((B * S, D_MODEL), BF16),
                   jax.ShapeDtypeStruct((B * S, LANES), F32)],
        compiler_params=_cparams("arbitrary", "arbitrary"),
        name="mix",
    )(x, att, ssm, mod4, mod4, mod4, wo, l1g, l1b, rwt, rb)


N_EXPERTS_ALL = N_EXPERTS + 1
EXPERTS_PER_STEP = 5


def _moe_kernel(h_ref, gate_ref, x1_ref, g2_ref, wg_ref, wu_ref, wd_ref,
                l2g_ref, l2b_ref, o_ref, acc_ref, *, bb, ts, alpha):
    tm = bb * ts
    e = pl.program_id(2)
    h = h_ref[...]

    @pl.when(e == 0)
    def _():
        acc_ref[...] = jnp.zeros_like(acc_ref)

    gate = gate_ref[...]
    lane = lax.broadcasted_iota(jnp.int32, (tm, LANES), 1)
    hids = []
    for s in range(EXPERTS_PER_STEP):
        col = e * EXPERTS_PER_STEP + s
        gcol = jnp.sum(jnp.where(lane == col, gate, 0.0), axis=1, keepdims=True)
        hid = _silu(_dot(h, wg_ref[s])) * _dot(h, wu_ref[s]) * gcol
        hids.append(hid.astype(BF16))
    hid = jnp.concatenate(hids, axis=-1)
    acc_ref[...] += _dot(hid, wd_ref[...].reshape(EXPERTS_PER_STEP * EXPERT_DIM, D_MODEL))

    @pl.when(e == pl.num_programs(2) - 1)
    def _():
        ffn = acc_ref[...].reshape(bb, ts, D_MODEL)
        y = alpha * x1_ref[...] + (1.0 + g2_ref[...]) * ffn
        o_ref[...] = _ln(y) * l2g_ref[...] + l2b_ref[...]


def _moe(h2, gate, x1, mod4, wg, wu, wd, l2g, l2b, *, bb, ts, alpha):
    B, S, _ = x1.shape
    tm = bb * ts
    row3 = lambda w: pl.BlockSpec((bb, ts, w), lambda i, b, e: (b, i, 0))
    flat = lambda w: pl.BlockSpec((tm, w), lambda i, b, e: (b * (S // ts) + i, 0))
    const = lambda r, w: pl.BlockSpec((r, w), lambda i, b, e: (0, 0))
    ew_in = pl.BlockSpec((EXPERTS_PER_STEP, D_MODEL, EXPERT_DIM), lambda i, b, e: (e, 0, 0))
    ew_out = pl.BlockSpec((EXPERTS_PER_STEP, EXPERT_DIM, D_MODEL), lambda i, b, e: (e, 0, 0))
    return pl.pallas_call(
        functools.partial(_moe_kernel, bb=bb, ts=ts, alpha=alpha),
        grid=(S // ts, B // bb, N_EXPERTS_ALL // EXPERTS_PER_STEP),
        in_specs=[flat(D_MODEL), flat(LANES), row3(D_MODEL),
                  pl.BlockSpec((bb, None, 1, D_MODEL), lambda i, b, e: (b, 5, 0, 0)),
                  ew_in, ew_in, ew_out,
                  const(1, D_MODEL), const(1, D_MODEL)],
        out_specs=row3(D_MODEL),
        out_shape=jax.ShapeDtypeStruct((B, S, D_MODEL), F32),
        scratch_shapes=[pltpu.VMEM((tm, D_MODEL), F32)],
        compiler_params=_cparams("arbitrary", "arbitrary", "arbitrary"),
        name="moe",
    )(h2, gate, x1, mod4, wg, wu, wd, l2g, l2b)


def _rope_tables(pos):
    half = DA_HEAD_DIM // 2
    inv = ROPE_THETA ** (-jnp.arange(half, dtype=F32) / half)
    ang = pos.astype(F32)[:, None] * inv[None, :]
    cos = jnp.cos(ang)
    sin = jnp.sin(ang)
    reps = LANES // DA_HEAD_DIM
    cos_t = jnp.tile(jnp.concatenate([cos, cos], -1), (1, reps))
    sin_t = jnp.tile(jnp.concatenate([-sin, sin], -1), (1, reps))
    return cos_t, sin_t


def _layer(x, mod4, pos, prm, layer_idx, depth, k_past, v_past, conv_prev, ssm_prev, sample):
    B, S, _ = x.shape
    alpha = (2 * depth) ** 0.25
    lam_init = 0.8 - 0.6 * math.exp(-0.3 * layer_idx)
    cos_t, sin_t = _rope_tables(pos)
    if sample:
        bb, ts = B, S
        cos_t = jnp.tile(cos_t, (B, 1))
        sin_t = jnp.tile(sin_t, (B, 1))
    else:
        bb, ts = 1, min(512, S)
    ssd_consts = (prm['conv_w'], prm['conv_b'], prm['dt_bias'], prm['a_log'], prm['d_skip'], prm['ssm_norm_w'])
    if sample:
        qb, k, kb, v, vb, z, xbc, dt_raw = _proj(x, mod4, cos_t, sin_t, prm['w_main'], prm['w_dt'],
                                                 bb=bb, ts=ts, v_transposed=False)
        att = _attn_sample(qb, kb, vb, k_past, v_past, prm['lam_vecs'], prm['subln_w'], lam_init)
        conv_prev8 = jnp.pad(conv_prev, ((0, 0), (8 - (CONV_W - 1), 0), (0, 0)))
        h0 = ssm_prev.reshape(B, SSM_WIDTH, SSM_STATE)
        ssm, h_last = _ssd(xbc, z, dt_raw, conv_prev8, h0, *ssd_consts, t_in=S, T=max(S, LANES),
                           rb=math.gcd(B, 4))
        new_conv = jnp.concatenate([conv_prev, xbc], axis=1)[:, -(CONV_W - 1):]
    else:
        qb, k, kb, v, vb, ssm, h_last, tail = _proj_ssd(x, mod4, cos_t, sin_t, prm['w_main'], prm['w_dt'],
                                                        *ssd_consts, ts=ts, T=min(256, S))
        att = _attn_prompt(qb, kb, vb, prm['lam_vecs'], prm['subln_w'], lam_init, tq=min(512, S))
        new_conv = tail[:, -(CONV_W - 1):]
    if not sample:
        ts = min(1024, S)
    x1, h2, gate = _mix(x, att, ssm, mod4, prm['w_out'], prm['ln1_g'], prm['ln1_b'], prm['router_wt'],
                        prm['router_b'], bb=bb, ts=ts, alpha=alpha)
    y = _moe(h2, gate, x1, mod4, prm['wg'], prm['wu'], prm['wd'],
             prm['ln2_g'], prm['ln2_b'], bb=bb, ts=ts, alpha=alpha)
    return (y, k.reshape(B, S, DA_HEADS, 2 * DA_HEAD_DIM), v.reshape(B, S, DA_HEADS, 2 * DA_HEAD_DIM),
            new_conv, h_last.reshape(B, SSM_HEADS, SSM_HEAD_DIM, SSM_STATE))


CAST_EXPERTS = 8


def _cast_kernel(w_ref, s_ref, o_ref):
    last = pl.num_programs(0) - 1

    @pl.when(pl.program_id(0) < last)
    def _():
        o_ref[...] = w_ref[...].astype(BF16)

    @pl.when(pl.program_id(0) == last)
    def _():
        o_ref[0] = s_ref[...].astype(BF16)


def _cast_with_shared(w, s):
    n, r, c = w.shape
    full = n // CAST_EXPERTS
    return pl.pallas_call(
        _cast_kernel,
        grid=(full + 1,),
        in_specs=[pl.BlockSpec((CAST_EXPERTS, r, c), lambda e: (jnp.minimum(e, full - 1), 0, 0)),
                  pl.BlockSpec((r, c), lambda e: (0, 0))],
        out_specs=pl.BlockSpec((CAST_EXPERTS, r, c), lambda e: (e, 0, 0)),
        out_shape=jax.ShapeDtypeStruct((n + 1, r, c), BF16),
        compiler_params=_cparams("arbitrary"),
        name="cast_experts",
    )(w, s)


def _prep_params(w_in, lq1, lk1, lq2, lk2, subln_w, conv_w, conv_b, dt_bias, a_log, d_skip, ssm_norm_w, w_out,
                 ln1_g, ln1_b, router_w, router_bias, ewg, ewu, ewd, swg, swu, swd, ln2_g, ln2_b):
    pad8 = lambda a: jnp.pad(a.reshape(1, -1), ((0, 0), (0, LANES - a.shape[-1])))
    row = lambda a: a.reshape(1, -1)
    perm = (jnp.arange(N_EXPERTS) % N_EXPERT_GROUPS) * EXPERTS_PER_GROUP + jnp.arange(N_EXPERTS) // N_EXPERT_GROUPS
    rwt = jnp.pad(router_w.T[perm], ((0, LANES - N_EXPERTS), (0, 0)))
    rb = jnp.pad(router_bias[perm], (0, LANES - N_EXPERTS)).reshape(LANES, 1)
    with_shared = _cast_with_shared
    return {
        'w_main': w_in[:, :DT_OFF].astype(BF16),
        'w_dt': jnp.pad(w_in[:, DT_OFF:], ((0, 0), (0, LANES - SSM_HEADS))),
        'lam_vecs': (row(lq1), row(lk1), row(lq2), row(lk2)),
        'subln_w': row(subln_w),
        'conv_w': conv_w, 'conv_b': row(conv_b),
        'dt_bias': pad8(dt_bias), 'a_log': pad8(a_log),
        'd_skip': row(jnp.repeat(d_skip, SSM_HEAD_DIM)), 'ssm_norm_w': row(ssm_norm_w),
        'w_out': w_out.astype(BF16), 'ln1_g': row(ln1_g), 'ln1_b': row(ln1_b),
        'router_wt': rwt, 'router_b': rb,
        'wg': with_shared(ewg, swg), 'wu': with_shared(ewu, swu), 'wd': with_shared(ewd, swd),
        'ln2_g': row(ln2_g), 'ln2_b': row(ln2_b),
    }


def kernel(x_prompt, x_sample, c_prompt, c_sample, cache_k, cache_v, state_conv, state_ssm, w_ada, b_ada, w_in, lambda_q1, lambda_k1, lambda_q2, lambda_k2, subln_w, conv_w, conv_b, dt_bias, a_log, d_skip, ssm_norm_w, w_out, ln1_g, ln1_b, router_w, router_bias, exp_w_gate, exp_w_up, exp_w_down, sh_w_gate, sh_w_up, sh_w_down, ln2_g, ln2_b):
    depth = w_ada.shape[0]
    bp, sp, _ = x_prompt.shape
    bs, ss, _ = x_sample.shape
    past = cache_k.shape[2]
    pos_prompt = jnp.arange(sp)
    pos_sample = past + jnp.arange(ss)
    c_rows = bp + bs
    c_pad = (-c_rows) % 8
    c_all = jnp.pad(jnp.concatenate([c_prompt, c_sample], 0), ((0, c_pad), (0, 0)))
    yp, ys = x_prompt, x_sample
    outs = [[] for _ in range(8)]
    for l in range(depth):
        prm = _prep_params(w_in[l], lambda_q1[l], lambda_k1[l], lambda_q2[l], lambda_k2[l], subln_w[l],
                           conv_w[l], conv_b[l], dt_bias[l], a_log[l], d_skip[l], ssm_norm_w[l], w_out[l],
                           ln1_g[l], ln1_b[l], router_w[l], router_bias[l], exp_w_gate[l], exp_w_up[l],
                           exp_w_down[l], sh_w_gate[l], sh_w_up[l], sh_w_down[l], ln2_g[l], ln2_b[l])
        mod = _ada(c_all, w_ada[l], b_ada[l]).reshape(c_rows + c_pad, 6, 1, D_MODEL)
        yp, kp, vp, cp, hp = _layer(yp, mod[:bp], pos_prompt, prm, l, depth, None, None, None, None, False)
        ys, kn, vn, cn, hn = _layer(ys, mod[bp:c_rows], pos_sample, prm, l, depth,
                                    cache_k[l].reshape(bs, past * DA_HEADS, LANES),
                                    cache_v[l].reshape(bs, past * DA_HEADS, LANES),
                                    state_conv[l], state_ssm[l], True)
        for lst, val in zip(outs, (kp, vp, cp, hp, kn, vn, cn, hn)):
            lst.append(val)
    return (yp, ys) + tuple(jnp.stack(o) for o in outs)
```

```python
import functools
import math

import jax
import jax.numpy as jnp
from jax import lax
from jax.experimental import pallas as pl
from jax.experimental.pallas import tpu as pltpu

F32 = jnp.float32
BF16 = jnp.bfloat16

D_MODEL = 1024
CHUNK = 64
CHUNK_SHIFT = CHUNK.bit_length() - 1
EPS = 1e-5
LOG2E = math.log2(math.e)
DA_HEADS = 4
DA_HEAD_DIM = 64
DA_WIDTH = DA_HEADS * 2 * DA_HEAD_DIM
ROPE_THETA = 10000.0
SSM_HEADS = 8
SSM_HEAD_DIM = 64
HEAD_SHIFT = SSM_HEAD_DIM.bit_length() - 1
SSM_WIDTH = SSM_HEADS * SSM_HEAD_DIM
SSM_GROUPS = 2
SSM_STATE = 128
CONV_W = 4
CONV_DIM = SSM_WIDTH + 2 * SSM_GROUPS * SSM_STATE
Q_OFF = 0
K_OFF = 512
V_OFF = 1024
Z_OFF = 1536
XBC_OFF = 2048
DT_OFF = 3072
N_EXPERTS = 64
N_EXPERT_GROUPS = 8
EXPERTS_PER_GROUP = 8
SLOT_SHIFT = EXPERTS_PER_GROUP.bit_length() - 1
TOPK_GROUPS = 4
TOP_K = 8
EXPERT_DIM = 256
ROUTED_SCALE = 2.5

LANES = 128
VMEM_LIMIT = 56 * 1024 * 1024


def _cparams(*sem):
    return pltpu.CompilerParams(dimension_semantics=sem, vmem_limit_bytes=VMEM_LIMIT)


def _dot(a, b):
    return jnp.dot(a, b, preferred_element_type=F32)


def _dot_nt(a, b):
    return lax.dot_general(a, b, (((1,), (1,)), ((), ())), preferred_element_type=F32)


def _split2(a):
    hi = a.astype(BF16)
    lo = (a - hi.astype(F32)).astype(BF16)
    return hi, lo


def _split3(a):
    hi = a.astype(BF16)
    r = a - hi.astype(F32)
    mid = r.astype(BF16)
    lo = (r - mid.astype(F32)).astype(BF16)
    return hi, mid, lo


def _dot3(a, b, dot=_dot):
    ah, al = _split2(a)
    bh, bl = _split2(b)
    return dot(ah, bh) + (dot(ah, bl) + dot(al, bh))


def _dot_exact_lhs(e, a, dot=_dot):
    ah, am, al = _split3(a)
    return dot(e, ah) + (dot(e, am) + dot(e, al))


def _silu(x):
    return x * jax.nn.sigmoid(x)


def _softplus(x):
    return jnp.maximum(x, 0.0) + jnp.log1p(jnp.exp(-jnp.abs(x)))


def _ln(x):
    mu = jnp.mean(x, -1, keepdims=True)
    xc = x - mu
    var = jnp.mean(xc * xc, -1, keepdims=True)
    return xc * lax.rsqrt(var + EPS)


def _ada_kernel(c_ref, w_ref, b_ref, o_ref):
    o_ref[...] = _dot3(_silu(c_ref[...]), w_ref[...]) + b_ref[...]


def _ada(c, w_ada, b_ada):
    rows = c.shape[0]
    n = w_ada.shape[1]
    tn = 1024
    return pl.pallas_call(
        _ada_kernel,
        grid=(n // tn,),
        in_specs=[pl.BlockSpec((rows, D_MODEL), lambda i: (0, 0)),
                  pl.BlockSpec((D_MODEL, tn), lambda i: (0, i)),
                  pl.BlockSpec((1, tn), lambda i: (0, i))],
        out_specs=pl.BlockSpec((rows, tn), lambda i: (0, i)),
        out_shape=jax.ShapeDtypeStruct((rows, n), F32),
        compiler_params=_cparams("arbitrary"),
        name="ada",
    )(c, w_ada, b_ada.reshape(1, n))


def _proj_hidden(x_ref, sc_ref, sh_ref, tm):
    h = _ln(x_ref[...]) * (1.0 + sc_ref[...]) + sh_ref[...]
    return h.reshape(tm, D_MODEL)


def _proj_ssm_inputs(h, w_ref, wdt_ref):
    hb = h.astype(BF16)
    return _dot(hb, w_ref[:, Z_OFF:XBC_OFF]), _dot(hb, w_ref[:, XBC_OFF:DT_OFF]), _dot3(h, wdt_ref[...])


def _proj_qkv(h, cos_ref, sin_ref, w_ref, q_ref, k_ref, kb_ref, v_ref, vb_ref, *, bb, ts, v_transposed):
    tm = bb * ts
    hb = h.astype(BF16)
    cos = jnp.tile(cos_ref[...], (1, DA_WIDTH // LANES))
    sin = jnp.tile(sin_ref[...], (1, DA_WIDTH // LANES))
    lane = lax.broadcasted_iota(jnp.int32, (tm, DA_WIDTH), 1)
    first = (lane & (DA_HEAD_DIM - 1)) < (DA_HEAD_DIM // 2)

    def rope(t):
        rot = jnp.where(first, pltpu.roll(t, DA_WIDTH - DA_HEAD_DIM // 2, 1),
                        pltpu.roll(t, DA_HEAD_DIM // 2, 1))
        return t * cos + rot * sin

    q = rope(_dot(hb, w_ref[:, Q_OFF:K_OFF])) * (DA_HEAD_DIM ** -0.5 * LOG2E)
    q_ref[...] = q.reshape(bb, ts, DA_WIDTH).astype(BF16)
    def store_heads(ref, t):
        for hd in range(DA_HEADS):
            ref[:, pl.ds(hd, ts, stride=DA_HEADS), :] = t[:, :, hd * LANES:(hd + 1) * LANES]

    k = rope(_dot(hb, w_ref[:, K_OFF:V_OFF])).reshape(bb, ts, DA_WIDTH)
    store_heads(k_ref, k)
    kb_ref[...] = k.astype(BF16)
    v = _dot(hb, w_ref[:, V_OFF:Z_OFF]).reshape(bb, ts, DA_WIDTH)
    store_heads(v_ref, v)
    if v_transposed:
        vb_ref[0, 0] = v.reshape(tm, DA_WIDTH).T.astype(BF16)
    else:
        vb_ref[...] = v.astype(BF16)


def _proj_kernel(x_ref, sc_ref, sh_ref, cos_ref, sin_ref, w_ref, wdt_ref,
                 q_ref, k_ref, kb_ref, v_ref, vb_ref, z_ref, xbc_ref, dt_ref, *, bb, ts, v_transposed):
    h = _proj_hidden(x_ref, sc_ref, sh_ref, bb * ts)
    _proj_qkv(h, cos_ref, sin_ref, w_ref, q_ref, k_ref, kb_ref, v_ref, vb_ref,
              bb=bb, ts=ts, v_transposed=v_transposed)
    z, xbc, dt_raw = _proj_ssm_inputs(h, w_ref, wdt_ref)
    z_ref[...] = z.reshape(bb, ts, SSM_WIDTH)
    xbc_ref[...] = xbc.reshape(bb, ts, CONV_DIM)
    dt_ref[...] = dt_raw.reshape(bb, ts, LANES)


def _proj_ssd_kernel(x_ref, sc_ref, sh_ref, cos_ref, sin_ref, w_ref, wdt_ref,
                     cw_ref, cb_ref, dtb_ref, alog_ref, dsk_ref, nw_ref,
                     q_ref, k_ref, kb_ref, v_ref, vb_ref, y_ref, hout_ref, tout_ref, tail_ref, st_ref, *, ts, T):
    @pl.when(pl.program_id(1) == 0)
    def _():
        tail_ref[...] = jnp.zeros_like(tail_ref)
        st_ref[...] = jnp.zeros_like(st_ref)

    h = _proj_hidden(x_ref, sc_ref, sh_ref, ts)
    z, xbc, dt_raw = _proj_ssm_inputs(h, w_ref, wdt_ref)
    for c in range(ts // T):
        rows = slice(c * T, (c + 1) * T)
        y = _ssd_chunk(xbc[rows], z[rows], dt_raw[rows], tail_ref, st_ref,
                       cw_ref, cb_ref, dtb_ref, alog_ref, dsk_ref, nw_ref, T=T, n_valid=T)
        y_ref[0, rows, :] = y.astype(BF16)
    _proj_qkv(h, cos_ref, sin_ref, w_ref, q_ref, k_ref, kb_ref, v_ref, vb_ref, bb=1, ts=ts, v_transposed=True)
    @pl.when(pl.program_id(1) == pl.num_programs(1) - 1)
    def _():
        hout_ref[0] = st_ref[...]
        tout_ref[0] = tail_ref[...]


def _proj_specs(B, S, bb, ts, v_transposed, batch_major=False):
    tm = bb * ts
    spec = lambda shape, f: pl.BlockSpec(shape, (lambda b, i: f(i, b)) if batch_major else f)
    row3 = lambda w: spec((bb, ts, w), lambda i, b: (b, i, 0))
    modspec = lambda which: spec((bb, None, 1, D_MODEL), lambda i, b: (b, which, 0, 0))
    if v_transposed:
        vb_shape = jax.ShapeDtypeStruct((B, S // ts, DA_WIDTH, ts), BF16)
        vb_spec = spec((1, 1, DA_WIDTH, ts), lambda i, b: (b, i, 0, 0))
    else:
        vb_shape = jax.ShapeDtypeStruct((B, S, DA_WIDTH), BF16)
        vb_spec = row3(DA_WIDTH)
    heads_shape = jax.ShapeDtypeStruct((B, S * DA_HEADS, LANES), F32)
    heads_spec = spec((bb, ts * DA_HEADS, LANES), lambda i, b: (b, i, 0))
    in_specs = [row3(D_MODEL), modspec(1), modspec(0),
                spec((tm, LANES), lambda i, b: (i, 0)),
                spec((tm, LANES), lambda i, b: (i, 0)),
                spec((D_MODEL, DT_OFF), lambda i, b: (0, 0)),
                spec((D_MODEL, LANES), lambda i, b: (0, 0))]
    out_shapes = [jax.ShapeDtypeStruct((B, S, DA_WIDTH), BF16),
                  heads_shape,
                  jax.ShapeDtypeStruct((B, S, DA_WIDTH), BF16),
                  heads_shape,
                  vb_shape]
    out_specs = [row3(DA_WIDTH), heads_spec, row3(DA_WIDTH), heads_spec, vb_spec]
    return row3, in_specs, out_shapes, out_specs


def _proj(x, mod4, cos_t, sin_t, w_main, w_dt, *, bb, ts, v_transposed):
    B, S, _ = x.shape
    row3, in_specs, out_shapes, out_specs = _proj_specs(B, S, bb, ts, v_transposed)
    out_shapes += [jax.ShapeDtypeStruct((B, S, SSM_WIDTH), F32),
                   jax.ShapeDtypeStruct((B, S, CONV_DIM), F32),
                   jax.ShapeDtypeStruct((B, S, LANES), F32)]
    out_specs += [row3(SSM_WIDTH), row3(CONV_DIM), row3(LANES)]
    return pl.pallas_call(
        functools.partial(_proj_kernel, bb=bb, ts=ts, v_transposed=v_transposed),
        grid=(S // ts, B // bb),
        in_specs=in_specs,
        out_specs=out_specs,
        out_shape=out_shapes,
        compiler_params=_cparams("arbitrary", "arbitrary"),
        name="proj",
    )(x, mod4, mod4, cos_t, sin_t, w_main, w_dt)


def _proj_ssd(x, mod4, cos_t, sin_t, w_main, w_dt, conv_w, conv_b, dtb, alog, dsk, nw, *, ts, T):
    B, S, _ = x.shape
    row3, in_specs, out_shapes, out_specs = _proj_specs(B, S, 1, ts, True, batch_major=True)
    const = lambda r, w: pl.BlockSpec((r, w), lambda b, i: (0, 0))
    per_b = lambda r, w: pl.BlockSpec((1, r, w), lambda b, i: (b, 0, 0))
    in_specs += [const(CONV_W, CONV_DIM), const(1, CONV_DIM), const(1, LANES), const(1, LANES),
                 const(1, SSM_WIDTH), const(1, SSM_WIDTH)]
    out_shapes += [jax.ShapeDtypeStruct((B, S, SSM_WIDTH), BF16),
                   jax.ShapeDtypeStruct((B, SSM_WIDTH, SSM_STATE), F32),
                   jax.ShapeDtypeStruct((B, 8, CONV_DIM), F32)]
    out_specs += [row3(SSM_WIDTH), per_b(SSM_WIDTH, SSM_STATE), per_b(8, CONV_DIM)]
    return pl.pallas_call(
        functools.partial(_proj_ssd_kernel, ts=ts, T=T),
        grid=(B, S // ts),
        in_specs=in_specs,
        out_specs=out_specs,
        out_shape=out_shapes,
        scratch_shapes=[pltpu.VMEM((8, CONV_DIM), F32), pltpu.VMEM((SSM_WIDTH, SSM_STATE), F32)],
        compiler_params=_cparams("arbitrary", "arbitrary"),
        name="proj_ssd",
    )(x, mod4, mod4, cos_t, sin_t, w_main, w_dt, conv_w, conv_b, dtb, alog, dsk, nw)


def _lambda(lq1, lk1, lq2, lk2, lam_init):
    s1 = jnp.sum(lq1[...] * lk1[...], axis=1, keepdims=True)
    s2 = jnp.sum(lq2[...] * lk2[...], axis=1, keepdims=True)
    return jnp.exp(s1) - jnp.exp(s2) + lam_init


def _subln(o, subw, lam_init):
    ms = jnp.mean(o * o, -1, keepdims=True)
    return o * lax.rsqrt(ms + EPS) * subw * (1.0 - lam_init)


def _attn_prompt_kernel(lq1, lk1, lq2, lk2, subw_ref, bias_ref, q_ref, k_ref, vt_ref, o_ref,
                        acc1, acc2, sa1, sa2, sb1, sb2, mxa, mxb, m1, l1, m2, l2, *, tq, tk, lam_init):
    for hd in range(q_ref.shape[2] // LANES):
        _attn_prompt_head(lq1, lk1, lq2, lk2, subw_ref, bias_ref, q_ref, k_ref, vt_ref, o_ref,
                          acc1, acc2, sa1, sa2, sb1, sb2, mxa, mxb, m1, l1, m2, l2,
                          cols=slice(hd * LANES, (hd + 1) * LANES), tq=tq, tk=tk, lam_init=lam_init)


def _attn_prompt_head(lq1, lk1, lq2, lk2, subw_ref, bias_ref, q_ref, k_ref, vt_ref, o_ref,
                      acc1, acc2, sa1, sa2, sb1, sb2, mxa, mxb, m1, l1, m2, l2, *, cols, tq, tk, lam_init):
    i = pl.program_id(1)
    qt = q_ref[0, :, cols].astype(F32).T
    row = lax.broadcasted_iota(jnp.int32, (LANES, tq), 0)
    q1t = jnp.where(row < DA_HEAD_DIM, qt, 0.0).astype(BF16)
    q2t = jnp.where(row >= DA_HEAD_DIM, qt, 0.0).astype(BF16)
    acc1[...] = jnp.zeros_like(acc1)
    acc2[...] = jnp.zeros_like(acc2)
    m1[...] = jnp.full_like(m1, -jnp.inf)
    m2[...] = jnp.full_like(m2, -jnp.inf)
    l1[...] = jnp.zeros_like(l1)
    l2[...] = jnp.zeros_like(l2)
    buf_a = (sa1, sa2, mxa)
    buf_b = (sb1, sb2, mxb)

    def produce(j, buf):
        k = k_ref[0, pl.ds(pl.multiple_of(j * tk, tk), tk), cols]
        for half, qh in enumerate((q1t, q2t)):
            s = _dot(k, qh)
            buf[half][...] = s
            buf[2][half:half + 1, :] = jnp.max(s, axis=0, keepdims=True)

    def consume(j, buf, masked):
        vt = vt_ref[0, j, cols, :]
        for half, (m, l, acc) in enumerate(((m1, l1, acc1), (m2, l2, acc2))):
            st = buf[half][...]
            if masked:
                st = st + bias_ref[...]
                tile_max = jnp.max(st, axis=0, keepdims=True)
            else:
                tile_max = buf[2][half:half + 1, :]
            mo = m[...]
            mn = jnp.maximum(mo, tile_max)
            a = jnp.exp2(mo - mn)
            p = jnp.exp2(st - mn)
            l[...] = a * l[...] + jnp.sum(p, axis=0, keepdims=True)
            m[...] = mn
            acc[...] = a * acc[...] + _dot(vt, p.astype(BF16))

    nfull = (i * tq) // tk
    produce(nfull, buf_a)
    produce(0, buf_b)
    consume(nfull, buf_a, True)

    def pair(jj, carry):
        produce(2 * jj + 1, buf_a)
        consume(2 * jj, buf_b, False)
        produce(2 * jj + 2, buf_b)
        consume(2 * jj + 1, buf_a, False)
        return carry

    lax.fori_loop(0, nfull >> 1, pair, 0)

    @pl.when((nfull & 1) == 1)
    def _():
        consume(nfull - 1, buf_b, False)

    lam = _lambda(lq1, lk1, lq2, lk2, lam_init)
    ot = acc1[...] / l1[...] - lam * (acc2[...] / l2[...])
    ot = ot * lax.rsqrt(jnp.mean(ot * ot, axis=0, keepdims=True) + EPS)
    o_ref[0, :, cols] = (ot.T * subw_ref[...] * (1.0 - lam_init)).astype(BF16)


def _attn_prompt(qb, kb, vt, lam_vecs, subw, lam_init, *, tq):
    B, S, _ = qb.shape
    nkt, tk = vt.shape[1], vt.shape[3]
    assert tq == tk, "the masked tile is the aligned diagonal tile"
    key_chunk = lax.broadcasted_iota(jnp.int32, (tk, tq), 0) >> CHUNK_SHIFT
    qry_chunk = lax.broadcasted_iota(jnp.int32, (tk, tq), 1) >> CHUNK_SHIFT
    bias = jnp.where(key_chunk <= qry_chunk, 0.0, -jnp.inf).astype(F32)
    small = pl.BlockSpec((1, DA_HEAD_DIM), lambda b, i: (0, 0))
    return pl.pallas_call(
        functools.partial(_attn_prompt_kernel, tq=tq, tk=tk, lam_init=lam_init),
        grid=(B, S // tq),
        in_specs=[small, small, small, small,
                  pl.BlockSpec((1, LANES), lambda b, i: (0, 0)),
                  pl.BlockSpec((tk, tq), lambda b, i: (0, 0)),
                  pl.BlockSpec((1, tq, DA_WIDTH), lambda b, i: (b, i, 0)),
                  pl.BlockSpec((1, S, DA_WIDTH), lambda b, i: (b, 0, 0)),
                  pl.BlockSpec((1, nkt, DA_WIDTH, tk), lambda b, i: (b, 0, 0, 0))],
        out_specs=pl.BlockSpec((1, tq, DA_WIDTH), lambda b, i: (b, i, 0)),
        out_shape=jax.ShapeDtypeStruct((B, S, DA_WIDTH), BF16),
        scratch_shapes=([pltpu.VMEM((LANES, tq), F32)] * 2 + [pltpu.VMEM((tk, tq), F32)] * 4
                        + [pltpu.VMEM((2, tq), F32)] * 2 + [pltpu.VMEM((1, tq), F32)] * 4),
        compiler_params=_cparams("arbitrary", "arbitrary"),
        name="attn_prompt",
    )(*lam_vecs, subw, bias, qb, kb, vt)


def _attn_sample_kernel(lq1, lk1, lq2, lk2, subw_ref, q_ref, kn_ref, vn_ref, kp_ref, vp_ref, o_ref,
                        *, lam_init):
    s = q_ref.shape[1]
    lam = _lambda(lq1, lk1, lq2, lk2, lam_init)
    lane = lax.broadcasted_iota(jnp.int32, (s, LANES), 1)
    outs = []
    for h in range(DA_HEADS):
        cols = slice(h * LANES, (h + 1) * LANES)
        q = q_ref[0, :, cols]
        zero = jnp.zeros_like(q)
        kn = kn_ref[0, :, cols]
        vn = vn_ref[0, :, cols]
        past = kp_ref.shape[1] // DA_HEADS
        kp = kp_ref[0, pl.ds(h, past, stride=DA_HEADS), :].astype(BF16)
        vp = vp_ref[0, pl.ds(h, past, stride=DA_HEADS), :].astype(BF16)
        q12 = jnp.concatenate([jnp.where(lane < DA_HEAD_DIM, q, zero), jnp.where(lane >= DA_HEAD_DIM, q, zero)], 0)
        sp12 = _dot_nt(q12, kp)
        sn12 = _dot_nt(q12, kn)
        probs = []
        for half in range(2):
            sp = sp12[half * s:(half + 1) * s]
            sn = sn12[half * s:(half + 1) * s]
            m = jnp.maximum(jnp.max(sp, -1, keepdims=True), jnp.max(sn, -1, keepdims=True))
            pp = jnp.exp2(sp - m)
            pn = jnp.exp2(sn - m)
            l = jnp.sum(pp, -1, keepdims=True) + jnp.sum(pn, -1, keepdims=True)
            probs.append((pp / l, pn / l))
        ap = probs[0][0] - lam * probs[1][0]
        an = probs[0][1] - lam * probs[1][1]
        o = _dot(ap.astype(BF16), vp) + _dot(an.astype(BF16), vn)
        outs.append(_subln(o, subw_ref[...], lam_init))
    o_ref[0] = jnp.concatenate(outs, axis=-1).astype(BF16)


def _attn_sample(qb, kb, vb, k_past, v_past, lam_vecs, subw, lam_init):
    B, s, _ = qb.shape
    past_rows = k_past.shape[1]
    small = pl.BlockSpec((1, DA_HEAD_DIM), lambda b: (0, 0))
    new = pl.BlockSpec((1, s, DA_WIDTH), lambda b: (b, 0, 0))
    old = pl.BlockSpec((1, past_rows, LANES), lambda b: (b, 0, 0))
    return pl.pallas_call(
        functools.partial(_attn_sample_kernel, lam_init=lam_init),
        grid=(B,),
        in_specs=[small, small, small, small, pl.BlockSpec((1, LANES), lambda b: (0, 0)),
                  new, new, new, old, old],
        out_specs=new,
        out_shape=jax.ShapeDtypeStruct((B, s, DA_WIDTH), BF16),
        compiler_params=_cparams("arbitrary"),
        name="attn_sample",
    )(*lam_vecs, subw, qb, kb, vb, k_past, v_past)


def _ssd_kernel(xbc_ref, z_ref, dt_ref, cp_ref, h0_ref, cw_ref, cb_ref, dtb_ref, alog_ref, dsk_ref, nw_ref,
                y_ref, hout_ref, tail_ref, st_ref, *, t_in, T):
    j = pl.program_id(1)

    @pl.when(j == 0)
    def _():
        tail_ref[...] = cp_ref[...]
        st_ref[...] = h0_ref[...]

    def rows(ref, r, width):
        val = ref[r]
        if T == t_in:
            return val
        return jnp.concatenate([val, jnp.zeros((T - t_in, width), F32)], axis=0)

    for r in range(xbc_ref.shape[0]):
        y = _ssd_chunk(rows(xbc_ref, r, CONV_DIM), rows(z_ref, r, SSM_WIDTH), rows(dt_ref, r, LANES),
                       tail_ref.at[r], st_ref.at[r],
                       cw_ref, cb_ref, dtb_ref, alog_ref, dsk_ref, nw_ref, T=T, n_valid=t_in)
        y_ref[r] = y[:t_in, :].astype(BF16)

    @pl.when(j == pl.num_programs(1) - 1)
    def _():
        hout_ref[...] = st_ref[...]


def _ssd_chunk(xb, zb, dt_raw, tail_ref, st_ref, cw_ref, cb_ref, dtb_ref, alog_ref, dsk_ref, nw_ref, *, T, n_valid):
    PAIR = 2 * SSM_HEAD_DIM
    xp = jnp.concatenate([tail_ref[...], xb], axis=0)
    conv = cb_ref[...] + cw_ref[CONV_W - 1:CONV_W, :] * xb
    for sft in range(1, CONV_W):
        conv = conv + cw_ref[CONV_W - 1 - sft:CONV_W - sft, :] * pltpu.roll(xp, sft, 0)[8:, :]
    if T == n_valid:
        tail_ref[...] = xb[T - 8:, :]
    xa = _silu(conv)
    xs = xa[:, :SSM_WIDTH]

    dt = _softplus(dt_raw + dtb_ref[...])
    if T != n_valid:
        valid = lax.broadcasted_iota(jnp.int32, (T, LANES), 0) < n_valid
        dt = jnp.where(valid, dt, 0.0)
    da = dt * (-jnp.exp(alog_ref[...]))

    rr = lax.broadcasted_iota(jnp.int32, (T, T), 0)
    cc = lax.broadcasted_iota(jnp.int32, (T, T), 1)
    causal = cc <= rr
    tril = jnp.where(causal, 1.0, 0.0).astype(BF16)
    acs = _dot_exact_lhs(tril, da)
    e16 = jnp.where(lax.broadcasted_iota(jnp.int32, (16, LANES), 0)
                    == lax.broadcasted_iota(jnp.int32, (16, LANES), 1), 1.0, 0.0).astype(BF16)
    acs_t = _dot_exact_lhs(e16, acs, dot=_dot_nt)

    last = acs[T - 1:T, :]
    expand = jnp.where((lax.broadcasted_iota(jnp.int32, (LANES, SSM_WIDTH), 1) >> HEAD_SHIFT)
                       == lax.broadcasted_iota(jnp.int32, (LANES, SSM_WIDTH), 0), 1.0, 0.0).astype(BF16)

    def per_head_lanes(v):
        hi, lo = _split2(v)
        return _dot(hi, expand) + _dot(lo, expand)

    xdt = xs * per_head_lanes(dt)
    xdd = xdt * per_head_lanes(jnp.exp(last - acs))
    eacs = per_head_lanes(jnp.exp(acs))

    head_of_row = lax.broadcasted_iota(jnp.int32, (SSM_WIDTH, LANES), 0) >> HEAD_SHIFT
    sel = head_of_row == lax.broadcasted_iota(jnp.int32, (SSM_WIDTH, LANES), 1)
    rowdec = jnp.sum(jnp.where(sel, jnp.exp(last), 0.0), axis=1, keepdims=True)

    lane = lax.broadcasted_iota(jnp.int32, (T, PAIR), 1)
    ys = []
    for p in range(SSM_HEADS // 2):
        g = (2 * p) // (SSM_HEADS // SSM_GROUPS)
        bm = xa[:, SSM_WIDTH + g * SSM_STATE:SSM_WIDTH + (g + 1) * SSM_STATE].astype(BF16)
        cm = xa[:, SSM_WIDTH + (SSM_GROUPS + g) * SSM_STATE:
                SSM_WIDTH + (SSM_GROUPS + g + 1) * SSM_STATE].astype(BF16)
        cb = _dot_nt(cm, bm)
        cols = slice(p * PAIR, (p + 1) * PAIR)
        xdt_p = xdt[:, cols].astype(BF16)
        yd = []
        for hh in (2 * p, 2 * p + 1):
            seg = acs[:, hh:hh + 1] - acs_t[hh:hh + 1, :]
            lmat = jnp.exp(jnp.where(causal, seg, -jnp.inf))
            yd.append(_dot((cb * lmat).astype(BF16), xdt_p))
        y_diag = jnp.where(lane < SSM_HEAD_DIM, yd[0], yd[1])
        st = st_ref[cols, :]
        y_off = _dot_nt(cm, st.astype(BF16)) * eacs[:, cols]
        ys.append(y_diag + y_off)
        st_ref[cols, :] = st * rowdec[cols, :] + _dot(xdd[:, cols].T.astype(BF16), bm)
    y = jnp.concatenate(ys, axis=-1) + xs * dsk_ref[...]
    y = y * _silu(zb)
    half = SSM_WIDTH // SSM_GROUPS
    outs = []
    for g in range(SSM_GROUPS):
        yg = y[:, g * half:(g + 1) * half]
        outs.append(yg * lax.rsqrt(jnp.mean(yg * yg, -1, keepdims=True) + EPS))
    return jnp.concatenate(outs, axis=-1) * nw_ref[...]


def _ssd(xbc, z, dt_raw, conv_prev8, h0, conv_w, conv_b, dtb, alog, dsk, nw, *, t_in, T, rb):
    B, S, _ = xbc.shape
    row = lambda w: pl.BlockSpec((rb, t_in, w), lambda b, j: (b, j, 0))
    per_b = lambda r, w: pl.BlockSpec((rb, r, w), lambda b, j: (b, 0, 0))
    const = lambda r, w: pl.BlockSpec((r, w), lambda b, j: (0, 0))
    return pl.pallas_call(
        functools.partial(_ssd_kernel, t_in=t_in, T=T),
        grid=(B // rb, S // t_in),
        in_specs=[row(CONV_DIM), row(SSM_WIDTH), row(LANES),
                  per_b(8, CONV_DIM), per_b(SSM_WIDTH, SSM_STATE),
                  const(CONV_W, CONV_DIM), const(1, CONV_DIM), const(1, LANES), const(1, LANES),
                  const(1, SSM_WIDTH), const(1, SSM_WIDTH)],
        out_specs=[row(SSM_WIDTH), per_b(SSM_WIDTH, SSM_STATE)],
        out_shape=[jax.ShapeDtypeStruct((B, S, SSM_WIDTH), BF16),
                   jax.ShapeDtypeStruct((B, SSM_WIDTH, SSM_STATE), F32)],
        scratch_shapes=[pltpu.VMEM((rb, 8, CONV_DIM), F32), pltpu.VMEM((rb, SSM_WIDTH, SSM_STATE), F32)],
        compiler_params=_cparams("arbitrary", "arbitrary"),
        name="ssd",
    )(xbc, z, dt_raw, conv_prev8, h0, conv_w, conv_b, dtb, alog, dsk, nw)


def _router(h, rwt_ref, rb_ref):
    tm = h.shape[0]
    G = N_EXPERT_GROUPS
    logits = _dot3(rwt_ref[...], h, dot=_dot_nt)
    score = jax.nn.sigmoid(logits[:N_EXPERTS, :])
    biased = score + rb_ref[:N_EXPERTS, :]
    sc = [score[G * j:G * (j + 1), :] for j in range(EXPERTS_PER_GROUP)]
    v = [biased[G * j:G * (j + 1), :] for j in range(EXPERTS_PER_GROUP)]

    def tree(op, xs):
        while len(xs) > 1:
            xs = [op(xs[a], xs[a + 1]) for a in range(0, len(xs) - 1, 2)] + ([xs[-1]] if len(xs) % 2 else [])
        return xs[0]

    m1 = tree(jnp.maximum, v)
    first = tree(jnp.minimum, [jnp.where(v[j] == m1, float(j), float(EXPERTS_PER_GROUP))
                               for j in range(EXPERTS_PER_GROUP)])
    m2 = tree(jnp.maximum, [jnp.where(first == float(j), -jnp.inf, v[j]) for j in range(EXPERTS_PER_GROUP)])
    grp = m1 + m2

    gid = lax.broadcasted_iota(jnp.int32, (G, tm), 0)
    cnt = jnp.zeros((G, tm), jnp.int32)
    for g2 in range(G):
        row = grp[g2:g2 + 1, :]
        beats = (row > grp) | ((row == grp) & (gid > g2))
        cnt = cnt + beats.astype(jnp.int32)
    gmask = cnt < TOPK_GROUPS

    idx = [(gid * EXPERTS_PER_GROUP + j).astype(F32) for j in range(EXPERTS_PER_GROUP)]
    mv = [jnp.where(gmask, v[j], -jnp.inf) for j in range(EXPERTS_PER_GROUP)]
    w = [jnp.zeros((G, tm), F32) for _ in range(EXPERTS_PER_GROUP)]
    for _ in range(TOP_K):
        top = jnp.max(tree(jnp.maximum, mv), axis=0, keepdims=True)
        who = jnp.min(tree(jnp.minimum, [jnp.where(mv[j] == top, idx[j], float(N_EXPERTS))
                                         for j in range(EXPERTS_PER_GROUP)]), axis=0, keepdims=True)
        for j in range(EXPERTS_PER_GROUP):
            hit = idx[j] == who
            w[j] = jnp.where(hit, sc[j], w[j])
            mv[j] = jnp.where(hit, -jnp.inf, mv[j])
    tot = w[0]
    for j in range(1, EXPERTS_PER_GROUP):
        tot = tot + w[j]
    tot = jnp.sum(tot, axis=0, keepdims=True)
    gates = [w[j] / tot * ROUTED_SCALE for j in range(EXPERTS_PER_GROUP)]
    shared = jnp.where(lax.broadcasted_iota(jnp.int32, (LANES - N_EXPERTS, tm), 0) == 0, 1.0, 0.0)
    by_slot = jnp.concatenate(gates + [shared], axis=0)
    e_id = lax.broadcasted_iota(jnp.int32, (LANES, LANES), 0)
    r_id = lax.broadcasted_iota(jnp.int32, (LANES, LANES), 1)
    src_row = jnp.where(e_id < N_EXPERTS,
                        (e_id & (EXPERTS_PER_GROUP - 1)) * G + (e_id >> SLOT_SHIFT), e_id)
    perm = jnp.where(r_id == src_row, 1.0, 0.0).astype(BF16)
    return _dot_exact_lhs(perm, by_slot)


def _mix_kernel(x_ref, att_ref, ssm_ref, g1_ref, sc2_ref, sh2_ref, wo_ref, l1g_ref, l1b_ref, rwt_ref, rb_ref,
                x1_ref, h2_ref, gate_ref, *, bb, ts, alpha):
    tm = bb * ts
    att = att_ref[...].reshape(tm, DA_WIDTH)
    ssm = ssm_ref[...].reshape(tm, SSM_WIDTH)
    mix = _dot(att, wo_ref[:DA_WIDTH, :]) + _dot(ssm, wo_ref[DA_WIDTH:, :])
    y = alpha * x_ref[...] + (1.0 + g1_ref[...]) * mix.reshape(bb, ts, D_MODEL)
    x1 = _ln(y) * l1g_ref[...] + l1b_ref[...]
    x1_ref[...] = x1
    h2 = (_ln(x1) * (1.0 + sc2_ref[...]) + sh2_ref[...]).reshape(tm, D_MODEL)
    h2_ref[...] = h2.astype(BF16)
    gate_ref[...] = _router(h2, rwt_ref, rb_ref).T


def _mix(x, att, ssm, mod4, wo, l1g, l1b, rwt, rb, *, bb, ts, alpha):
    B, S, _ = x.shape
    tm = bb * ts
    row3 = lambda w: pl.BlockSpec((bb, ts, w), lambda i, b: (b, i, 0))
    modspec = lambda which: pl.BlockSpec((bb, None, 1, D_MODEL), lambda i, b: (b, which, 0, 0))
    const = lambda r, w: pl.BlockSpec((r, w), lambda i, b: (0, 0))
    flat = lambda w: pl.BlockSpec((tm, w), lambda i, b: (b * (S // ts) + i, 0))
    return pl.pallas_call(
        functools.partial(_mix_kernel, bb=bb, ts=ts, alpha=alpha),
        grid=(S // ts, B // bb),
        in_specs=[row3(D_MODEL), row3(DA_WIDTH), row3(SSM_WIDTH), modspec(2), modspec(4), modspec(3),
                  const(D_MODEL, D_MODEL), const(1, D_MODEL), const(1, D_MODEL),
                  const(LANES, D_MODEL), const(LANES, 1)],
        out_specs=[row3(D_MODEL), flat(D_MODEL), flat(LANES)],
        out_shape=[jax.ShapeDtypeStruct((B, S, D_MODEL), F32),
                   jax.ShapeDtypeStruct((B * S, D_MODEL), BF16),
                   jax.ShapeDtypeStruct((B * S, LANES), F32)],
        compiler_params=_cparams("arbitrary", "arbitrary"),
        name="mix",
    )(x, att, ssm, mod4, mod4, mod4, wo, l1g, l1b, rwt, rb)


N_EXPERTS_ALL = N_EXPERTS + 1
EXPERTS_PER_STEP = 5


def _moe_kernel(h_ref, gate_ref, x1_ref, g2_ref, wg_ref, wu_ref, wd_ref,
                l2g_ref, l2b_ref, o_ref, acc_ref, *, bb, ts, alpha):
    tm = bb * ts
    e = pl.program_id(2)
    h = h_ref[...]

    @pl.when(e == 0)
    def _():
        acc_ref[...] = jnp.zeros_like(acc_ref)

    gate = gate_ref[...]
    lane = lax.broadcasted_iota(jnp.int32, (tm, LANES), 1)
    hids = []
    for s in range(EXPERTS_PER_STEP):
        col = e * EXPERTS_PER_STEP + s
        gcol = jnp.sum(jnp.where(lane == col, gate, 0.0), axis=1, keepdims=True)
        hid = _silu(_dot(h, wg_ref[s])) * _dot(h, wu_ref[s]) * gcol
        hids.append(hid.astype(BF16))
    hid = jnp.concatenate(hids, axis=-1)
    acc_ref[...] += _dot(hid, wd_ref[...].reshape(EXPERTS_PER_STEP * EXPERT_DIM, D_MODEL))

    @pl.when(e == pl.num_programs(2) - 1)
    def _():
        ffn = acc_ref[...].reshape(bb, ts, D_MODEL)
        y = alpha * x1_ref[...] + (1.0 + g2_ref[...]) * ffn
        o_ref[...] = _ln(y) * l2g_ref[...] + l2b_ref[...]


def _moe(h2, gate, x1, mod4, wg, wu, wd, l2g, l2b, *, bb, ts, alpha):
    B, S, _ = x1.shape
    tm = bb * ts
    row3 = lambda w: pl.BlockSpec((bb, ts, w), lambda i, b, e: (b, i, 0))
    flat = lambda w: pl.BlockSpec((tm, w), lambda i, b, e: (b * (S // ts) + i, 0))
    const = lambda r, w: pl.BlockSpec((r, w), lambda i, b, e: (0, 0))
    ew_in = pl.BlockSpec((EXPERTS_PER_STEP, D_MODEL, EXPERT_DIM), lambda i, b, e: (e, 0, 0))
    ew_out = pl.BlockSpec((EXPERTS_PER_STEP, EXPERT_DIM, D_MODEL), lambda i, b, e: (e, 0, 0))
    return pl.pallas_call(
        functools.partial(_moe_kernel, bb=bb, ts=ts, alpha=alpha),
        grid=(S // ts, B // bb, N_EXPERTS_ALL // EXPERTS_PER_STEP),
        in_specs=[flat(D_MODEL), flat(LANES), row3(D_MODEL),
                  pl.BlockSpec((bb, None, 1, D_MODEL), lambda i, b, e: (b, 5, 0, 0)),
                  ew_in, ew_in, ew_out,
                  const(1, D_MODEL), const(1, D_MODEL)],
        out_specs=row3(D_MODEL),
        out_shape=jax.ShapeDtypeStruct((B, S, D_MODEL), F32),
        scratch_shapes=[pltpu.VMEM((tm, D_MODEL), F32)],
        compiler_params=_cparams("arbitrary", "arbitrary", "arbitrary"),
        name="moe",
    )(h2, gate, x1, mod4, wg, wu, wd, l2g, l2b)


def _rope_tables(pos):
    half = DA_HEAD_DIM // 2
    inv = ROPE_THETA ** (-jnp.arange(half, dtype=F32) / half)
    ang = pos.astype(F32)[:, None] * inv[None, :]
    cos = jnp.cos(ang)
    sin = jnp.sin(ang)
    reps = LANES // DA_HEAD_DIM
    cos_t = jnp.tile(jnp.concatenate([cos, cos], -1), (1, reps))
    sin_t = jnp.tile(jnp.concatenate([-sin, sin], -1), (1, reps))
    return cos_t, sin_t


def _layer(x, mod4, pos, prm, layer_idx, depth, k_past, v_past, conv_prev, ssm_prev, sample):
    B, S, _ = x.shape
    alpha = (2 * depth) ** 0.25
    lam_init = 0.8 - 0.6 * math.exp(-0.3 * layer_idx)
    cos_t, sin_t = _rope_tables(pos)
    if sample:
        bb, ts = B, S
        cos_t = jnp.tile(cos_t, (B, 1))
        sin_t = jnp.tile(sin_t, (B, 1))
    else:
        bb, ts = 1, min(512, S)
    ssd_consts = (prm['conv_w'], prm['conv_b'], prm['dt_bias'], prm['a_log'], prm['d_skip'], prm['ssm_norm_w'])
    if sample:
        qb, k, kb, v, vb, z, xbc, dt_raw = _proj(x, mod4, cos_t, sin_t, prm['w_main'], prm['w_dt'],
                                                 bb=bb, ts=ts, v_transposed=False)
        att = _attn_sample(qb, kb, vb, k_past, v_past, prm['lam_vecs'], prm['subln_w'], lam_init)
        conv_prev8 = jnp.pad(conv_prev, ((0, 0), (8 - (CONV_W - 1), 0), (0, 0)))
        h0 = ssm_prev.reshape(B, SSM_WIDTH, SSM_STATE)
        ssm, h_last = _ssd(xbc, z, dt_raw, conv_prev8, h0, *ssd_consts, t_in=S, T=max(S, LANES),
                           rb=math.gcd(B, 4))
        new_conv = jnp.concatenate([conv_prev, xbc], axis=1)[:, -(CONV_W - 1):]
    else:
        qb, k, kb, v, vb, ssm, h_last, tail = _proj_ssd(x, mod4, cos_t, sin_t, prm['w_main'], prm['w_dt'],
                                                        *ssd_consts, ts=ts, T=min(256, S))
        att = _attn_prompt(qb, kb, vb, prm['lam_vecs'], prm['subln_w'], lam_init, tq=min(512, S))
        new_conv = tail[:, -(CONV_W - 1):]
    if not sample:
        ts = min(1024, S)
    x1, h2, gate = _mix(x, att, ssm, mod4, prm['w_out'], prm['ln1_g'], prm['ln1_b'], prm['router_wt'],
                        prm['router_b'], bb=bb, ts=ts, alpha=alpha)
    y = _moe(h2, gate, x1, mod4, prm['wg'], prm['wu'], prm['wd'],
             prm['ln2_g'], prm['ln2_b'], bb=bb, ts=ts, alpha=alpha)
    return (y, k.reshape(B, S, DA_HEADS, 2 * DA_HEAD_DIM), v.reshape(B, S, DA_HEADS, 2 * DA_HEAD_DIM),
            new_conv, h_last.reshape(B, SSM_HEADS, SSM_HEAD_DIM, SSM_STATE))


CAST_EXPERTS = 8


def _cast_kernel(w_ref, s_ref, o_ref):
    last = pl.num_programs(0) - 1

    @pl.when(pl.program_id(0) < last)
    def _():
        o_ref[...] = w_ref[...].astype(BF16)

    @pl.when(pl.program_id(0) == last)
    def _():
        o_ref[0] = s_ref[...].astype(BF16)


def _cast_with_shared(w, s):
    n, r, c = w.shape
    full = n // CAST_EXPERTS
    return pl.pallas_call(
        _cast_kernel,
        grid=(full + 1,),
        in_specs=[pl.BlockSpec((CAST_EXPERTS, r, c), lambda e: (jnp.minimum(e, full - 1), 0, 0)),
                  pl.BlockSpec((r, c), lambda e: (0, 0))],
        out_specs=pl.BlockSpec((CAST_EXPERTS, r, c), lambda e: (e, 0, 0)),
        out_shape=jax.ShapeDtypeStruct((n + 1, r, c), BF16),
        compiler_params=_cparams("arbitrary"),
        name="cast_experts",
    )(w, s)


def _prep_params(w_in, lq1, lk1, lq2, lk2, subln_w, conv_w, conv_b, dt_bias, a_log, d_skip, ssm_norm_w, w_out,
                 ln1_g, ln1_b, router_w, router_bias, ewg, ewu, ewd, swg, swu, swd, ln2_g, ln2_b):
    pad8 = lambda a: jnp.pad(a.reshape(1, -1), ((0, 0), (0, LANES - a.shape[-1])))
    row = lambda a: a.reshape(1, -1)
    perm = (jnp.arange(N_EXPERTS) % N_EXPERT_GROUPS) * EXPERTS_PER_GROUP + jnp.arange(N_EXPERTS) // N_EXPERT_GROUPS
    rwt = jnp.pad(router_w.T[perm], ((0, LANES - N_EXPERTS), (0, 0)))
    rb = jnp.pad(router_bias[perm], (0, LANES - N_EXPERTS)).reshape(LANES, 1)
    with_shared = _cast_with_shared
    return {
        'w_main': w_in[:, :DT_OFF].astype(BF16),
        'w_dt': jnp.pad(w_in[:, DT_OFF:], ((0, 0), (0, LANES - SSM_HEADS))),
        'lam_vecs': (row(lq1), row(lk1), row(lq2), row(lk2)),
        'subln_w': row(subln_w),
        'conv_w': conv_w, 'conv_b': row(conv_b),
        'dt_bias': pad8(dt_bias), 'a_log': pad8(a_log),
        'd_skip': row(jnp.repeat(d_skip, SSM_HEAD_DIM)), 'ssm_norm_w': row(ssm_norm_w),
        'w_out': w_out.astype(BF16), 'ln1_g': row(ln1_g), 'ln1_b': row(ln1_b),
        'router_wt': rwt, 'router_b': rb,
        'wg': with_shared(ewg, swg), 'wu': with_shared(ewu, swu), 'wd': with_shared(ewd, swd),
        'ln2_g': row(ln2_g), 'ln2_b': row(ln2_b),
    }


def kernel(x_prompt, x_sample, c_prompt, c_sample, cache_k, cache_v, state_conv, state_ssm, w_ada, b_ada, w_in, lambda_q1, lambda_k1, lambda_q2, lambda_k2, subln_w, conv_w, conv_b, dt_bias, a_log, d_skip, ssm_norm_w, w_out, ln1_g, ln1_b, router_w, router_bias, exp_w_gate, exp_w_up, exp_w_down, sh_w_gate, sh_w_up, sh_w_down, ln2_g, ln2_b):
    depth = w_ada.shape[0]
    bp, sp, _ = x_prompt.shape
    bs, ss, _ = x_sample.shape
    past = cache_k.shape[2]
    pos_prompt = jnp.arange(sp)
    pos_sample = past + jnp.arange(ss)
    c_rows = bp + bs
    c_pad = (-c_rows) % 8
    c_all = jnp.pad(jnp.concatenate([c_prompt, c_sample], 0), ((0, c_pad), (0, 0)))
    yp, ys = x_prompt, x_sample
    outs = [[] for _ in range(8)]
    for l in range(depth):
        prm = _prep_params(w_in[l], lambda_q1[l], lambda_k1[l], lambda_q2[l], lambda_k2[l], subln_w[l],
                           conv_w[l], conv_b[l], dt_bias[l], a_log[l], d_skip[l], ssm_norm_w[l], w_out[l],
                           ln1_g[l], ln1_b[l], router_w[l], router_bias[l], exp_w_gate[l], exp_w_up[l],
                           exp_w_down[l], sh_w_gate[l], sh_w_up[l], sh_w_down[l], ln2_g[l], ln2_b[l])
        mod = _ada(c_all, w_ada[l], b_ada[l]).reshape(c_rows + c_pad, 6, 1, D_MODEL)
        yp, kp, vp, cp, hp = _layer(yp, mod[:bp], pos_prompt, prm, l, depth, None, None, None, None, False)
        ys, kn, vn, cn, hn = _layer(ys, mod[bp:c_rows], pos_sample, prm, l, depth,
                                    cache_k[l].reshape(bs, past * DA_HEADS, LANES),
                                    cache_v[l].reshape(bs, past * DA_HEADS, LANES),
                                    state_conv[l], state_ssm[l], True)
        for lst, val in zip(outs, (kp, vp, cp, hp, kn, vn, cn, hn)):
            lst.append(val)
    return (yp, ys) + tuple(jnp.stack(o) for o in outs)
```

```python
import functools
import math

import jax
import jax.numpy as jnp
from jax import lax
from jax.experimental import pallas as pl
from jax.experimental.pallas import tpu as pltpu

F32 = jnp.float32
BF16 = jnp.bfloat16

D_MODEL = 1024
CHUNK = 64
CHUNK_SHIFT = CHUNK.bit_length() - 1
EPS = 1e-5
LOG2E = math.log2(math.e)
DA_HEADS = 4
DA_HEAD_DIM = 64
DA_WIDTH = DA_HEADS * 2 * DA_HEAD_DIM
ROPE_THETA = 10000.0
SSM_HEADS = 8
SSM_HEAD_DIM = 64
HEAD_SHIFT = SSM_HEAD_DIM.bit_length() - 1
SSM_WIDTH = SSM_HEADS * SSM_HEAD_DIM
SSM_GROUPS = 2
SSM_STATE = 128
CONV_W = 4
CONV_DIM = SSM_WIDTH + 2 * SSM_GROUPS * SSM_STATE
Q_OFF = 0
K_OFF = 512
V_OFF = 1024
Z_OFF = 1536
XBC_OFF = 2048
DT_OFF = 3072
N_EXPERTS = 64
N_EXPERT_GROUPS = 8
EXPERTS_PER_GROUP = 8
SLOT_SHIFT = EXPERTS_PER_GROUP.bit_length() - 1
TOPK_GROUPS = 4
TOP_K = 8
EXPERT_DIM = 256
ROUTED_SCALE = 2.5

LANES = 128
VMEM_LIMIT = 56 * 1024 * 1024


def _cparams(*sem):
    return pltpu.CompilerParams(dimension_semantics=sem, vmem_limit_bytes=VMEM_LIMIT)


def _dot(a, b):
    return jnp.dot(a, b, preferred_element_type=F32)


def _dot_nt(a, b):
    return lax.dot_general(a, b, (((1,), (1,)), ((), ())), preferred_element_type=F32)


def _split2(a):
    hi = a.astype(BF16)
    lo = (a - hi.astype(F32)).astype(BF16)
    return hi, lo


def _split3(a):
    hi = a.astype(BF16)
    r = a - hi.astype(F32)
    mid = r.astype(BF16)
    lo = (r - mid.astype(F32)).astype(BF16)
    return hi, mid, lo


def _dot3(a, b, dot=_dot):
    ah, al = _split2(a)
    bh, bl = _split2(b)
    return dot(ah, bh) + (dot(ah, bl) + dot(al, bh))


def _dot_exact_lhs(e, a, dot=_dot):
    ah, am, al = _split3(a)
    return dot(e, ah) + (dot(e, am) + dot(e, al))


def _silu(x):
    return x * jax.nn.sigmoid(x)


def _softplus(x):
    return jnp.maximum(x, 0.0) + jnp.log1p(jnp.exp(-jnp.abs(x)))


def _ln(x):
    mu = jnp.mean(x, -1, keepdims=True)
    xc = x - mu
    var = jnp.mean(xc * xc, -1, keepdims=True)
    return xc * lax.rsqrt(var + EPS)


def _ada_kernel(c_ref, w_ref, b_ref, o_ref):
    o_ref[...] = _dot3(_silu(c_ref[...]), w_ref[...]) + b_ref[...]


def _ada(c, w_ada, b_ada):
    rows = c.shape[0]
    n = w_ada.shape[1]
    tn = 1024
    return pl.pallas_call(
        _ada_kernel,
        grid=(n // tn,),
        in_specs=[pl.BlockSpec((rows, D_MODEL), lambda i: (0, 0)),
                  pl.BlockSpec((D_MODEL, tn), lambda i: (0, i)),
                  pl.BlockSpec((1, tn), lambda i: (0, i))],
        out_specs=pl.BlockSpec((rows, tn), lambda i: (0, i)),
        out_shape=jax.ShapeDtypeStruct((rows, n), F32),
        compiler_params=_cparams("arbitrary"),
        name="ada",
    )(c, w_ada, b_ada.reshape(1, n))


def _proj_hidden(x_ref, sc_ref, sh_ref, tm):
    h = _ln(x_ref[...]) * (1.0 + sc_ref[...]) + sh_ref[...]
    return h.reshape(tm, D_MODEL)


def _proj_ssm_inputs(h, w_ref, wdt_ref):
    hb = h.astype(BF16)
    return _dot(hb, w_ref[:, Z_OFF:XBC_OFF]), _dot(hb, w_ref[:, XBC_OFF:DT_OFF]), _dot3(h, wdt_ref[...])


def _proj_qkv(h, cos_ref, sin_ref, w_ref, q_ref, k_ref, kb_ref, v_ref, vb_ref, *, bb, ts, v_transposed):
    tm = bb * ts
    hb = h.astype(BF16)
    cos = jnp.tile(cos_ref[...], (1, DA_WIDTH // LANES))
    sin = jnp.tile(sin_ref[...], (1, DA_WIDTH // LANES))
    lane = lax.broadcasted_iota(jnp.int32, (tm, DA_WIDTH), 1)
    first = (lane & (DA_HEAD_DIM - 1)) < (DA_HEAD_DIM // 2)

    def rope(t):
        rot = jnp.where(first, pltpu.roll(t, DA_WIDTH - DA_HEAD_DIM // 2, 1),
                        pltpu.roll(t, DA_HEAD_DIM // 2, 1))
        return t * cos + rot * sin

    q = rope(_dot(hb, w_ref[:, Q_OFF:K_OFF])) * (DA_HEAD_DIM ** -0.5 * LOG2E)
    q_ref[...] = q.reshape(bb, ts, DA_WIDTH).astype(BF16)
    def store_heads(ref, t):
        for hd in range(DA_HEADS):
            ref[:, pl.ds(hd, ts, stride=DA_HEADS), :] = t[:, :, hd * LANES:(hd + 1) * LANES]

    k = rope(_dot(hb, w_ref[:, K_OFF:V_OFF])).reshape(bb, ts, DA_WIDTH)
    store_heads(k_ref, k)
    kb_ref[...] = k.astype(BF16)
    v = _dot(hb, w_ref[:, V_OFF:Z_OFF]).reshape(bb, ts, DA_WIDTH)
    store_heads(v_ref, v)
    if v_transposed:
        vb_ref[0, 0] = v.reshape(tm, DA_WIDTH).T.astype(BF16)
    else:
        vb_ref[...] = v.astype(BF16)


def _proj_kernel(x_ref, sc_ref, sh_ref, cos_ref, sin_ref, w_ref, wdt_ref,
                 q_ref, k_ref, kb_ref, v_ref, vb_ref, z_ref, xbc_ref, dt_ref, *, bb, ts, v_transposed):
    h = _proj_hidden(x_ref, sc_ref, sh_ref, bb * ts)
    _proj_qkv(h, cos_ref, sin_ref, w_ref, q_ref, k_ref, kb_ref, v_ref, vb_ref,
              bb=bb, ts=ts, v_transposed=v_transposed)
    z, xbc, dt_raw = _proj_ssm_inputs(h, w_ref, wdt_ref)
    z_ref[...] = z.reshape(bb, ts, SSM_WIDTH)
    xbc_ref[...] = xbc.reshape(bb, ts, CONV_DIM)
    dt_ref[...] = dt_raw.reshape(bb, ts, LANES)


def _proj_ssd_kernel(x_ref, sc_ref, sh_ref, cos_ref, sin_ref, w_ref, wdt_ref,
                     cw_ref, cb_ref, dtb_ref, alog_ref, dsk_ref, nw_ref,
                     q_ref, k_ref, kb_ref, v_ref, vb_ref, y_ref, hout_ref, tout_ref, tail_ref, st_ref, *, ts, T):
    @pl.when(pl.program_id(1) == 0)
    def _():
        tail_ref[...] = jnp.zeros_like(tail_ref)
        st_ref[...] = jnp.zeros_like(st_ref)

    h = _proj_hidden(x_ref, sc_ref, sh_ref, ts)
    z, xbc, dt_raw = _proj_ssm_inputs(h, w_ref, wdt_ref)
    for c in range(ts // T):
        rows = slice(c * T, (c + 1) * T)
        y = _ssd_chunk(xbc[rows], z[rows], dt_raw[rows], tail_ref, st_ref,
                       cw_ref, cb_ref, dtb_ref, alog_ref, dsk_ref, nw_ref, T=T, n_valid=T)
        y_ref[0, rows, :] = y.astype(BF16)
    _proj_qkv(h, cos_ref, sin_ref, w_ref, q_ref, k_ref, kb_ref, v_ref, vb_ref, bb=1, ts=ts, v_transposed=True)
    @pl.when(pl.program_id(1) == pl.num_programs(1) - 1)
    def _():
        hout_ref[0] = st_ref[...]
        tout_ref[0] = tail_ref[...]


def _proj_specs(B, S, bb, ts, v_transposed, batch_major=False):
    tm = bb * ts
    spec = lambda shape, f: pl.BlockSpec(shape, (lambda b, i: f(i, b)) if batch_major else f)
    row3 = lambda w: spec((bb, ts, w), lambda i, b: (b, i, 0))
    modspec = lambda which: spec((bb, None, 1, D_MODEL), lambda i, b: (b, which, 0, 0))
    if v_transposed:
        vb_shape = jax.ShapeDtypeStruct((B, S // ts, DA_WIDTH, ts), BF16)
        vb_spec = spec((1, 1, DA_WIDTH, ts), lambda i, b: (b, i, 0, 0))
    else:
        vb_shape = jax.ShapeDtypeStruct((B, S, DA_WIDTH), BF16)
        vb_spec = row3(DA_WIDTH)
    heads_shape = jax.ShapeDtypeStruct((B, S * DA_HEADS, LANES), F32)
    heads_spec = spec((bb, ts * DA_HEADS, LANES), lambda i, b: (b, i, 0))
    in_specs = [row3(D_MODEL), modspec(1), modspec(0),
                spec((tm, LANES), lambda i, b: (i, 0)),
                spec((tm, LANES), lambda i, b: (i, 0)),
                spec((D_MODEL, DT_OFF), lambda i, b: (0, 0)),
                spec((D_MODEL, LANES), lambda i, b: (0, 0))]
    out_shapes = [jax.ShapeDtypeStruct((B, S, DA_WIDTH), BF16),
                  heads_shape,
                  jax.ShapeDtypeStruct((B, S, DA_WIDTH), BF16),
                  heads_shape,
                  vb_shape]
    out_specs = [row3(DA_WIDTH), heads_spec, row3(DA_WIDTH), heads_spec, vb_spec]
    return row3, in_specs, out_shapes, out_specs


def _proj(x, mod4, cos_t, sin_t, w_main, w_dt, *, bb, ts, v_transposed):
    B, S, _ = x.shape
    row3, in_specs, out_shapes, out_specs = _proj_specs(B, S, bb, ts, v_transposed)
    out_shapes += [jax.ShapeDtypeStruct((B, S, SSM_WIDTH), F32),
                   jax.ShapeDtypeStruct((B, S, CONV_DIM), F32),
                   jax.ShapeDtypeStruct((B, S, LANES), F32)]
    out_specs += [row3(SSM_WIDTH), row3(CONV_DIM), row3(LANES)]
    return pl.pallas_call(
        functools.partial(_proj_kernel, bb=bb, ts=ts, v_transposed=v_transposed),
        grid=(S // ts, B // bb),
        in_specs=in_specs,
        out_specs=out_specs,
        out_shape=out_shapes,
        compiler_params=_cparams("arbitrary", "arbitrary"),
        name="proj",
    )(x, mod4, mod4, cos_t, sin_t, w_main, w_dt)


def _proj_ssd(x, mod4, cos_t, sin_t, w_main, w_dt, conv_w, conv_b, dtb, alog, dsk, nw, *, ts, T):
    B, S, _ = x.shape
    row3, in_specs, out_shapes, out_specs = _proj_specs(B, S, 1, ts, True, batch_major=True)
    const = lambda r, w: pl.BlockSpec((r, w), lambda b, i: (0, 0))
    per_b = lambda r, w: pl.BlockSpec((1, r, w), lambda b, i: (b, 0, 0))
    in_specs += [const(CONV_W, CONV_DIM), const(1, CONV_DIM), const(1, LANES), const(1, LANES),
                 const(1, SSM_WIDTH), const(1, SSM_WIDTH)]
    out_shapes += [jax.ShapeDtypeStruct((B, S, SSM_WIDTH), BF16),
                   jax.ShapeDtypeStruct((B, SSM_WIDTH, SSM_STATE), F32),
                   jax.ShapeDtypeStruct((B, 8, CONV_DIM), F32)]
    out_specs += [row3(SSM_WIDTH), per_b(SSM_WIDTH, SSM_STATE), per_b(8, CONV_DIM)]
    return pl.pallas_call(
        functools.partial(_proj_ssd_kernel, ts=ts, T=T),
        grid=(B, S // ts),
        in_specs=in_specs,
        out_specs=out_specs,
        out_shape=out_shapes,
        scratch_shapes=[pltpu.VMEM((8, CONV_DIM), F32), pltpu.VMEM((SSM_WIDTH, SSM_STATE), F32)],
        compiler_params=_cparams("arbitrary", "arbitrary"),
        name="proj_ssd",
    )(x, mod4, mod4, cos_t, sin_t, w_main, w_dt, conv_w, conv_b, dtb, alog, dsk, nw)


def _lambda(lq1, lk1, lq2, lk2, lam_init):
    s1 = jnp.sum(lq1[...] * lk1[...], axis=1, keepdims=True)
    s2 = jnp.sum(lq2[...] * lk2[...], axis=1, keepdims=True)
    return jnp.exp(s1) - jnp.exp(s2) + lam_init


def _subln(o, subw, lam_init):
    ms = jnp.mean(o * o, -1, keepdims=True)
    return o * lax.rsqrt(ms + EPS) * subw * (1.0 - lam_init)


def _attn_prompt_kernel(lq1, lk1, lq2, lk2, subw_ref, bias_ref, q_ref, k_ref, vt_ref, o_ref,
                        acc1, acc2, sa1, sa2, sb1, sb2, mxa, mxb, m1, l1, m2, l2, *, tq, tk, lam_init):
    for hd in range(q_ref.shape[2] // LANES):
        _attn_prompt_head(lq1, lk1, lq2, lk2, subw_ref, bias_ref, q_ref, k_ref, vt_ref, o_ref,
                          acc1, acc2, sa1, sa2, sb1, sb2, mxa, mxb, m1, l1, m2, l2,
                          cols=slice(hd * LANES, (hd + 1) * LANES), tq=tq, tk=tk, lam_init=lam_init)


def _attn_prompt_head(lq1, lk1, lq2, lk2, subw_ref, bias_ref, q_ref, k_ref, vt_ref, o_ref,
                      acc1, acc2, sa1, sa2, sb1, sb2, mxa, mxb, m1, l1, m2, l2, *, cols, tq, tk, lam_init):
    i = pl.program_id(1)
    qt = q_ref[0, :, cols].astype(F32).T
    row = lax.broadcasted_iota(jnp.int32, (LANES, tq), 0)
    q1t = jnp.where(row < DA_HEAD_DIM, qt, 0.0).astype(BF16)
    q2t = jnp.where(row >= DA_HEAD_DIM, qt, 0.0).astype(BF16)
    acc1[...] = jnp.zeros_like(acc1)
    acc2[...] = jnp.zeros_like(acc2)
    m1[...] = jnp.full_like(m1, -jnp.inf)
    m2[...] = jnp.full_like(m2, -jnp.inf)
    l1[...] = jnp.zeros_like(l1)
    l2[...] = jnp.zeros_like(l2)
    buf_a = (sa1, sa2, mxa)
    buf_b = (sb1, sb2, mxb)

    def produce(j, buf):
        k = k_ref[0, pl.ds(pl.multiple_of(j * tk, tk), tk), cols]
        for half, qh in enumerate((q1t, q2t)):
            s = _dot(k, qh)
            buf[half][...] = s
            buf[2][half:half + 1, :] = jnp.max(s, axis=0, keepdims=True)

    def consume(j, buf, masked):
        vt = vt_ref[0, j, cols, :]
        for half, (m, l, acc) in enumerate(((m1, l1, acc1), (m2, l2, acc2))):
            st = buf[half][...]
            if masked:
                st = st + bias_ref[...]
                tile_max = jnp.max(st, axis=0, keepdims=True)
            else:
                tile_max = buf[2][half:half + 1, :]
            mo = m[...]
            mn = jnp.maximum(mo, tile_max)
            a = jnp.exp2(mo - mn)
            p = jnp.exp2(st - mn)
            l[...] = a * l[...] + jnp.sum(p, axis=0, keepdims=True)
            m[...] = mn
            acc[...] = a * acc[...] + _dot(vt, p.astype(BF16))

    nfull = (i * tq) // tk
    produce(nfull, buf_a)
    produce(0, buf_b)
    consume(nfull, buf_a, True)

    def pair(jj, carry):
        produce(2 * jj + 1, buf_a)
        consume(2 * jj, buf_b, False)
        produce(2 * jj + 2, buf_b)
        consume(2 * jj + 1, buf_a, False)
        return carry

    lax.fori_loop(0, nfull >> 1, pair, 0)

    @pl.when((nfull & 1) == 1)
    def _():
        consume(nfull - 1, buf_b, False)

    lam = _lambda(lq1, lk1, lq2, lk2, lam_init)
    ot = acc1[...] / l1[...] - lam * (acc2[...] / l2[...])
    ot = ot * lax.rsqrt(jnp.mean(ot * ot, axis=0, keepdims=True) + EPS)
    o_ref[0, :, cols] = (ot.T * subw_ref[...] * (1.0 - lam_init)).astype(BF16)


def _attn_prompt(qb, kb, vt, lam_vecs, subw, lam_init, *, tq):
    B, S, _ = qb.shape
    nkt, tk = vt.shape[1], vt.shape[3]
    assert tq == tk, "the masked tile is the aligned diagonal tile"
    key_chunk = lax.broadcasted_iota(jnp.int32, (tk, tq), 0) >> CHUNK_SHIFT
    qry_chunk = lax.broadcasted_iota(jnp.int32, (tk, tq), 1) >> CHUNK_SHIFT
    bias = jnp.where(key_chunk <= qry_chunk, 0.0, -jnp.inf).astype(F32)
    small = pl.BlockSpec((1, DA_HEAD_DIM), lambda b, i: (0, 0))
    return pl.pallas_call(
        functools.partial(_attn_prompt_kernel, tq=tq, tk=tk, lam_init=lam_init),
        grid=(B, S // tq),
        in_specs=[small, small, small, small,
                  pl.BlockSpec((1, LANES), lambda b, i: (0, 0)),
                  pl.BlockSpec((tk, tq), lambda b, i: (0, 0)),
                  pl.BlockSpec((1, tq, DA_WIDTH), lambda b, i: (b, i, 0)),
                  pl.BlockSpec((1, S, DA_WIDTH), lambda b, i: (b, 0, 0)),
                  pl.BlockSpec((1, nkt, DA_WIDTH, tk), lambda b, i: (b, 0, 0, 0))],
        out_specs=pl.BlockSpec((1, tq, DA_WIDTH), lambda b, i: (b, i, 0)),
        out_shape=jax.ShapeDtypeStruct((B, S, DA_WIDTH), BF16),
        scratch_shapes=([pltpu.VMEM((LANES, tq), F32)] * 2 + [pltpu.VMEM((tk, tq), F32)] * 4
                        + [pltpu.VMEM((2, tq), F32)] * 2 + [pltpu.VMEM((1, tq), F32)] * 4),
        compiler_params=_cparams("arbitrary", "arbitrary"),
        name="attn_prompt",
    )(*lam_vecs, subw, bias, qb, kb, vt)


def _attn_sample_kernel(lq1, lk1, lq2, lk2, subw_ref, q_ref, kn_ref, vn_ref, kp_ref, vp_ref, o_ref,
                        *, lam_init):
    s = q_ref.shape[1]
    past = kp_ref.shape[1] // DA_HEADS
    nk = -(-(past + s) // LANES) * LANES
    half_off = LANES // 2
    lam = _lambda(lq1, lk1, lq2, lk2, lam_init)
    lane = lax.broadcasted_iota(jnp.int32, (s, LANES), 1)
    valid = lax.broadcasted_iota(jnp.int32, (nk, LANES), 0) < past + s
    outs = []
    for h in range(DA_HEADS):
        cols = slice(h * LANES, (h + 1) * LANES)
        q = q_ref[0, :, cols].astype(F32)
        gap = jnp.zeros((half_off - s, LANES), F32)
        q_rows = jnp.concatenate([jnp.where(lane < DA_HEAD_DIM, q, 0.0), gap,
                                  jnp.where(lane >= DA_HEAD_DIM, q, 0.0), gap], axis=0)
        qt = q_rows.T.astype(BF16)
        kp = kp_ref[0, pl.ds(h, past, stride=DA_HEADS), :].astype(BF16)
        k_all = jnp.concatenate([kp, kn_ref[0, :, cols], jnp.zeros((nk - past - s, LANES), BF16)], axis=0)
        vp = vp_ref[0, pl.ds(h, past, stride=DA_HEADS), :]
        v_all = jnp.concatenate([vp, vn_ref[0, :, cols].astype(F32), jnp.zeros((nk - past - s, LANES), F32)], axis=0)
        st = jnp.where(valid, _dot(k_all, qt), -jnp.inf)
        m = jnp.max(st, axis=0, keepdims=True)
        p = jnp.exp2(st - m)
        pn = p / jnp.sum(p, axis=0, keepdims=True)
        a = pn - lam * pltpu.roll(pn, half_off, 1)
        ot = _dot(v_all.T.astype(BF16), a.astype(BF16))
        outs.append(_subln(ot.T[:s, :], subw_ref[...], lam_init))
    o_ref[0] = jnp.concatenate(outs, axis=-1).astype(BF16)


def _attn_sample(qb, kb, vb, k_past, v_past, lam_vecs, subw, lam_init):
    B, s, _ = qb.shape
    past_rows = k_past.shape[1]
    small = pl.BlockSpec((1, DA_HEAD_DIM), lambda b: (0, 0))
    new = pl.BlockSpec((1, s, DA_WIDTH), lambda b: (b, 0, 0))
    old = pl.BlockSpec((1, past_rows, LANES), lambda b: (b, 0, 0))
    return pl.pallas_call(
        functools.partial(_attn_sample_kernel, lam_init=lam_init),
        grid=(B,),
        in_specs=[small, small, small, small, pl.BlockSpec((1, LANES), lambda b: (0, 0)),
                  new, new, new, old, old],
        out_specs=new,
        out_shape=jax.ShapeDtypeStruct((B, s, DA_WIDTH), BF16),
        compiler_params=_cparams("arbitrary"),
        name="attn_sample",
    )(*lam_vecs, subw, qb, kb, vb, k_past, v_past)


def _ssd_kernel(xbc_ref, z_ref, dt_ref, cp_ref, h0_ref, cw_ref, cb_ref, dtb_ref, alog_ref, dsk_ref, nw_ref,
                y_ref, hout_ref, tail_ref, st_ref, *, t_in, T):
    j = pl.program_id(1)

    @pl.when(j == 0)
    def _():
        tail_ref[...] = cp_ref[...]
        st_ref[...] = h0_ref[...]

    def rows(ref, r, width):
        val = ref[r]
        if T == t_in:
            return val
        return jnp.concatenate([val, jnp.zeros((T - t_in, width), F32)], axis=0)

    for r in range(xbc_ref.shape[0]):
        y = _ssd_chunk(rows(xbc_ref, r, CONV_DIM), rows(z_ref, r, SSM_WIDTH), rows(dt_ref, r, LANES),
                       tail_ref.at[r], st_ref.at[r],
                       cw_ref, cb_ref, dtb_ref, alog_ref, dsk_ref, nw_ref, T=T, n_valid=t_in)
        y_ref[r] = y[:t_in, :].astype(BF16)

    @pl.when(j == pl.num_programs(1) - 1)
    def _():
        hout_ref[...] = st_ref[...]


def _ssd_chunk(xb, zb, dt_raw, tail_ref, st_ref, cw_ref, cb_ref, dtb_ref, alog_ref, dsk_ref, nw_ref, *, T, n_valid):
    PAIR = 2 * SSM_HEAD_DIM
    xp = jnp.concatenate([tail_ref[...], xb], axis=0)
    conv = cb_ref[...] + cw_ref[CONV_W - 1:CONV_W, :] * xb
    for sft in range(1, CONV_W):
        conv = conv + cw_ref[CONV_W - 1 - sft:CONV_W - sft, :] * pltpu.roll(xp, sft, 0)[8:, :]
    if T == n_valid:
        tail_ref[...] = xb[T - 8:, :]
    xa = _silu(conv)
    xs = xa[:, :SSM_WIDTH]

    dt = _softplus(dt_raw + dtb_ref[...])
    if T != n_valid:
        valid = lax.broadcasted_iota(jnp.int32, (T, LANES), 0) < n_valid
        dt = jnp.where(valid, dt, 0.0)
    da = dt * (-jnp.exp(alog_ref[...]))

    rr = lax.broadcasted_iota(jnp.int32, (T, T), 0)
    cc = lax.broadcasted_iota(jnp.int32, (T, T), 1)
    causal = cc <= rr
    tril = jnp.where(causal, 1.0, 0.0).astype(BF16)
    acs = _dot_exact_lhs(tril, da)
    e16 = jnp.where(lax.broadcasted_iota(jnp.int32, (16, LANES), 0)
                    == lax.broadcasted_iota(jnp.int32, (16, LANES), 1), 1.0, 0.0).astype(BF16)
    acs_t = _dot_exact_lhs(e16, acs, dot=_dot_nt)

    last = acs[T - 1:T, :]
    expand = jnp.where((lax.broadcasted_iota(jnp.int32, (LANES, SSM_WIDTH), 1) >> HEAD_SHIFT)
                       == lax.broadcasted_iota(jnp.int32, (LANES, SSM_WIDTH), 0), 1.0, 0.0).astype(BF16)

    def per_head_lanes(v):
        hi, lo = _split2(v)
        return _dot(hi, expand) + _dot(lo, expand)

    xdt = xs * per_head_lanes(dt)
    xdd = xdt * per_head_lanes(jnp.exp(last - acs))
    eacs = per_head_lanes(jnp.exp(acs))

    head_of_row = lax.broadcasted_iota(jnp.int32, (SSM_WIDTH, LANES), 0) >> HEAD_SHIFT
    sel = head_of_row == lax.broadcasted_iota(jnp.int32, (SSM_WIDTH, LANES), 1)
    rowdec = jnp.sum(jnp.where(sel, jnp.exp(last), 0.0), axis=1, keepdims=True)

    lane = lax.broadcasted_iota(jnp.int32, (T, PAIR), 1)
    ys = []
    for p in range(SSM_HEADS // 2):
        g = (2 * p) // (SSM_HEADS // SSM_GROUPS)
        bm = xa[:, SSM_WIDTH + g * SSM_STATE:SSM_WIDTH + (g + 1) * SSM_STATE].astype(BF16)
        cm = xa[:, SSM_WIDTH + (SSM_GROUPS + g) * SSM_STATE:
                SSM_WIDTH + (SSM_GROUPS + g + 1) * SSM_STATE].astype(BF16)
        cb = _dot_nt(cm, bm)
        cols = slice(p * PAIR, (p + 1) * PAIR)
        xdt_p = xdt[:, cols].astype(BF16)
        yd = []
        for hh in (2 * p, 2 * p + 1):
            seg = acs[:, hh:hh + 1] - acs_t[hh:hh + 1, :]
            lmat = jnp.exp(jnp.where(causal, seg, -jnp.inf))
            yd.append(_dot((cb * lmat).astype(BF16), xdt_p))
        y_diag = jnp.where(lane < SSM_HEAD_DIM, yd[0], yd[1])
        st = st_ref[cols, :]
        y_off = _dot_nt(cm, st.astype(BF16)) * eacs[:, cols]
        ys.append(y_diag + y_off)
        st_ref[cols, :] = st * rowdec[cols, :] + _dot(xdd[:, cols].T.astype(BF16), bm)
    y = jnp.concatenate(ys, axis=-1) + xs * dsk_ref[...]
    y = y * _silu(zb)
    half = SSM_WIDTH // SSM_GROUPS
    outs = []
    for g in range(SSM_GROUPS):
        yg = y[:, g * half:(g + 1) * half]
        outs.append(yg * lax.rsqrt(jnp.mean(yg * yg, -1, keepdims=True) + EPS))
    return jnp.concatenate(outs, axis=-1) * nw_ref[...]


def _ssd(xbc, z, dt_raw, conv_prev8, h0, conv_w, conv_b, dtb, alog, dsk, nw, *, t_in, T, rb):
    B, S, _ = xbc.shape
    row = lambda w: pl.BlockSpec((rb, t_in, w), lambda b, j: (b, j, 0))
    per_b = lambda r, w: pl.BlockSpec((rb, r, w), lambda b, j: (b, 0, 0))
    const = lambda r, w: pl.BlockSpec((r, w), lambda b, j: (0, 0))
    return pl.pallas_call(
        functools.partial(_ssd_kernel, t_in=t_in, T=T),
        grid=(B // rb, S // t_in),
        in_specs=[row(CONV_DIM), row(SSM_WIDTH), row(LANES),
                  per_b(8, CONV_DIM), per_b(SSM_WIDTH, SSM_STATE),
                  const(CONV_W, CONV_DIM), const(1, CONV_DIM), const(1, LANES), const(1, LANES),
                  const(1, SSM_WIDTH), const(1, SSM_WIDTH)],
        out_specs=[row(SSM_WIDTH), per_b(SSM_WIDTH, SSM_STATE)],
        out_shape=[jax.ShapeDtypeStruct((B, S, SSM_WIDTH), BF16),
                   jax.ShapeDtypeStruct((B, SSM_WIDTH, SSM_STATE), F32)],
        scratch_shapes=[pltpu.VMEM((rb, 8, CONV_DIM), F32), pltpu.VMEM((rb, SSM_WIDTH, SSM_STATE), F32)],
        compiler_params=_cparams("arbitrary", "arbitrary"),
        name="ssd",
    )(xbc, z, dt_raw, conv_prev8, h0, conv_w, conv_b, dtb, alog, dsk, nw)


def _router(h, rwt_ref, rb_ref):
    tm = h.shape[0]
    G = N_EXPERT_GROUPS
    logits = _dot3(rwt_ref[...], h, dot=_dot_nt)
    score = jax.nn.sigmoid(logits[:N_EXPERTS, :])
    biased = score + rb_ref[:N_EXPERTS, :]
    sc = [score[G * j:G * (j + 1), :] for j in range(EXPERTS_PER_GROUP)]
    v = [biased[G * j:G * (j + 1), :] for j in range(EXPERTS_PER_GROUP)]

    def tree(op, xs):
        while len(xs) > 1:
            xs = [op(xs[a], xs[a + 1]) for a in range(0, len(xs) - 1, 2)] + ([xs[-1]] if len(xs) % 2 else [])
        return xs[0]

    m1 = tree(jnp.maximum, v)
    first = tree(jnp.minimum, [jnp.where(v[j] == m1, float(j), float(EXPERTS_PER_GROUP))
                               for j in range(EXPERTS_PER_GROUP)])
    m2 = tree(jnp.maximum, [jnp.where(first == float(j), -jnp.inf, v[j]) for j in range(EXPERTS_PER_GROUP)])
    grp = m1 + m2

    gid = lax.broadcasted_iota(jnp.int32, (G, tm), 0)
    cnt = jnp.zeros((G, tm), jnp.int32)
    for g2 in range(G):
        row = grp[g2:g2 + 1, :]
        beats = (row > grp) | ((row == grp) & (gid > g2))
        cnt = cnt + beats.astype(jnp.int32)
    gmask = cnt < TOPK_GROUPS

    idx = [(gid * EXPERTS_PER_GROUP + j).astype(F32) for j in range(EXPERTS_PER_GROUP)]
    mv = [jnp.where(gmask, v[j], -jnp.inf) for j in range(EXPERTS_PER_GROUP)]
    w = [jnp.zeros((G, tm), F32) for _ in range(EXPERTS_PER_GROUP)]
    for _ in range(TOP_K):
        top = jnp.max(tree(jnp.maximum, mv), axis=0, keepdims=True)
        who = jnp.min(tree(jnp.minimum, [jnp.where(mv[j] == top, idx[j], float(N_EXPERTS))
                                         for j in range(EXPERTS_PER_GROUP)]), axis=0, keepdims=True)
        for j in range(EXPERTS_PER_GROUP):
            hit = idx[j] == who
            w[j] = jnp.where(hit, sc[j], w[j])
            mv[j] = jnp.where(hit, -jnp.inf, mv[j])
    tot = w[0]
    for j in range(1, EXPERTS_PER_GROUP):
        tot = tot + w[j]
    tot = jnp.sum(tot, axis=0, keepdims=True)
    gates = [w[j] / tot * ROUTED_SCALE for j in range(EXPERTS_PER_GROUP)]
    shared = jnp.where(lax.broadcasted_iota(jnp.int32, (LANES - N_EXPERTS, tm), 0) == 0, 1.0, 0.0)
    by_slot = jnp.concatenate(gates + [shared], axis=0)
    e_id = lax.broadcasted_iota(jnp.int32, (LANES, LANES), 0)
    r_id = lax.broadcasted_iota(jnp.int32, (LANES, LANES), 1)
    src_row = jnp.where(e_id < N_EXPERTS,
                        (e_id & (EXPERTS_PER_GROUP - 1)) * G + (e_id >> SLOT_SHIFT), e_id)
    perm = jnp.where(r_id == src_row, 1.0, 0.0).astype(BF16)
    return _dot_exact_lhs(perm, by_slot)


def _mix_kernel(x_ref, att_ref, ssm_ref, g1_ref, sc2_ref, sh2_ref, wo_ref, l1g_ref, l1b_ref, rwt_ref, rb_ref,
                x1_ref, h2_ref, gate_ref, *, bb, ts, alpha):
    tm = bb * ts
    att = att_ref[...].reshape(tm, DA_WIDTH)
    ssm = ssm_ref[...].reshape(tm, SSM_WIDTH)
    mix = _dot(att, wo_ref[:DA_WIDTH, :]) + _dot(ssm, wo_ref[DA_WIDTH:, :])
    y = alpha * x_ref[...] + (1.0 + g1_ref[...]) * mix.reshape(bb, ts, D_MODEL)
    x1 = _ln(y) * l1g_ref[...] + l1b_ref[...]
    x1_ref[...] = x1
    h2 = (_ln(x1) * (1.0 + sc2_ref[...]) + sh2_ref[...]).reshape(tm, D_MODEL)
    h2_ref[...] = h2.astype(BF16)
    gate_ref[...] = _router(h2, rwt_ref, rb_ref).T


def _mix(x, att, ssm, mod4, wo, l1g, l1b, rwt, rb, *, bb, ts, alpha):
    B, S, _ = x.shape
    tm = bb * ts
    row3 = lambda w: pl.BlockSpec((bb, ts, w), lambda i, b: (b, i, 0))
    modspec = lambda which: pl.BlockSpec((bb, None, 1, D_MODEL), lambda i, b: (b, which, 0, 0))
    const = lambda r, w: pl.BlockSpec((r, w), lambda i, b: (0, 0))
    flat = lambda w: pl.BlockSpec((tm, w), lambda i, b: (b * (S // ts) + i, 0))
    return pl.pallas_call(
        functools.partial(_mix_kernel, bb=bb, ts=ts, alpha=alpha),
        grid=(S // ts, B // bb),
        in_specs=[row3(D_MODEL), row3(DA_WIDTH), row3(SSM_WIDTH), modspec(2), modspec(4), modspec(3),
                  const(D_MODEL, D_MODEL), const(1, D_MODEL), const(1, D_MODEL),
                  const(LANES, D_MODEL), const(LANES, 1)],
        out_specs=[row3(D_MODEL), flat(D_MODEL), flat(LANES)],
        out_shape=[jax.ShapeDtypeStruct((B, S, D_MODEL), F32),
                   jax.ShapeDtypeStruct((B * S, D_MODEL), BF16),
                   jax.ShapeDtypeStruct((B * S, LANES), F32)],
        compiler_params=_cparams("arbitrary", "arbitrary"),
        name="mix",
    )(x, att, ssm, mod4, mod4, mod4, wo, l1g, l1b, rwt, rb)


N_EXPERTS_ALL = N_EXPERTS + 1
EXPERTS_PER_STEP = 5


def _moe_kernel(h_ref, gate_ref, x1_ref, g2_ref, wg_ref, wu_ref, wd_ref,
                l2g_ref, l2b_ref, o_ref, acc_ref, *, bb, ts, alpha):
    tm = bb * ts
    e = pl.program_id(2)
    h = h_ref[...]

    @pl.when(e == 0)
    def _():
        acc_ref[...] = jnp.zeros_like(acc_ref)

    gate = gate_ref[...]
    lane = lax.broadcasted_iota(jnp.int32, (tm, LANES), 1)
    hids = []
    for s in range(EXPERTS_PER_STEP):
        col = e * EXPERTS_PER_STEP + s
        gcol = jnp.sum(jnp.where(lane == col, gate, 0.0), axis=1, keepdims=True)
        hid = _silu(_dot(h, wg_ref[s])) * _dot(h, wu_ref[s]) * gcol
        hids.append(hid.astype(BF16))
    hid = jnp.concatenate(hids, axis=-1)
    acc_ref[...] += _dot(hid, wd_ref[...].reshape(EXPERTS_PER_STEP * EXPERT_DIM, D_MODEL))

    @pl.when(e == pl.num_programs(2) - 1)
    def _():
        ffn = acc_ref[...].reshape(bb, ts, D_MODEL)
        y = alpha * x1_ref[...] + (1.0 + g2_ref[...]) * ffn
        o_ref[...] = _ln(y) * l2g_ref[...] + l2b_ref[...]


def _moe(h2, gate, x1, mod4, wg, wu, wd, l2g, l2b, *, bb, ts, alpha):
    B, S, _ = x1.shape
    tm = bb * ts
    row3 = lambda w: pl.BlockSpec((bb, ts, w), lambda i, b, e: (b, i, 0))
    flat = lambda w: pl.BlockSpec((tm, w), lambda i, b, e: (b * (S // ts) + i, 0))
    const = lambda r, w: pl.BlockSpec((r, w), lambda i, b, e: (0, 0))
    ew_in = pl.BlockSpec((EXPERTS_PER_STEP, D_MODEL, EXPERT_DIM), lambda i, b, e: (e, 0, 0))
    ew_out = pl.BlockSpec((EXPERTS_PER_STEP, EXPERT_DIM, D_MODEL), lambda i, b, e: (e, 0, 0))
    return pl.pallas_call(
        functools.partial(_moe_kernel, bb=bb, ts=ts, alpha=alpha),
        grid=(S // ts, B // bb, N_EXPERTS_ALL // EXPERTS_PER_STEP),
        in_specs=[flat(D_MODEL), flat(LANES), row3(D_MODEL),
                  pl.BlockSpec((bb, None, 1, D_MODEL), lambda i, b, e: (b, 5, 0, 0)),
                  ew_in, ew_in, ew_out,
                  const(1, D_MODEL), const(1, D_MODEL)],
        out_specs=row3(D_MODEL),
        out_shape=jax.ShapeDtypeStruct((B, S, D_MODEL), F32),
        scratch_shapes=[pltpu.VMEM((tm, D_MODEL), F32)],
        compiler_params=_cparams("arbitrary", "arbitrary", "arbitrary"),
        name="moe",
    )(h2, gate, x1, mod4, wg, wu, wd, l2g, l2b)


def _rope_tables(pos):
    half = DA_HEAD_DIM // 2
    inv = ROPE_THETA ** (-jnp.arange(half, dtype=F32) / half)
    ang = pos.astype(F32)[:, None] * inv[None, :]
    cos = jnp.cos(ang)
    sin = jnp.sin(ang)
    reps = LANES // DA_HEAD_DIM
    cos_t = jnp.tile(jnp.concatenate([cos, cos], -1), (1, reps))
    sin_t = jnp.tile(jnp.concatenate([-sin, sin], -1), (1, reps))
    return cos_t, sin_t


def _layer(x, mod4, pos, prm, layer_idx, depth, k_past, v_past, conv_prev, ssm_prev, sample):
    B, S, _ = x.shape
    alpha = (2 * depth) ** 0.25
    lam_init = 0.8 - 0.6 * math.exp(-0.3 * layer_idx)
    cos_t, sin_t = _rope_tables(pos)
    if sample:
        bb, ts = B, S
        cos_t = jnp.tile(cos_t, (B, 1))
        sin_t = jnp.tile(sin_t, (B, 1))
    else:
        bb, ts = 1, min(512, S)
    ssd_consts = (prm['conv_w'], prm['conv_b'], prm['dt_bias'], prm['a_log'], prm['d_skip'], prm['ssm_norm_w'])
    if sample:
        qb, k, kb, v, vb, z, xbc, dt_raw = _proj(x, mod4, cos_t, sin_t, prm['w_main'], prm['w_dt'],
                                                 bb=bb, ts=ts, v_transposed=False)
        att = _attn_sample(qb, kb, vb, k_past, v_past, prm['lam_vecs'], prm['subln_w'], lam_init)
        conv_prev8 = jnp.pad(conv_prev, ((0, 0), (8 - (CONV_W - 1), 0), (0, 0)))
        h0 = ssm_prev.reshape(B, SSM_WIDTH, SSM_STATE)
        ssm, h_last = _ssd(xbc, z, dt_raw, conv_prev8, h0, *ssd_consts, t_in=S, T=max(S, LANES),
                           rb=math.gcd(B, 4))
        new_conv = jnp.concatenate([conv_prev, xbc], axis=1)[:, -(CONV_W - 1):]
    else:
        qb, k, kb, v, vb, ssm, h_last, tail = _proj_ssd(x, mod4, cos_t, sin_t, prm['w_main'], prm['w_dt'],
                                                        *ssd_consts, ts=ts, T=min(256, S))
        att = _attn_prompt(qb, kb, vb, prm['lam_vecs'], prm['subln_w'], lam_init, tq=min(512, S))
        new_conv = tail[:, -(CONV_W - 1):]
    if not sample:
        ts = min(1024, S)
    x1, h2, gate = _mix(x, att, ssm, mod4, prm['w_out'], prm['ln1_g'], prm['ln1_b'], prm['router_wt'],
                        prm['router_b'], bb=bb, ts=ts, alpha=alpha)
    y = _moe(h2, gate, x1, mod4, prm['wg'], prm['wu'], prm['wd'],
             prm['ln2_g'], prm['ln2_b'], bb=bb, ts=ts, alpha=alpha)
    return (y, k.reshape(B, S, DA_HEADS, 2 * DA_HEAD_DIM), v.reshape(B, S, DA_HEADS, 2 * DA_HEAD_DIM),
            new_conv, h_last.reshape(B, SSM_HEADS, SSM_HEAD_DIM, SSM_STATE))


CAST_EXPERTS = 8


def _cast_kernel(w_ref, s_ref, o_ref):
    last = pl.num_programs(0) - 1

    @pl.when(pl.program_id(0) < last)
    def _():
        o_ref[...] = w_ref[...].astype(BF16)

    @pl.when(pl.program_id(0) == last)
    def _():
        o_ref[0] = s_ref[...].astype(BF16)


def _cast_with_shared(w, s):
    n, r, c = w.shape
    full = n // CAST_EXPERTS
    return pl.pallas_call(
        _cast_kernel,
        grid=(full + 1,),
        in_specs=[pl.BlockSpec((CAST_EXPERTS, r, c), lambda e: (jnp.minimum(e, full - 1), 0, 0)),
                  pl.BlockSpec((r, c), lambda e: (0, 0))],
        out_specs=pl.BlockSpec((CAST_EXPERTS, r, c), lambda e: (e, 0, 0)),
        out_shape=jax.ShapeDtypeStruct((n + 1, r, c), BF16),
        compiler_params=_cparams("arbitrary"),
        name="cast_experts",
    )(w, s)


def _prep_params(w_in, lq1, lk1, lq2, lk2, subln_w, conv_w, conv_b, dt_bias, a_log, d_skip, ssm_norm_w, w_out,
                 ln1_g, ln1_b, router_w, router_bias, ewg, ewu, ewd, swg, swu, swd, ln2_g, ln2_b):
    pad8 = lambda a: jnp.pad(a.reshape(1, -1), ((0, 0), (0, LANES - a.shape[-1])))
    row = lambda a: a.reshape(1, -1)
    perm = (jnp.arange(N_EXPERTS) % N_EXPERT_GROUPS) * EXPERTS_PER_GROUP + jnp.arange(N_EXPERTS) // N_EXPERT_GROUPS
    rwt = jnp.pad(router_w.T[perm], ((0, LANES - N_EXPERTS), (0, 0)))
    rb = jnp.pad(router_bias[perm], (0, LANES - N_EXPERTS)).reshape(LANES, 1)
    with_shared = _cast_with_shared
    return {
        'w_main': w_in[:, :DT_OFF].astype(BF16),
        'w_dt': jnp.pad(w_in[:, DT_OFF:], ((0, 0), (0, LANES - SSM_HEADS))),
        'lam_vecs': (row(lq1), row(lk1), row(lq2), row(lk2)),
        'subln_w': row(subln_w),
        'conv_w': conv_w, 'conv_b': row(conv_b),
        'dt_bias': pad8(dt_bias), 'a_log': pad8(a_log),
        'd_skip': row(jnp.repeat(d_skip, SSM_HEAD_DIM)), 'ssm_norm_w': row(ssm_norm_w),
        'w_out': w_out.astype(BF16), 'ln1_g': row(ln1_g), 'ln1_b': row(ln1_b),
        'router_wt': rwt, 'router_b': rb,
        'wg': with_shared(ewg, swg), 'wu': with_shared(ewu, swu), 'wd': with_shared(ewd, swd),
        'ln2_g': row(ln2_g), 'ln2_b': row(ln2_b),
    }


def kernel(x_prompt, x_sample, c_prompt, c_sample, cache_k, cache_v, state_conv, state_ssm, w_ada, b_ada, w_in, lambda_q1, lambda_k1, lambda_q2, lambda_k2, subln_w, conv_w, conv_b, dt_bias, a_log, d_skip, ssm_norm_w, w_out, ln1_g, ln1_b, router_w, router_bias, exp_w_gate, exp_w_up, exp_w_down, sh_w_gate, sh_w_up, sh_w_down, ln2_g, ln2_b):
    depth = w_ada.shape[0]
    bp, sp, _ = x_prompt.shape
    bs, ss, _ = x_sample.shape
    past = cache_k.shape[2]
    pos_prompt = jnp.arange(sp)
    pos_sample = past + jnp.arange(ss)
    c_rows = bp + bs
    c_pad = (-c_rows) % 8
    c_all = jnp.pad(jnp.concatenate([c_prompt, c_sample], 0), ((0, c_pad), (0, 0)))
    yp, ys = x_prompt, x_sample
    outs = [[] for _ in range(8)]
    for l in range(depth):
        prm = _prep_params(w_in[l], lambda_q1[l], lambda_k1[l], lambda_q2[l], lambda_k2[l], subln_w[l],
                           conv_w[l], conv_b[l], dt_bias[l], a_log[l], d_skip[l], ssm_norm_w[l], w_out[l],
                           ln1_g[l], ln1_b[l], router_w[l], router_bias[l], exp_w_gate[l], exp_w_up[l],
                           exp_w_down[l], sh_w_gate[l], sh_w_up[l], sh_w_down[l], ln2_g[l], ln2_b[l])
        mod = _ada(c_all, w_ada[l], b_ada[l]).reshape(c_rows + c_pad, 6, 1, D_MODEL)
        yp, kp, vp, cp, hp = _layer(yp, mod[:bp], pos_prompt, prm, l, depth, None, None, None, None, False)
        ys, kn, vn, cn, hn = _layer(ys, mod[bp:c_rows], pos_sample, prm, l, depth,
                                    cache_k[l].reshape(bs, past * DA_HEADS, LANES),
                                    cache_v[l].reshape(bs, past * DA_HEADS, LANES),
                                    state_conv[l], state_ssm[l], True)
        for lst, val in zip(outs, (kp, vp, cp, hp, kn, vn, cn, hn)):
            lst.append(val)
    return (yp, ys) + tuple(jnp.stack(o) for o in outs)
```
